```python
import jax, jax.numpy as jnp
from jax import lax
import numpy as np

D_MODEL = 4096
BATCH = 4
SEQ = 2048
DEPTH = 2
DEC_BATCH = 8
DEC_SEQ = 4
PAST_LEN = 16384
PAGE_SIZE = 128

N_HEADS = 16
HEAD_DIM = 128
N_KV = 4
GROUP = N_HEADS // N_KV
ATTN_W = N_HEADS * HEAD_DIM
KV_W = N_KV * HEAD_DIM
ROPE_DIM = HEAD_DIM // 4
ROPE_THETA = 500000.0
L_CMP = 32
CMP_STRIDE = 16
L_SEL = 64
N_SEL = 16
WINDOW = 512
N_BRANCH = 3
Q_BLOCK = 32
CONF_CH = D_MODEL // 2
CONF_W = 31
SC_CH = D_MODEL
SC_W = 3
D_FF = -(-8 * D_MODEL // (3 * 256)) * 256
PLE_DIM = 256
RMS_EPS = 1e-6
LN_EPS = 1e-5
NEG = -1e30
FORCE = 1e9
N_EVEN = (DEPTH + 1) // 2
N_ODD = DEPTH // 2
IN0 = 2 * CONF_CH + ATTN_W + 6 * KV_W + N_HEADS * N_BRANCH

kernel_name = "hybrid_conformer_nsa_shortconv_decode_step"


def rmsnorm(x, g):
    xf = x.astype(jnp.float32)
    y = xf * lax.rsqrt(jnp.mean(xf * xf, axis=-1, keepdims=True) + RMS_EPS)
    return (y * g.astype(jnp.float32)).astype(x.dtype)


def layernorm(x, g, b):
    xf = x.astype(jnp.float32)
    mu = jnp.mean(xf, axis=-1, keepdims=True)
    var = jnp.mean(jnp.square(xf - mu), axis=-1, keepdims=True)
    y = (xf - mu) * lax.rsqrt(var + LN_EPS)
    return (y * g.astype(jnp.float32) + b.astype(jnp.float32)).astype(x.dtype)


def rope(x, pos):
    half = ROPE_DIM // 2
    inv = ROPE_THETA ** (-2.0 * jnp.arange(half, dtype=jnp.float32) / ROPE_DIM)
    ang = pos.astype(jnp.float32)[:, None] * inv[None, :]
    shape = (pos.shape[0],) + (1,) * (x.ndim - 3) + (half,)
    cos = jnp.cos(ang).reshape(shape).astype(x.dtype)
    sin = jnp.sin(ang).reshape(shape).astype(x.dtype)
    x1 = x[..., :half]
    x2 = x[..., half:ROPE_DIM]
    return jnp.concatenate([x1 * cos - x2 * sin, x1 * sin + x2 * cos, x[..., ROPE_DIM:]], axis=-1)


def causal_dwconv(x, w, prev):
    width = w.shape[0]
    xp = jnp.concatenate([prev.astype(x.dtype), x], axis=1)
    y = lax.conv_general_dilated(xp, w[:, None, :].astype(x.dtype), window_strides=(1,), padding='VALID',
                                 dimension_numbers=('NWC', 'WIO', 'NWC'), feature_group_count=x.shape[-1])
    return y, xp[:, xp.shape[1] - (width - 1):]


def masked_softmax(s, mask):
    p = jax.nn.softmax(jnp.where(mask, s.astype(jnp.float32), NEG), axis=-1)
    return p * mask


def compress(k, pe, w1, b1, w2):
    B, T, G, _ = k.shape
    r = L_CMP // CMP_STRIDE
    n_sub = T // CMP_STRIDE
    n_blk = n_sub - r + 1
    ks = k[:, :n_sub * CMP_STRIDE].reshape(B, n_sub, CMP_STRIDE, G, HEAD_DIM)
    pre = b1 + jnp.einsum('sd,sde->e', pe, w1)
    for j in range(r):
        w1j = w1[j * CMP_STRIDE:(j + 1) * CMP_STRIDE]
        pre = pre + jnp.einsum('bnsgd,sde->bnge', ks[:, j:j + n_blk], w1j)
    return jnp.einsum('bnge,ef->bngf', jax.nn.gelu(pre), w2)


def sel_blocks(k):
    B, T, G, _ = k.shape
    nsb = -(-T // L_SEL)
    k = jnp.pad(k, ((0, 0), (0, nsb * L_SEL - T), (0, 0), (0, 0)))
    return k.reshape(B, nsb, L_SEL, G, HEAD_DIM).transpose(0, 3, 1, 2, 4)


def nsa_attend(q, q_pos, gate, kc_blk, vc_blk, cmp_end, ks_blk, vs_blk, kw, vw, w_pos):
    B, Tq = q.shape[:2]
    dt = q.dtype
    qg = q.reshape(B, Tq, N_KV, GROUP, HEAD_DIM)
    scale = HEAD_DIM ** -0.5
    s = jnp.einsum('btgrd,bngd->btgrn', qg, kc_blk) * scale
    cmask = (cmp_end[None, :] <= q_pos[:, None])[None, :, None, None, :]
    p_cmp = masked_softmax(s, cmask)
    o_cmp = jnp.einsum('btgrn,bngd->btgrd', p_cmp.astype(dt), vc_blk)
    nc, nsb = kc_blk.shape[1], ks_blk.shape[2]
    cs = jnp.arange(nc, dtype=jnp.int32) * CMP_STRIDE
    js = jnp.arange(nsb, dtype=jnp.int32) * L_SEL
    overlap = ((cs[:, None] < js[None, :] + L_SEL) & (cs[:, None] + L_CMP > js[None, :])).astype(jnp.float32)
    imp = jnp.einsum('btgrn,nj->btgj', p_cmp, overlap)
    jj = jnp.arange(nsb, dtype=jnp.int32)[None, :]
    cur = (q_pos // L_SEL)[:, None]
    valid = jj * L_SEL <= q_pos[:, None]
    forced = (jj == 0) | (jj == cur) | (jj == cur - 1)
    score = jnp.where(forced[None, :, None, :], FORCE, jnp.where(valid[None, :, None, :], imp, NEG))
    n_top = min(N_SEL, nsb)
    top_v, top_i = lax.top_k(score, n_top)
    sel_ok = top_v > 0.5 * NEG
    idx = top_i.transpose(0, 2, 1, 3)
    bi = jnp.arange(B)[:, None, None, None]
    gi = jnp.arange(N_KV)[None, :, None, None]
    k_sel = ks_blk[bi, gi, idx]
    v_sel = vs_blk[bi, gi, idx]
    s = jnp.einsum('btgrd,bgtkld->btgrkl', qg, k_sel) * scale
    kpos = top_i[..., None] * L_SEL + jnp.arange(L_SEL, dtype=jnp.int32)
    smask = (kpos <= q_pos[None, :, None, None, None]) & sel_ok[..., None]
    p = masked_softmax(s.reshape(B, Tq, N_KV, GROUP, n_top * L_SEL),
                       smask.reshape(B, Tq, N_KV, 1, n_top * L_SEL))
    o_slc = jnp.einsum('btgrkl,bgtkld->btgrd', p.reshape(s.shape).astype(dt), v_sel)
    s = jnp.einsum('btgrd,bsgd->btgrs', qg, kw) * scale
    wmask = ((w_pos[None, :] >= 0) & (w_pos[None, :] <= q_pos[:, None])
             & (w_pos[None, :] > q_pos[:, None] - WINDOW))[None, :, None, None, :]
    o_win = jnp.einsum('btgrs,bsgd->btgrd', masked_softmax(s, wmask).astype(dt), vw)
    g = jax.nn.sigmoid(gate.astype(jnp.float32)).reshape(B, Tq, N_KV, GROUP, N_BRANCH).astype(dt)
    o = g[..., 0:1] * o_cmp + g[..., 1:2] * o_slc + g[..., 2:3] * o_win
    return o.reshape(B, Tq, ATTN_W)


def prompt_nsa(q, gate, kc_blk, vc_blk, cmp_end, ks_blk, vs_blk, win):
    B, T = q.shape[:2]
    kw = jnp.pad(win[:, :, 0], ((0, 0), (WINDOW, 0), (0, 0), (0, 0)))
    vw = jnp.pad(win[:, :, 1], ((0, 0), (WINDOW, 0), (0, 0), (0, 0)))

    def block(n):
        t0 = n * Q_BLOCK
        q_pos = t0 + jnp.arange(Q_BLOCK, dtype=jnp.int32)
        w_pos = t0 - WINDOW + jnp.arange(Q_BLOCK + WINDOW, dtype=jnp.int32)
        return nsa_attend(lax.dynamic_slice_in_dim(q, t0, Q_BLOCK, axis=1), q_pos,
                          lax.dynamic_slice_in_dim(gate, t0, Q_BLOCK, axis=1),
                          kc_blk, vc_blk, cmp_end, ks_blk, vs_blk,
                          lax.dynamic_slice_in_dim(kw, t0, Q_BLOCK + WINDOW, axis=1),
                          lax.dynamic_slice_in_dim(vw, t0, Q_BLOCK + WINDOW, axis=1), w_pos)

    out = lax.map(block, jnp.arange(T // Q_BLOCK, dtype=jnp.int32))
    return out.transpose(1, 0, 2, 3).reshape(B, T, ATTN_W)


def even_mixer(x, pos, w, past):
    g_mix, w_in, conv_w, conv_b, ln_g, ln_b, cmp_pe, cmp_w1, cmp_b1, cmp_w2, w_out = w
    B, T, _ = x.shape
    z = rmsnorm(x, g_mix) @ w_in
    o1 = CONF_CH
    o2 = 2 * CONF_CH
    o3 = o2 + ATTN_W
    o4 = o3 + 6 * KV_W
    a_val, a_gate, q, kv, gl = z[..., :o1], z[..., o1:o2], z[..., o2:o3], z[..., o3:o4], z[..., o4:]
    u = a_val * jax.nn.sigmoid(a_gate)
    prev = jnp.zeros((B, CONF_W - 1, CONF_CH), x.dtype) if past is None else past[3]
    c, conf_state = causal_dwconv(u, conv_w, prev)
    a_out = jax.nn.silu(layernorm(c + conv_b, ln_g, ln_b))
    q = rope(q.reshape(B, T, N_HEADS, HEAD_DIM), pos)
    kv = kv.reshape(B, T, 6, N_KV, HEAD_DIM)
    kv = kv.at[:, :, 0::2].set(rope(kv[:, :, 0::2], pos))
    new_cmp, new_slc, new_win = kv[:, :, 0:2], kv[:, :, 2:4], kv[:, :, 4:6]
    gate = gl.reshape(B, T, N_HEADS, N_BRANCH)
    if past is None:
        full_cmp, full_slc = new_cmp, new_slc
    else:
        full_cmp = jnp.concatenate([past[0], new_cmp], axis=1)
        full_slc = jnp.concatenate([past[1], new_slc], axis=1)
    kc_blk = compress(full_cmp[:, :, 0], cmp_pe[0], cmp_w1[0], cmp_b1[0], cmp_w2[0])
    vc_blk = compress(full_cmp[:, :, 1], cmp_pe[1], cmp_w1[1], cmp_b1[1], cmp_w2[1])
    cmp_end = jnp.arange(kc_blk.shape[1], dtype=jnp.int32) * CMP_STRIDE + (L_CMP - 1)
    ks_blk = sel_blocks(full_slc[:, :, 0])
    vs_blk = sel_blocks(full_slc[:, :, 1])
    if past is None:
        attn = prompt_nsa(q, gate, kc_blk, vc_blk, cmp_end, ks_blk, vs_blk, new_win)
        win_state = new_win[:, T - min(WINDOW, T):]
    else:
        wb = past[2].shape[1]
        win_all = jnp.concatenate([past[2], new_win], axis=1)
        w_pos = pos[0] - wb + jnp.arange(wb + T, dtype=jnp.int32)
        attn = nsa_attend(q, pos, gate, kc_blk, vc_blk, cmp_end, ks_blk, vs_blk,
                          win_all[:, :, 0], win_all[:, :, 1], w_pos)
        win_state = win_all[:, win_all.shape[1] - wb:]
    out = jnp.concatenate([a_out, attn], axis=-1) @ w_out
    return out, new_cmp, new_slc, win_state, conf_state


def odd_mixer(x, w, prev):
    g_mix, w_in, conv_w, w_out = w
    z = rmsnorm(x, g_mix) @ w_in
    b_g, c_g, v = z[..., :SC_CH], z[..., SC_CH:2 * SC_CH], z[..., 2 * SC_CH:]
    conv, state = causal_dwconv(c_g * v, conv_w, prev)
    return (b_g * conv) @ w_out, state


def ffn_ple(x, p, g_ffn, w_gate, w_up, w_down, g_ple, w_pg, w_pp):
    h = rmsnorm(x, g_ffn)
    x = x + (jax.nn.silu(h @ w_gate) * (h @ w_up)) @ w_down
    return x + jax.nn.sigmoid(rmsnorm(x, g_ple) @ w_pg) * (p @ w_pp)


def setup_inputs(seed: int = 0) -> dict:
    key = jax.random.key(seed)
    keys = iter(jax.random.split(key, 48))
    f32 = jnp.float32

    def nrm(shape, scale=1.0):
        return jax.random.normal(next(keys), shape, f32) * scale

    def gain(shape):
        return 1.0 + nrm(shape, 0.01)

    n_pages = PAST_LEN // PAGE_SIZE
    n_pool = (DEC_BATCH * n_pages * 5) // 4
    win_buf = min(WINDOW, PAST_LEN)
    page_table = jax.random.permutation(next(keys), n_pool)[:DEC_BATCH * n_pages]
    page_table = page_table.reshape(DEC_BATCH, n_pages).astype(jnp.int32)
    return {
        "x_prompt": nrm((BATCH, SEQ, D_MODEL)),
        "x_sample": nrm((DEC_BATCH, DEC_SEQ, D_MODEL)),
        "cache_cmp_kv": nrm((N_EVEN, n_pool, PAGE_SIZE, 2, N_KV, HEAD_DIM)),
        "cache_slc_kv": nrm((N_EVEN, n_pool, PAGE_SIZE, 2, N_KV, HEAD_DIM)),
        "cache_win_kv": nrm((N_EVEN, DEC_BATCH, win_buf, 2, N_KV, HEAD_DIM)),
        "state_conf_conv": nrm((N_EVEN, DEC_BATCH, CONF_W - 1, CONF_CH)),
        "state_short_conv": nrm((N_ODD, DEC_BATCH, SC_W - 1, SC_CH)),
        "page_table": page_table,
        "p_prompt": nrm((DEPTH, BATCH, SEQ, PLE_DIM)),
        "p_sample": nrm((DEPTH, DEC_BATCH, DEC_SEQ, PLE_DIM)),
        "g_mix0": gain((N_EVEN, D_MODEL)),
        "w_in0": nrm((N_EVEN, D_MODEL, IN0), D_MODEL ** -0.5),
        "conv_w0": nrm((N_EVEN, CONF_W, CONF_CH), CONF_W ** -0.5),
        "conv_b0": nrm((N_EVEN, CONF_CH), 0.01),
        "ln_g0": gain((N_EVEN, CONF_CH)),
        "ln_b0": nrm((N_EVEN, CONF_CH), 0.01),
        "cmp_pe": nrm((N_EVEN, 2, L_CMP, HEAD_DIM), 0.1),
        "cmp_w1": nrm((N_EVEN, 2, L_CMP, HEAD_DIM, HEAD_DIM), (L_CMP * HEAD_DIM) ** -0.5),
        "cmp_b1": nrm((N_EVEN, 2, HEAD_DIM), 0.01),
        "cmp_w2": nrm((N_EVEN, 2, HEAD_DIM, HEAD_DIM), HEAD_DIM ** -0.5),
        "w_out0": nrm((N_EVEN, CONF_CH + ATTN_W, D_MODEL), (CONF_CH + ATTN_W) ** -0.5),
        "g_mix1": gain((N_ODD, D_MODEL)),
        "w_in1": nrm((N_ODD, D_MODEL, 3 * SC_CH), D_MODEL ** -0.5),
        "sconv_w1": nrm((N_ODD, SC_W, SC_CH), SC_W ** -0.5),
        "w_out1": nrm((N_ODD, SC_CH, D_MODEL), SC_CH ** -0.5),
        "g_ffn": gain((DEPTH, D_MODEL)),
        "w_ffn_gate": nrm((DEPTH, D_MODEL, D_FF), D_MODEL ** -0.5),
        "w_ffn_up": nrm((DEPTH, D_MODEL, D_FF), D_MODEL ** -0.5),
        "w_ffn_down": nrm((DEPTH, D_FF, D_MODEL), D_FF ** -0.5),
        "g_ple": gain((DEPTH, D_MODEL)),
        "w_ple_gate": nrm((DEPTH, D_MODEL, D_MODEL), D_MODEL ** -0.5),
        "w_ple_proj": nrm((DEPTH, PLE_DIM, D_MODEL), PLE_DIM ** -0.5),
        "g_final": gain((D_MODEL,)),
    }


def reference(x_prompt, x_sample, cache_cmp_kv, cache_slc_kv, cache_win_kv, state_conf_conv,
              state_short_conv, page_table, p_prompt, p_sample, g_mix0, w_in0, conv_w0, conv_b0,
              ln_g0, ln_b0, cmp_pe, cmp_w1, cmp_b1, cmp_w2, w_out0, g_mix1, w_in1, sconv_w1, w_out1,
              g_ffn, w_ffn_gate, w_ffn_up, w_ffn_down, g_ple, w_ple_gate, w_ple_proj, g_final):
    db = x_sample.shape[0]
    pos_p = jnp.arange(x_prompt.shape[1], dtype=jnp.int32)
    pos_s = PAST_LEN + jnp.arange(x_sample.shape[1], dtype=jnp.int32)
    xp, xs = x_prompt, x_sample
    cmp_p, cmp_s, slc_p, slc_s, win_p, win_s, conf_p, conf_s, sc_p, sc_s = ([] for _ in range(10))
    for i in range(DEPTH):
        if i % 2 == 0:
            e = i // 2
            w = (g_mix0[e], w_in0[e], conv_w0[e], conv_b0[e], ln_g0[e], ln_b0[e],
                 cmp_pe[e], cmp_w1[e], cmp_b1[e], cmp_w2[e], w_out0[e])
            dp, c_p, s_p, wi_p, cf_p = even_mixer(xp, pos_p, w, None)
            past_cmp = cache_cmp_kv[e][page_table].reshape(db, -1, 2, N_KV, HEAD_DIM)
            past_slc = cache_slc_kv[e][page_table].reshape(db, -1, 2, N_KV, HEAD_DIM)
            ds, c_s, s_s, wi_s, cf_s = even_mixer(
                xs, pos_s, w, (past_cmp, past_slc, cache_win_kv[e], state_conf_conv[e]))
            cmp_p.append(c_p); cmp_s.append(c_s); slc_p.append(s_p); slc_s.append(s_s)
            win_p.append(wi_p); win_s.append(wi_s); conf_p.append(cf_p); conf_s.append(cf_s)
        else:
            o = i // 2
            w = (g_mix1[o], w_in1[o], sconv_w1[o], w_out1[o])
            dp, st_p = odd_mixer(xp, w, jnp.zeros((xp.shape[0], SC_W - 1, SC_CH), xp.dtype))
            ds, st_s = odd_mixer(xs, w, state_short_conv[o])
            sc_p.append(st_p); sc_s.append(st_s)
        xp = xp + dp
        xs = xs + ds
        fw = (g_ffn[i], w_ffn_gate[i], w_ffn_up[i], w_ffn_down[i], g_ple[i], w_ple_gate[i], w_ple_proj[i])
        xp = ffn_ple(xp, p_prompt[i], *fw)
        xs = ffn_ple(xs, p_sample[i], *fw)
    y_prompt = rmsnorm(xp, g_final)
    y_sample = rmsnorm(xs, g_final)
    return (y_prompt, y_sample,
            jnp.stack(cmp_p), jnp.stack(cmp_s), jnp.stack(slc_p), jnp.stack(slc_s),
            jnp.stack(win_p), jnp.stack(win_s), jnp.stack(conf_p), jnp.stack(conf_s),
            jnp.stack(sc_p), jnp.stack(sc_s))
```

```python
import functools

import jax
import jax.numpy as jnp
from jax import lax
from jax.experimental import pallas as pl
from jax.experimental.pallas import tpu as pltpu

F32 = jnp.float32
BF16 = jnp.bfloat16

V7X_VMEM_BYTES = 64 * 1024 * 1024
LANES = 128
SUBLANES = 8

D_MODEL = 4096
PAST_LEN = 16384
PAGE_SIZE = 128
N_HEADS = 16
HEAD_DIM = 128
N_KV = 4
GROUP = N_HEADS // N_KV
ATTN_W = N_HEADS * HEAD_DIM
KV_W = N_KV * HEAD_DIM
ROPE_DIM = HEAD_DIM // 4
ROPE_THETA = 500000.0
L_CMP = 32
CMP_STRIDE = 16
L_SEL = 64
N_SEL = 16
WINDOW = 512
N_BRANCH = 3
CONF_CH = D_MODEL // 2
CONF_W = 31
SC_CH = D_MODEL
SC_W = 3
RMS_EPS = 1e-6
LN_EPS = 1e-5
NEG = -1e30
FORCE = 1e9
QKV_W = ATTN_W + 6 * KV_W
SUB_PER_PAGE = PAGE_SIZE // CMP_STRIDE
SUB_FLAT = CMP_STRIDE * HEAD_DIM
PAGE_ROW_W = 2 * KV_W
HALO = 32
S_PAD = 8


def _vmem_limit(n_bytes):
    return int(min(V7X_VMEM_BYTES - (4 << 20), max(n_bytes, 16 << 20)))


def _params(sem, vmem):
    return pltpu.CompilerParams(dimension_semantics=sem, vmem_limit_bytes=_vmem_limit(vmem))


def _dot(a, b):
    return jnp.dot(a, b, preferred_element_type=F32)


def _dot_nt(a, b):
    return lax.dot_general(a, b, (((1,), (1,)), ((), ())), preferred_element_type=F32)


def _sigmoid(x):
    return jax.nn.sigmoid(x)


def _rmsnorm_body(x_ref, g_ref, o_ref):
    x = x_ref[...]
    y = x * lax.rsqrt(jnp.mean(x * x, axis=-1, keepdims=True) + RMS_EPS)
    o_ref[...] = (y * g_ref[...]).astype(o_ref.dtype)


def rmsnorm(x, g, out_dtype, rows=256):
    m, d = x.shape
    tr = min(rows, m)
    return pl.pallas_call(
        _rmsnorm_body,
        grid=(m // tr,),
        in_specs=[pl.BlockSpec((tr, d), lambda i: (i, 0)),
                  pl.BlockSpec((1, d), lambda i: (0, 0))],
        out_specs=pl.BlockSpec((tr, d), lambda i: (i, 0)),
        out_shape=jax.ShapeDtypeStruct((m, d), out_dtype),
        compiler_params=_params(("arbitrary",), 6 * tr * d * 4),
    )(x, g.reshape(1, d))


class W:
    def __init__(self, arr, k_rows=None, row_blk=0, col_off=0):
        self.arr = arr
        self.k_rows = arr.shape[0] if k_rows is None else k_rows
        self.row_blk = row_blk
        self.col_off = col_off


def _mm_body(*refs, group_sizes, n_rx, n_tx, n_out, epilogue):
    it = iter(refs)
    groups = []
    for n_w in group_sizes:
        a_p = next(it)
        a_s = next(it)
        groups.append((a_p, a_s, [next(it) for _ in range(n_w)]))
    rx_p = [next(it) for _ in range(n_rx)]
    rx_s = [next(it) for _ in range(n_rx)]
    tx_p = [next(it) for _ in range(n_tx)]
    tx_s = [next(it) for _ in range(n_tx)]
    o_p = [next(it) for _ in range(n_out)]
    o_s = [next(it) for _ in range(n_out)]
    j = pl.program_id(0)
    wb = [[w[...].astype(BF16) for w in ws] for _, _, ws in groups]

    def run(which, rx, tx, outs):
        dots = []
        for (a_pp, a_ss, _), wbs in zip(groups, wb):
            a = (a_pp if which == 0 else a_ss)[...]
            dots.extend(_dot(a, w) for w in wbs)
        res = epilogue(dots, [r[...] for r in rx], [t[...] for t in tx], j)
        for o, r in zip(outs, res):
            o[...] = r.astype(o.dtype)

    run(0, rx_p, tx_p, o_p)

    @pl.when(pl.program_id(1) == 0)
    def _():
        run(1, rx_s, tx_s, o_s)


def fused_mm(groups, epilogue, out_dtypes, n_cols, tm, tn, rowx=(), tilex=(), rx_period=None):
    mp = groups[0][0].shape[0]
    ms = groups[0][1].shape[0]
    nj, ni = n_cols // tn, mp // tm
    assert nj * tn == n_cols and ni * tm == mp
    args, in_specs = [], []
    vmem = 0
    for a_p, a_s, ws in groups:
        k = a_p.shape[1]
        args += [a_p, a_s]
        in_specs += [pl.BlockSpec((tm, k), lambda j, i: (i, 0)),
                     pl.BlockSpec((ms, k), lambda j, i: (0, 0))]
        vmem += 2 * (tm + ms) * k * a_p.dtype.itemsize
        for w in ws:
            assert w.k_rows == k
            args.append(w.arr)
            in_specs.append(pl.BlockSpec(
                (k, tn), functools.partial(lambda j, i, rb, co: (rb, co + j), rb=w.row_blk, co=w.col_off)))
            vmem += k * tn * (2 * w.arr.dtype.itemsize + 2)
    for which in (0, 1):
        for tab_p, tab_s in rowx:
            if which == 0:
                per = tab_p.shape[0] // tm
                args.append(tab_p)
                in_specs.append(pl.BlockSpec(
                    (tm, tab_p.shape[1]), functools.partial(lambda j, i, per: (i % per, 0), per=per)))
            else:
                args.append(tab_s)
                in_specs.append(pl.BlockSpec(tab_s.shape, lambda j, i: (0, 0)))
    for which in (0, 1):
        for t_p, t_s in tilex:
            if which == 0:
                args.append(t_p)
                in_specs.append(pl.BlockSpec((tm, tn), lambda j, i: (i, j)))
            else:
                args.append(t_s)
                in_specs.append(pl.BlockSpec((ms, tn), lambda j, i: (0, j)))
    out_shape, out_specs = [], []
    for which in (0, 1):
        for dt in out_dtypes:
            if which == 0:
                out_shape.append(jax.ShapeDtypeStruct((mp, n_cols), dt))
                out_specs.append(pl.BlockSpec((tm, tn), lambda j, i: (i, j)))
            else:
                out_shape.append(jax.ShapeDtypeStruct((ms, n_cols), dt))
                out_specs.append(pl.BlockSpec((ms, tn), lambda j, i: (0, j)))
    n_dots = sum(len(ws) for _, _, ws in groups)
    vmem += (2 * (len(tilex) + len(out_dtypes)) + n_dots + 2) * tm * tn * 4
    body = functools.partial(
        _mm_body, group_sizes=tuple(len(ws) for _, _, ws in groups), n_rx=len(rowx),
        n_tx=len(tilex), n_out=len(out_dtypes), epilogue=epilogue)
    res = pl.pallas_call(
        body, grid=(nj, ni), in_specs=in_specs, out_specs=out_specs, out_shape=out_shape,
        compiler_params=_params(("arbitrary", "arbitrary"), vmem + (6 << 20)),
    )(*args)
    n = len(out_dtypes)
    return res[:n], res[n:]


def _ep_glu(dots, rx, tx, j):
    return [dots[0] * _sigmoid(dots[1])]


def _ep_swiglu(dots, rx, tx, j):
    return [jax.nn.silu(dots[0]) * dots[1]]


def _ep_plain(dots, rx, tx, j):
    return [dots[0]]


def _ep_residual(dots, rx, tx, j):
    return [tx[0] + sum(dots[1:], dots[0])]


def _ep_ple(dots, rx, tx, j):
    return [tx[0] + _sigmoid(dots[0]) * dots[1]]


def _ep_shortconv(dots, rx, tx, j):
    return [dots[0], dots[1] * dots[2]]


def _ep_rope(dots, rx, tx, j):
    z = dots[0]
    cos, sin_lo, sin_hi = rx
    half = ROPE_DIM // 2
    heads = []
    for h in range(z.shape[1] // HEAD_DIM):
        x = z[:, h * HEAD_DIM:(h + 1) * HEAD_DIM]
        heads.append(x * cos + pltpu.roll(x, half, 1) * sin_hi
                     + pltpu.roll(x, HEAD_DIM - half, 1) * sin_lo)
    roped = jnp.concatenate(heads, axis=1)
    slot = j - ATTN_W // z.shape[1]
    is_v = jnp.logical_and(slot >= 0, slot % 2 == 1)
    return [jnp.where(is_v, z, roped)]


def rope_tables(pos):
    half = ROPE_DIM // 2
    inv = ROPE_THETA ** (-2.0 * jnp.arange(half, dtype=F32) / ROPE_DIM)
    ang = pos.astype(F32)[:, None] * inv[None, :]
    cos, sin = jnp.cos(ang), jnp.sin(ang)
    n = pos.shape[0]
    rest = HEAD_DIM - ROPE_DIM
    c = jnp.concatenate([cos, cos, jnp.ones((n, rest), F32)], axis=1)
    s_lo = jnp.concatenate([-sin, jnp.zeros((n, half + rest), F32)], axis=1)
    s_hi = jnp.concatenate([jnp.zeros((n, half), F32), sin, jnp.zeros((n, rest), F32)], axis=1)
    return c, s_lo, s_hi


def _conf_core(xw_ref, cw_ref, cb_ref, g_ref, b_ref, cbuf_ref, o_ref, tt):
    base = HALO - (CONF_W - 1)
    for c in range(CONF_CH // LANES):
        cs = slice(c * LANES, (c + 1) * LANES)
        acc = jnp.zeros((tt, LANES), F32)
        for r in range(SUBLANES):
            offs = [a for a in range(HALO // SUBLANES + 1) if 0 <= SUBLANES * a + r - base < CONF_W]
            xr = xw_ref[pl.ds(r, SUBLANES * max(offs) + tt), cs]
            for a in offs:
                w = SUBLANES * a + r - base
                acc = acc + xr[SUBLANES * a:SUBLANES * a + tt] * cw_ref[w:w + 1, cs]
        cbuf_ref[:, cs] = acc
    c = cbuf_ref[...] + cb_ref[...]
    mu = jnp.mean(c, axis=-1, keepdims=True)
    var = jnp.mean(jnp.square(c - mu), axis=-1, keepdims=True)
    y = (c - mu) * lax.rsqrt(var + LN_EPS) * g_ref[...] + b_ref[...]
    o_ref[0] = jax.nn.silu(y).astype(o_ref.dtype)


def _conf_body(prev_ref, halo_ref, x_ref, cw_ref, cb_ref, g_ref, b_ref, o_ref, xw_ref, cbuf_ref, *, tt):
    first = pl.program_id(1) == 0
    xw_ref[0:HALO] = jnp.where(first, prev_ref[0], halo_ref[0])
    xw_ref[HALO:HALO + tt] = x_ref[0]
    _conf_core(xw_ref, cw_ref, cb_ref, g_ref, b_ref, cbuf_ref, o_ref, tt)


def conf_conv(u, prev, cw, cb, ln_g, ln_b, tt, out_dtype):
    b, t, c = u.shape
    if tt < HALO:
        assert t == tt
        halo_spec = pl.BlockSpec((1, HALO, c), lambda bi, i: (bi, 0, 0))
        halo_arr = prev
    else:
        hb = tt // HALO
        halo_spec = pl.BlockSpec((1, HALO, c), lambda bi, i: (bi, jnp.maximum(i * hb - 1, 0), 0))
        halo_arr = u
    cwp = jnp.pad(cw, ((0, HALO - cw.shape[0]), (0, 0)))
    row = lambda v: v.reshape(1, c)
    const = lambda bi, i: (0, 0)
    return pl.pallas_call(
        functools.partial(_conf_body, tt=tt),
        grid=(b, t // tt),
        in_specs=[pl.BlockSpec((1, HALO, c), lambda bi, i: (bi, 0, 0)), halo_spec,
                  pl.BlockSpec((1, tt, c), lambda bi, i: (bi, i, 0)),
                  pl.BlockSpec((HALO, c), const), pl.BlockSpec((1, c), const),
                  pl.BlockSpec((1, c), const), pl.BlockSpec((1, c), const)],
        out_specs=pl.BlockSpec((1, tt, c), lambda bi, i: (bi, i, 0)),
        out_shape=jax.ShapeDtypeStruct((b, t, c), out_dtype),
        scratch_shapes=[pltpu.VMEM((tt + HALO, c), F32), pltpu.VMEM((tt, c), F32)],
        compiler_params=_params(("arbitrary", "arbitrary"), 10 * (tt + HALO) * c * 4),
    )(prev, halo_arr, u, cwp, row(cb), row(ln_g), row(ln_b))


def _short_body(prev_ref, halo_ref, cv_ref, bg_ref, w_ref, o_ref, xw_ref, *, tt):
    first = pl.program_id(1) == 0
    xw_ref[0:SUBLANES] = jnp.where(first, prev_ref[0], halo_ref[0])
    xw_ref[SUBLANES:SUBLANES + tt] = cv_ref[0]
    base = SUBLANES - (SC_W - 1)
    conv = xw_ref[pl.ds(base, tt), :] * w_ref[0:1, :]
    for k in range(1, SC_W):
        conv = conv + xw_ref[pl.ds(base + k, tt), :] * w_ref[k:k + 1, :]
    o_ref[0] = (bg_ref[0] * conv).astype(o_ref.dtype)


def short_conv(cv, bg, prev, w, tt, out_dtype):
    b, t, c = cv.shape
    hb = tt // SUBLANES
    wp = jnp.pad(w, ((0, SUBLANES - w.shape[0]), (0, 0)))
    return pl.pallas_call(
        functools.partial(_short_body, tt=tt),
        grid=(b, t // tt),
        in_specs=[pl.BlockSpec((1, SUBLANES, c), lambda bi, i: (bi, 0, 0)),
                  pl.BlockSpec((1, SUBLANES, c), lambda bi, i: (bi, jnp.maximum(i * hb - 1, 0), 0)),
                  pl.BlockSpec((1, tt, c), lambda bi, i: (bi, i, 0)),
                  pl.BlockSpec((1, tt, c), lambda bi, i: (bi, i, 0)),
                  pl.BlockSpec((SUBLANES, c), lambda bi, i: (0, 0))],
        out_specs=pl.BlockSpec((1, tt, c), lambda bi, i: (bi, i, 0)),
        out_shape=jax.ShapeDtypeStruct((b, t, c), out_dtype),
        scratch_shapes=[pltpu.VMEM((tt + SUBLANES, c), F32)],
        compiler_params=_params(("arbitrary", "arbitrary"), 10 * (tt + SUBLANES) * c * 4),
    )(prev, cv, cv, bg, wp)


CMP_PAGES = 8


def _cmp_proj_body(pt_ref, *refs):
    pages = refs[:CMP_PAGES]
    pe_ref, w1_ref, p_ref, kbuf_ref, x_ref = refs[CMP_PAGES:]
    rows = CMP_PAGES * SUB_PER_PAGE
    for kv in range(2):
        for g in range(N_KV):
            cs = slice((kv * N_KV + g) * HEAD_DIM, (kv * N_KV + g + 1) * HEAD_DIM)
            for c in range(CMP_PAGES):
                kbuf_ref[g, c * PAGE_SIZE:(c + 1) * PAGE_SIZE, :] = pages[c][0, :, cs]
            for s in range(CMP_STRIDE):
                x_ref[g * rows:(g + 1) * rows, s * HEAD_DIM:(s + 1) * HEAD_DIM] = (
                    kbuf_ref[g, pl.ds(s, rows, stride=CMP_STRIDE), :])
        x = x_ref[...]
        for half in range(2):
            xh = (x + pe_ref[kv, half]).astype(BF16)
            p = _dot(xh, w1_ref[kv, half].astype(BF16))
            for g in range(N_KV):
                p_ref[0, kv, half, g] = p[g * rows:(g + 1) * rows]


def cmp_project(rows3d, col_blk, page_table, pe, w1):
    b, n_pages = page_table.shape
    n_sub = n_pages * SUB_PER_PAGE
    steps = n_pages // CMP_PAGES
    pe_flat = pe.reshape(2, 2, 1, SUB_FLAT)
    w1_flat = w1.reshape(2, 2, SUB_FLAT, HEAD_DIM)

    def page_spec(c):
        return pl.BlockSpec((1, PAGE_SIZE, PAGE_ROW_W),
                            lambda bi, i, pt: (pt[bi, i * CMP_PAGES + c], 0, col_blk))

    rows = CMP_PAGES * SUB_PER_PAGE
    grid_spec = pltpu.PrefetchScalarGridSpec(
        num_scalar_prefetch=1, grid=(b, steps),
        in_specs=[page_spec(c) for c in range(CMP_PAGES)] + [
            pl.BlockSpec((2, 2, 1, SUB_FLAT), lambda bi, i, pt: (0, 0, 0, 0)),
            pl.BlockSpec((2, 2, SUB_FLAT, HEAD_DIM), lambda bi, i, pt: (0, 0, 0, 0))],
        out_specs=pl.BlockSpec((1, 2, 2, N_KV, rows, HEAD_DIM), lambda bi, i, pt: (bi, 0, 0, 0, i, 0)),
        scratch_shapes=[pltpu.VMEM((N_KV, CMP_PAGES * PAGE_SIZE, HEAD_DIM), F32),
                        pltpu.VMEM((N_KV * rows, SUB_FLAT), F32)])
    return pl.pallas_call(
        _cmp_proj_body, grid_spec=grid_spec,
        out_shape=jax.ShapeDtypeStruct((b, 2, 2, N_KV, n_sub, HEAD_DIM), F32),
        compiler_params=_params(("arbitrary", "arbitrary"), 32 << 20),
    )(page_table, *([rows3d] * CMP_PAGES), pe_flat, w1_flat)


def _cmp_finish_body(p_ref, b1_ref, w2_ref, o_ref):
    n_sub = p_ref.shape[-2]
    p1_next = pltpu.roll(p_ref[0, 0, 1, 0], n_sub - 1, 0)
    pre = p_ref[0, 0, 0, 0] + p1_next + b1_ref[0]
    o_ref[0, 0, 0] = _dot(jax.nn.gelu(pre).astype(BF16), w2_ref[0].astype(BF16)).astype(o_ref.dtype)


def cmp_finish(p, b1, w2):
    b, _, _, _, n_sub, _ = p.shape
    return pl.pallas_call(
        _cmp_finish_body, grid=(b, 2, N_KV),
        in_specs=[pl.BlockSpec((1, 1, 2, 1, n_sub, HEAD_DIM), lambda bi, kv, g: (bi, kv, 0, g, 0, 0)),
                  pl.BlockSpec((1, 1, HEAD_DIM), lambda bi, kv, g: (kv, 0, 0)),
                  pl.BlockSpec((1, HEAD_DIM, HEAD_DIM), lambda bi, kv, g: (kv, 0, 0))],
        out_specs=pl.BlockSpec((1, 1, 1, n_sub, HEAD_DIM), lambda bi, kv, g: (bi, kv, g, 0, 0)),
        out_shape=jax.ShapeDtypeStruct((b, 2, N_KV, n_sub, HEAD_DIM), BF16),
        compiler_params=_params(("arbitrary",) * 3, 16 << 20),
    )(p, b1.reshape(2, 1, HEAD_DIM), w2)


def _masked_softmax(s, mask):
    s = jnp.where(mask, s, NEG)
    e = jnp.exp(s - jnp.max(s, axis=-1, keepdims=True))
    return jnp.where(mask, e / jnp.sum(e, axis=-1, keepdims=True), 0.0)


def _split3(x):
    x1 = x.astype(BF16)
    r1 = x - x1.astype(F32)
    x2 = r1.astype(BF16)
    x3 = (r1 - x2.astype(F32)).astype(BF16)
    return x1, x2, x3


def _overlap_matrix(n_cmp, n_sel):
    n = lax.broadcasted_iota(jnp.int32, (n_cmp, n_sel), 0) * CMP_STRIDE
    j = lax.broadcasted_iota(jnp.int32, (n_cmp, n_sel), 1) * L_SEL
    return jnp.where(jnp.logical_and(n < j + L_SEL, n + L_CMP > j), 1.0, 0.0).astype(BF16)


def _importance(p_sum, n_sel):
    ov = _overlap_matrix(p_sum.shape[1], n_sel)
    a, b, c = _split3(p_sum)
    return _dot(a, ov) + _dot(b, ov) + _dot(c, ov)


def _select_blocks(imp, t_pos):
    r, n = imp.shape
    j = lax.broadcasted_iota(jnp.int32, (r, n), 1)
    cur = t_pos // L_SEL
    forced = (j == 0) | (j == cur) | (j == cur - 1)
    valid = j * L_SEL <= t_pos
    score = jnp.where(forced, FORCE, jnp.where(valid, imp, NEG))
    ok = score > 0.5 * NEG
    picked = jnp.zeros((r, n), jnp.bool_)
    left = score
    jf = j.astype(F32)
    for _ in range(N_SEL):
        top = jnp.max(left, axis=-1, keepdims=True)
        first = jnp.min(jnp.where(left == top, jf, float(n)), axis=-1, keepdims=True)
        hit = jf == first
        picked = picked | hit
        left = jnp.where(hit, -jnp.inf, left)
    return picked & ok


def _flash_update(s, v, m_ref, l_ref, acc_ref):
    m_prev = m_ref[...]
    m_new = jnp.maximum(m_prev, jnp.max(s, axis=-1, keepdims=True))
    alpha = jnp.exp(m_prev - m_new)
    p = jnp.exp(s - m_new)
    l_ref[...] = alpha * l_ref[...] + jnp.sum(p, axis=-1, keepdims=True)
    acc_ref[...] = alpha * acc_ref[...] + _dot(p.astype(BF16), v)
    m_ref[...] = m_new


PQ = 128
PK = 512
P_SEL_PAD = 128
WIN_SPAN = WINDOW + PQ


def _nsa_prompt_body(q_ref, ks_ref, vs_ref, kw_ref, vw_ref, kc_ref, vc_ref, gl_ref, o_ref,
                     kaug_ref, vsb_ref, kwb_ref, vwb_ref, m_ref, l_ref, acc_ref, *, seq):
    i = pl.program_id(2)
    rows = GROUP * PQ
    scale = HEAD_DIM ** -0.5

    @pl.when(i == 0)
    def _():
        kaug_ref[:, 0:HEAD_DIM] = ks_ref[0].astype(BF16)
        key = lax.broadcasted_iota(jnp.int32, (seq, P_SEL_PAD), 0)
        blk = lax.broadcasted_iota(jnp.int32, (seq, P_SEL_PAD), 1)
        kaug_ref[:, HEAD_DIM:] = jnp.where(key // L_SEL == blk, 1.0, 0.0).astype(BF16)
        vsb_ref[...] = vs_ref[0].astype(BF16)
        kwb_ref[...] = kw_ref[0].astype(BF16)
        vwb_ref[...] = vw_ref[0].astype(BF16)

    q = q_ref[0]
    qs = jnp.concatenate([q[:, r * HEAD_DIM:(r + 1) * HEAD_DIM] for r in range(GROUP)], axis=0)
    qb = (qs * scale).astype(BF16)
    t_q = i * PQ + lax.broadcasted_iota(jnp.int32, (PQ, 1), 0)
    t_rows = jnp.concatenate([t_q] * GROUP, axis=0)

    kc = kc_ref[0, 0, 0]
    n_cmp = kc.shape[0]
    s = _dot_nt(qb, kc)
    cmp_end = lax.broadcasted_iota(jnp.int32, (rows, n_cmp), 1) * CMP_STRIDE + (L_CMP - 1)
    p_cmp = _masked_softmax(s, cmp_end <= t_rows)
    o_cmp = _dot(p_cmp.astype(BF16), vc_ref[0, 0, 0])

    p_sum = p_cmp[0:PQ]
    for r in range(1, GROUP):
        p_sum = p_sum + p_cmp[r * PQ:(r + 1) * PQ]
    sel = _select_blocks(_importance(p_sum, P_SEL_PAD), t_q)
    pen = jnp.where(sel, 0.0, NEG).astype(BF16)
    qa = jnp.concatenate([qb, jnp.concatenate([pen] * GROUP, axis=0)], axis=1)

    m_ref[...] = jnp.full(m_ref.shape, NEG, F32)
    l_ref[...] = jnp.zeros(l_ref.shape, F32)
    acc_ref[...] = jnp.zeros(acc_ref.shape, F32)
    n_tiles = (i * PQ + PQ - 1) // PK + 1

    def slc_tile(kt, carry):
        k0 = pl.multiple_of(kt * PK, PK)
        s = _dot_nt(qa, kaug_ref[pl.ds(k0, PK), :])
        kpos = k0 + lax.broadcasted_iota(jnp.int32, (rows, PK), 1)
        s = jnp.where(kpos <= t_rows, s, NEG)
        _flash_update(s, vsb_ref[pl.ds(k0, PK), :], m_ref, l_ref, acc_ref)
        return carry

    lax.fori_loop(0, n_tiles, slc_tile, 0)
    o_slc = acc_ref[...] / l_ref[...]

    w0 = pl.multiple_of(jnp.clip(i * PQ - WINDOW, 0, seq - WIN_SPAN), PQ)
    s = _dot_nt(qb, kwb_ref[pl.ds(w0, WIN_SPAN), :])
    wpos = w0 + lax.broadcasted_iota(jnp.int32, (rows, WIN_SPAN), 1)
    p_win = _masked_softmax(s, (wpos <= t_rows) & (wpos > t_rows - WINDOW))
    o_win = _dot(p_win.astype(BF16), vwb_ref[pl.ds(w0, WIN_SPAN), :])

    gate = _sigmoid(gl_ref[0])
    for r in range(GROUP):
        rs = slice(r * PQ, (r + 1) * PQ)
        c0 = r * N_BRANCH
        o = (gate[:, c0:c0 + 1] * o_cmp[rs] + gate[:, c0 + 1:c0 + 2] * o_slc[rs]
             + gate[:, c0 + 2:c0 + 3] * o_win[rs])
        o_ref[0, :, r * HEAD_DIM:(r + 1) * HEAD_DIM] = o.astype(o_ref.dtype)


def nsa_prompt(qkv, cblk, gl):
    b, t, _ = qkv.shape
    qcol = ATTN_W // HEAD_DIM

    def kv_spec(slot):
        return pl.BlockSpec((1, t, HEAD_DIM), lambda bi, g, i: (bi, 0, qcol + slot * N_KV + g))

    n_cmp = cblk.shape[3]
    rows = GROUP * PQ
    return pl.pallas_call(
        functools.partial(_nsa_prompt_body, seq=t),
        grid=(b, N_KV, t // PQ),
        in_specs=[pl.BlockSpec((1, PQ, GROUP * HEAD_DIM), lambda bi, g, i: (bi, i, g)),
                  kv_spec(2), kv_spec(3), kv_spec(4), kv_spec(5),
                  pl.BlockSpec((1, 1, 1, n_cmp, HEAD_DIM), lambda bi, g, i: (bi, 0, g, 0, 0)),
                  pl.BlockSpec((1, 1, 1, n_cmp, HEAD_DIM), lambda bi, g, i: (bi, 1, g, 0, 0)),
                  pl.BlockSpec((1, PQ, LANES), lambda bi, g, i: (bi, i, g))],
        out_specs=pl.BlockSpec((1, PQ, GROUP * HEAD_DIM), lambda bi, g, i: (bi, i, g)),
        out_shape=jax.ShapeDtypeStruct((b, t, ATTN_W), BF16),
        scratch_shapes=[pltpu.VMEM((t, 2 * HEAD_DIM), BF16), pltpu.VMEM((t, HEAD_DIM), BF16),
                        pltpu.VMEM((t, HEAD_DIM), BF16), pltpu.VMEM((t, HEAD_DIM), BF16),
                        pltpu.VMEM((rows, 1), F32), pltpu.VMEM((rows, 1), F32),
                        pltpu.VMEM((rows, HEAD_DIM), F32)],
        compiler_params=_params(("arbitrary",) * 3, 40 << 20),
    )(qkv, qkv, qkv, qkv, qkv, cblk, cblk, gl)


SP = 8
S_KEYS = SP * PAGE_SIZE
S_BLKS = S_KEYS // L_SEL
S_CHUNK = LANES // S_BLKS
S_ROWS = GROUP * S_PAD


def _nsa_sample_body(pt_ref, *refs, t_new, n_steps, n_sel_pad):
    pages = refs[:SP]
    (q_ref, cb_ref, win_ref, gl_ref, o_ref,
     qb_ref, pen_ref, new_ref, ocw_ref, m_ref, l_ref, acc_ref) = refs[SP:]
    i = pl.program_id(1)
    scale = HEAD_DIM ** -0.5
    n_chunks = n_sel_pad // LANES
    n_win = win_ref.shape[1]
    tok = lax.broadcasted_iota(jnp.int32, (S_PAD, 1), 0)
    t_q = PAST_LEN + tok
    t_rows = jnp.concatenate([t_q] * GROUP, axis=0)
    qcol = ATTN_W

    @pl.when(i == 0)
    def _():
        pad = jnp.zeros((LANES - S_PAD, HEAD_DIM), F32)
        for g in range(N_KV):
            heads = [q_ref[0, :, (g * GROUP + r) * HEAD_DIM:(g * GROUP + r + 1) * HEAD_DIM]
                     for r in range(GROUP)]
            qb_ref[g] = (jnp.concatenate(heads, axis=0) * scale).astype(BF16)
            for slot in range(2, 6):
                c0 = qcol + (slot * N_KV + g) * HEAD_DIM
                new_ref[slot - 2, g] = jnp.concatenate(
                    [q_ref[0, :, c0:c0 + HEAD_DIM], pad], axis=0).astype(BF16)
        m_ref[...] = jnp.full(m_ref.shape, NEG, F32)
        l_ref[...] = jnp.zeros(l_ref.shape, F32)
        acc_ref[...] = jnp.zeros(acc_ref.shape, F32)
        p_sums = []
        for g in range(N_KV):
            qb = qb_ref[g]
            kc = cb_ref[0, 0, g]
            n_cmp = kc.shape[0]
            s = _dot_nt(qb, kc)
            cmp_end = lax.broadcasted_iota(jnp.int32, (S_ROWS, n_cmp), 1) * CMP_STRIDE + (L_CMP - 1)
            p_cmp = _masked_softmax(s, cmp_end <= t_rows)
            ocw_ref[0, g] = _dot(p_cmp.astype(BF16), cb_ref[0, 1, g])
            p_sum = p_cmp[0:S_PAD]
            for r in range(1, GROUP):
                p_sum = p_sum + p_cmp[r * S_PAD:(r + 1) * S_PAD]
            p_sums.append(p_sum)
            kw = win_ref[0, :, g * HEAD_DIM:(g + 1) * HEAD_DIM].astype(BF16)
            vw = win_ref[0, :, (N_KV + g) * HEAD_DIM:(N_KV + g + 1) * HEAD_DIM].astype(BF16)
            s = jnp.concatenate([_dot_nt(qb, kw), _dot_nt(qb, new_ref[2, g])], axis=1)
            lane = lax.broadcasted_iota(jnp.int32, (S_ROWS, n_win + LANES), 1)
            wpos = PAST_LEN - n_win + lane
            mask = (wpos <= t_rows) & (wpos > t_rows - WINDOW) & (lane < n_win + t_new)
            p_win = _masked_softmax(s, mask).astype(BF16)
            ocw_ref[1, g] = _dot(p_win[:, 0:n_win], vw) + _dot(p_win[:, n_win:], new_ref[3, g])
        imp = _importance(jnp.concatenate(p_sums, axis=0), n_sel_pad)
        sel = _select_blocks(imp, jnp.concatenate([t_q] * N_KV, axis=0))
        pen = jnp.where(sel, 0.0, NEG).astype(BF16)
        for g in range(N_KV):
            for ch in range(n_chunks):
                blk = pen[g * S_PAD:(g + 1) * S_PAD, ch * LANES:(ch + 1) * LANES]
                pen_ref[ch, g] = jnp.concatenate([blk] * GROUP, axis=0)

    key = lax.broadcasted_iota(jnp.int32, (S_KEYS, LANES), 0)
    blk = lax.broadcasted_iota(jnp.int32, (S_KEYS, LANES), 1)
    ind = jnp.where((i % S_CHUNK) * S_BLKS + key // L_SEL == blk, 1.0, 0.0).astype(BF16)
    for g in range(N_KV):
        k = jnp.concatenate([p[0, :, g * HEAD_DIM:(g + 1) * HEAD_DIM] for p in pages], axis=0)
        v = jnp.concatenate(
            [p[0, :, (N_KV + g) * HEAD_DIM:(N_KV + g + 1) * HEAD_DIM] for p in pages], axis=0)
        kaug = jnp.concatenate([k.astype(BF16), ind], axis=1)
        qa = jnp.concatenate([qb_ref[g], pen_ref[i // S_CHUNK, g]], axis=1)
        _flash_update(_dot_nt(qa, kaug), v.astype(BF16), m_ref.at[g], l_ref.at[g], acc_ref.at[g])

    @pl.when(i == n_steps - 1)
    def _():
        gate = _sigmoid(gl_ref[0])
        lane = lax.broadcasted_iota(jnp.int32, (S_ROWS, LANES), 1)
        cur_chunk, cur_lane = (PAST_LEN // L_SEL) // LANES, (PAST_LEN // L_SEL) % LANES
        for g in range(N_KV):
            pen_cur = pen_ref[cur_chunk, g][:, cur_lane:cur_lane + 1].astype(F32)
            s = _dot_nt(qb_ref[g], new_ref[0, g]) + pen_cur
            s = jnp.where((PAST_LEN + lane <= t_rows) & (lane < t_new), s, NEG)
            _flash_update(s, new_ref[1, g], m_ref.at[g], l_ref.at[g], acc_ref.at[g])
            o_slc = acc_ref[g] / l_ref[g]
            for r in range(GROUP):
                rs = slice(r * S_PAD, (r + 1) * S_PAD)
                c0 = (g * GROUP + r) * N_BRANCH
                o = (gate[:, c0:c0 + 1] * ocw_ref[0, g][rs] + gate[:, c0 + 1:c0 + 2] * o_slc[rs]
                     + gate[:, c0 + 2:c0 + 3] * ocw_ref[1, g][rs])
                h = g * GROUP + r
                o_ref[0, :, h * HEAD_DIM:(h + 1) * HEAD_DIM] = o.astype(o_ref.dtype)


def nsa_sample(qkv, t_new, cblk, cache_slc, cache_win, page_table, gl):
    b = qkv.shape[0]
    n_pages = page_table.shape[1]
    n_steps = n_pages // SP
    n_cmp = cblk.shape[3]
    n_win = cache_win.shape[1]
    n_sel = -(-(PAST_LEN + t_new) // L_SEL)
    n_sel_pad = -(-n_sel // LANES) * LANES
    assert n_steps * S_BLKS <= n_sel_pad and n_steps * SP == n_pages

    def page_spec(c):
        return pl.BlockSpec((1, PAGE_SIZE, PAGE_ROW_W), lambda bi, i, pt: (pt[bi, i * SP + c], 0, 0))

    grid_spec = pltpu.PrefetchScalarGridSpec(
        num_scalar_prefetch=1, grid=(b, n_steps),
        in_specs=[page_spec(c) for c in range(SP)] + [
            pl.BlockSpec((1, S_PAD, QKV_W), lambda bi, i, pt: (bi, 0, 0)),
            pl.BlockSpec((1, 2, N_KV, n_cmp, HEAD_DIM), lambda bi, i, pt: (bi, 0, 0, 0, 0)),
            pl.BlockSpec((1, n_win, PAGE_ROW_W), lambda bi, i, pt: (bi, 0, 0)),
            pl.BlockSpec((1, S_PAD, LANES), lambda bi, i, pt: (bi, 0, 0))],
        out_specs=pl.BlockSpec((1, S_PAD, ATTN_W), lambda bi, i, pt: (bi, 0, 0)),
        scratch_shapes=[pltpu.VMEM((N_KV, S_ROWS, HEAD_DIM), BF16),
                        pltpu.VMEM((n_sel_pad // LANES, N_KV, S_ROWS, LANES), BF16),
                        pltpu.VMEM((4, N_KV, LANES, HEAD_DIM), BF16),
                        pltpu.VMEM((2, N_KV, S_ROWS, HEAD_DIM), F32),
                        pltpu.VMEM((N_KV, S_ROWS, 1), F32), pltpu.VMEM((N_KV, S_ROWS, 1), F32),
                        pltpu.VMEM((N_KV, S_ROWS, HEAD_DIM), F32)])
    return pl.pallas_call(
        functools.partial(_nsa_sample_body, t_new=t_new, n_steps=n_steps, n_sel_pad=n_sel_pad),
        grid_spec=grid_spec,
        out_shape=jax.ShapeDtypeStruct((b, S_PAD, ATTN_W), F32),
        compiler_params=_params(("arbitrary", "arbitrary"), 40 << 20),
    )(page_table, *([cache_slc] * SP), qkv, cblk, cache_win, gl)


def _pad_rows(x, front, total):
    return jnp.pad(x, ((0, 0), (front, total - front - x.shape[1]), (0, 0)))


def kernel(x_prompt, x_sample, cache_cmp_kv, cache_slc_kv, cache_win_kv, state_conf_conv,
           state_short_conv, page_table, p_prompt, p_sample, g_mix0, w_in0, conv_w0, conv_b0,
           ln_g0, ln_b0, cmp_pe, cmp_w1, cmp_b1, cmp_w2, w_out0, g_mix1, w_in1, sconv_w1, w_out1,
           g_ffn, w_ffn_gate, w_ffn_up, w_ffn_down, g_ple, w_ple_gate, w_ple_proj, g_final):
    bp, seq, d = x_prompt.shape
    bs, t_new, _ = x_sample.shape
    mp, ms = bp * seq, bs * t_new
    d_ff = w_ffn_gate.shape[-1]
    n_pool = cache_cmp_kv.shape[1]
    n_win = cache_win_kv.shape[2]
    tm = 1024

    xp = x_prompt.reshape(mp, d)
    xs = x_sample.reshape(ms, d)
    rope_p = rope_tables(jnp.arange(seq, dtype=jnp.int32))
    rope_s = tuple(jnp.tile(tb, (bs, 1)) for tb in rope_tables(PAST_LEN + jnp.arange(t_new, dtype=jnp.int32)))
    outs = {}

    def ffn_ple(xp, xs, i):
        hp, hs = rmsnorm(xp, g_ffn[i], BF16), rmsnorm(xs, g_ffn[i], BF16)
        (gp,), (gs,) = fused_mm([(hp, hs, [W(w_ffn_gate[i]), W(w_ffn_up[i])])], _ep_swiglu, [BF16],
                                d_ff, tm, 256)
        (xp,), (xs,) = fused_mm([(gp, gs, [W(w_ffn_down[i].astype(BF16))])], _ep_residual, [F32],
                                d, 512, 512, tilex=[(xp, xs)])
        hp, hs = rmsnorm(xp, g_ple[i], BF16), rmsnorm(xs, g_ple[i], BF16)
        pp = p_prompt[i].reshape(mp, -1).astype(BF16)
        ps = p_sample[i].reshape(ms, -1).astype(BF16)
        (xp,), (xs,) = fused_mm([(hp, hs, [W(w_ple_gate[i])]), (pp, ps, [W(w_ple_proj[i])])],
                                _ep_ple, [F32], d, tm, 512, tilex=[(xp, xs)])
        return xp, xs

    e = 0
    hp, hs = rmsnorm(xp, g_mix0[e], BF16), rmsnorm(xs, g_mix0[e], BF16)
    w_in = w_in0[e]
    (up,), (us,) = fused_mm([(hp, hs, [W(w_in), W(w_in, col_off=CONF_CH // 256)])], _ep_glu, [F32],
                            CONF_CH, tm, 256)
    (qkv_p,), (qkv_s,) = fused_mm([(hp, hs, [W(w_in, col_off=2 * CONF_CH // KV_W)])], _ep_rope, [F32],
                                  QKV_W, tm, KV_W, rowx=list(zip(rope_p, rope_s)))
    w_gl = w_in[:, 2 * CONF_CH + QKV_W:].reshape(d, N_KV, GROUP * N_BRANCH)
    w_gl = jnp.pad(w_gl, ((0, 0), (0, 0), (0, LANES - GROUP * N_BRANCH))).reshape(d, N_KV * LANES)
    (gl_p,), (gl_s,) = fused_mm([(hp, hs, [W(w_gl)])], _ep_plain, [F32], N_KV * LANES, tm, N_KV * LANES)

    up3, us3 = up.reshape(bp, seq, CONF_CH), us.reshape(bs, t_new, CONF_CH)
    conf_w = (conv_w0[e], conv_b0[e], ln_g0[e], ln_b0[e])
    a_p = conf_conv(up3, jnp.zeros((bp, HALO, CONF_CH), F32), *conf_w, tt=128, out_dtype=BF16)
    st = state_conf_conv[e]
    a_s = conf_conv(_pad_rows(us3, 0, S_PAD), _pad_rows(st, HALO - st.shape[1], HALO), *conf_w,
                    tt=S_PAD, out_dtype=F32)
    outs["conf_p"] = up3[:, seq - (CONF_W - 1):]
    outs["conf_s"] = jnp.concatenate([st, us3], axis=1)[:, t_new:]

    qkv_p3, qkv_s3 = qkv_p.reshape(bp, seq, QKV_W), qkv_s.reshape(bs, t_new, QKV_W)
    kv_shape = lambda x: x.reshape(x.shape[0], x.shape[1], 2, N_KV, HEAD_DIM)
    for name, slot in (("cmp", 0), ("slc", 2), ("win", 4)):
        c0 = ATTN_W + slot * KV_W
        outs[name + "_p"] = kv_shape(qkv_p3[:, :, c0:c0 + 2 * KV_W])
        outs[name + "_s"] = kv_shape(qkv_s3[:, :, c0:c0 + 2 * KV_W])
    outs["win_p"] = outs["win_p"][:, seq - min(WINDOW, seq):]
    outs["win_s"] = jnp.concatenate([cache_win_kv[e], outs["win_s"]], axis=1)[:, t_new:]

    pages_p = seq // PAGE_SIZE
    ident = jnp.arange(bp * pages_p, dtype=jnp.int32).reshape(bp, pages_p)
    proj_p = cmp_project(qkv_p.reshape(bp * pages_p, PAGE_SIZE, QKV_W), ATTN_W // PAGE_ROW_W, ident,
                         cmp_pe[e], cmp_w1[e])
    proj_s = cmp_project(cache_cmp_kv[e].reshape(n_pool, PAGE_SIZE, PAGE_ROW_W), 0, page_table,
                         cmp_pe[e], cmp_w1[e])
    cblk_p = cmp_finish(proj_p, cmp_b1[e], cmp_w2[e])
    cblk_s = cmp_finish(proj_s, cmp_b1[e], cmp_w2[e])

    attn_p = nsa_prompt(qkv_p3, cblk_p, gl_p.reshape(bp, seq, N_KV * LANES))
    gl_s3 = gl_s.reshape(bs, t_new, N_KV, LANES)[..., :GROUP * N_BRANCH].reshape(bs, t_new, -1)
    gl_s3 = jnp.pad(gl_s3, ((0, 0), (0, S_PAD - t_new), (0, LANES - gl_s3.shape[-1])))
    attn_s = nsa_sample(_pad_rows(qkv_s3, 0, S_PAD), t_new, cblk_s,
                        cache_slc_kv[e].reshape(n_pool, PAGE_SIZE, PAGE_ROW_W),
                        cache_win_kv[e].reshape(bs, n_win, PAGE_ROW_W), page_table, gl_s3)
    a_p2, a_s2 = a_p.reshape(mp, CONF_CH), a_s[:, :t_new].reshape(ms, CONF_CH).astype(BF16)
    at_p2, at_s2 = attn_p.reshape(mp, ATTN_W), attn_s[:, :t_new].reshape(ms, ATTN_W).astype(BF16)
    (xp,), (xs,) = fused_mm(
        [(a_p2, a_s2, [W(w_out0[e], k_rows=CONF_CH, row_blk=0)]),
         (at_p2, at_s2, [W(w_out0[e], k_rows=ATTN_W, row_blk=CONF_CH // ATTN_W)])],
        _ep_residual, [F32], d, tm, 512, tilex=[(xp, xs)])
    xp, xs = ffn_ple(xp, xs, 0)

    o = 0
    hp, hs = rmsnorm(xp, g_mix1[o], BF16), rmsnorm(xs, g_mix1[o], BF16)
    w_in = w_in1[o]
    (bg_p, cv_p), (bg_s, cv_s) = fused_mm(
        [(hp, hs, [W(w_in), W(w_in, col_off=SC_CH // 256), W(w_in, col_off=2 * SC_CH // 256)])],
        _ep_shortconv, [F32, F32], SC_CH, 512, 256)
    cv_p3, cv_s3 = cv_p.reshape(bp, seq, SC_CH), cv_s.reshape(bs, t_new, SC_CH)
    y_p = short_conv(cv_p3, bg_p.reshape(bp, seq, SC_CH), jnp.zeros((bp, SUBLANES, SC_CH), F32),
                     sconv_w1[o], tt=256, out_dtype=BF16)
    st = state_short_conv[o]
    y_s = short_conv(_pad_rows(cv_s3, 0, S_PAD), _pad_rows(bg_s.reshape(bs, t_new, SC_CH), 0, S_PAD),
                     _pad_rows(st, SUBLANES - st.shape[1], SUBLANES), sconv_w1[o], tt=S_PAD,
                     out_dtype=F32)
    outs["sc_p"] = cv_p3[:, seq - (SC_W - 1):]
    outs["sc_s"] = jnp.concatenate([st, cv_s3], axis=1)[:, t_new:]
    (xp,), (xs,) = fused_mm(
        [(y_p.reshape(mp, SC_CH), y_s[:, :t_new].reshape(ms, SC_CH).astype(BF16), [W(w_out1[o])])],
        _ep_residual, [F32], d, tm, 512, tilex=[(xp, xs)])
    xp, xs = ffn_ple(xp, xs, 1)

    y_p = rmsnorm(xp, g_final, F32).reshape(bp, seq, d)
    y_s = rmsnorm(xs, g_final, F32).reshape(bs, t_new, d)
    st1 = lambda x: x[None]
    return (y_p, y_s, st1(outs["cmp_p"]), st1(outs["cmp_s"]), st1(outs["slc_p"]), st1(outs["slc_s"]),
            st1(outs["win_p"]), st1(outs["win_s"]), st1(outs["conf_p"]), st1(outs["conf_s"]),
            st1(outs["sc_p"]), st1(outs["sc_s"]))
```

```python
import functools

import jax
import jax.numpy as jnp
from jax import lax
from jax.experimental import pallas as pl
from jax.experimental.pallas import tpu as pltpu

F32 = jnp.float32
BF16 = jnp.bfloat16

V7X_VMEM_BYTES = 64 * 1024 * 1024
LANES = 128
SUBLANES = 8

D_MODEL = 4096
PAST_LEN = 16384
PAGE_SIZE = 128
N_HEADS = 16
HEAD_DIM = 128
N_KV = 4
GROUP = N_HEADS // N_KV
ATTN_W = N_HEADS * HEAD_DIM
KV_W = N_KV * HEAD_DIM
ROPE_DIM = HEAD_DIM // 4
ROPE_THETA = 500000.0
L_CMP = 32
CMP_STRIDE = 16
L_SEL = 64
N_SEL = 16
WINDOW = 512
N_BRANCH = 3
CONF_CH = D_MODEL // 2
CONF_W = 31
SC_CH = D_MODEL
SC_W = 3
RMS_EPS = 1e-6
LN_EPS = 1e-5
NEG = -1e30
FORCE = 1e9
QKV_W = ATTN_W + 6 * KV_W
SUB_PER_PAGE = PAGE_SIZE // CMP_STRIDE
SUB_FLAT = CMP_STRIDE * HEAD_DIM
PAGE_ROW_W = 2 * KV_W
HALO = 32
S_PAD = 8


def _vmem_limit(n_bytes):
    return int(min(V7X_VMEM_BYTES - (4 << 20), max(n_bytes, 16 << 20)))


def _params(sem, vmem):
    return pltpu.CompilerParams(dimension_semantics=sem, vmem_limit_bytes=_vmem_limit(vmem))


def _dot(a, b):
    return jnp.dot(a, b, preferred_element_type=F32)


def _dot_nt(a, b):
    return lax.dot_general(a, b, (((1,), (1,)), ((), ())), preferred_element_type=F32)


def _sigmoid(x):
    return jax.nn.sigmoid(x)


def _rmsnorm_body(x_ref, g_ref, o_ref):
    x = x_ref[...]
    y = x * lax.rsqrt(jnp.mean(x * x, axis=-1, keepdims=True) + RMS_EPS)
    o_ref[...] = (y * g_ref[...]).astype(o_ref.dtype)


def rmsnorm(x, g, out_dtype, rows=256):
    m, d = x.shape
    tr = min(rows, m)
    return pl.pallas_call(
        _rmsnorm_body,
        grid=(m // tr,),
        in_specs=[pl.BlockSpec((tr, d), lambda i: (i, 0)),
                  pl.BlockSpec((1, d), lambda i: (0, 0))],
        out_specs=pl.BlockSpec((tr, d), lambda i: (i, 0)),
        out_shape=jax.ShapeDtypeStruct((m, d), out_dtype),
        compiler_params=_params(("arbitrary",), 6 * tr * d * 4),
    )(x, g.reshape(1, d))


class W:
    def __init__(self, arr, layer=0, k_rows=None, row_blk=0, col_off=0):
        self.arr = arr
        self.layer = layer
        self.k_rows = arr.shape[1] if k_rows is None else k_rows
        self.row_blk = row_blk
        self.col_off = col_off


def _mm_body(*refs, group_sizes, n_rx, n_tx, n_out, epilogue):
    it = iter(refs)
    groups = []
    for n_w in group_sizes:
        a_p = next(it)
        a_s = next(it)
        groups.append((a_p, a_s, [next(it) for _ in range(n_w)]))
    rx_p = [next(it) for _ in range(n_rx)]
    rx_s = [next(it) for _ in range(n_rx)]
    tx_p = [next(it) for _ in range(n_tx)]
    tx_s = [next(it) for _ in range(n_tx)]
    o_p = [next(it) for _ in range(n_out)]
    o_s = [next(it) for _ in range(n_out)]
    j = pl.program_id(0)
    wb = [[w[...].astype(BF16) for w in ws] for _, _, ws in groups]

    def run(which, rx, tx, outs):
        dots = []
        for (a_pp, a_ss, _), wbs in zip(groups, wb):
            a = (a_pp if which == 0 else a_ss)[...]
            dots.extend(_dot(a, w) for w in wbs)
        res = epilogue(dots, [r[...] for r in rx], [t[...] for t in tx], j)
        for o, r in zip(outs, res):
            o[...] = r.astype(o.dtype)

    run(0, rx_p, tx_p, o_p)

    @pl.when(pl.program_id(1) == 0)
    def _():
        run(1, rx_s, tx_s, o_s)


def fused_mm(groups, epilogue, out_dtypes, n_cols, tm, tn, rowx=(), tilex=(), rx_period=None):
    mp = groups[0][0].shape[0]
    ms = groups[0][1].shape[0]
    nj, ni = n_cols // tn, mp // tm
    assert nj * tn == n_cols and ni * tm == mp
    args, in_specs = [], []
    vmem = 0
    for a_p, a_s, ws in groups:
        k = a_p.shape[1]
        args += [a_p, a_s]
        in_specs += [pl.BlockSpec((tm, k), lambda j, i: (i, 0)),
                     pl.BlockSpec((ms, k), lambda j, i: (0, 0))]
        vmem += 2 * (tm + ms) * k * a_p.dtype.itemsize
        for w in ws:
            assert w.k_rows == k
            args.append(w.arr)
            in_specs.append(pl.BlockSpec(
                (None, k, tn), functools.partial(lambda j, i, la, rb, co: (la, rb, co + j),
                                                 la=w.layer, rb=w.row_blk, co=w.col_off)))
            vmem += k * tn * (2 * w.arr.dtype.itemsize + 2)
    for which in (0, 1):
        for tab_p, tab_s in rowx:
            if which == 0:
                per = tab_p.shape[0] // tm
                args.append(tab_p)
                in_specs.append(pl.BlockSpec(
                    (tm, tab_p.shape[1]), functools.partial(lambda j, i, per: (i % per, 0), per=per)))
            else:
                args.append(tab_s)
                in_specs.append(pl.BlockSpec(tab_s.shape, lambda j, i: (0, 0)))
    for which in (0, 1):
        for t_p, t_s in tilex:
            if which == 0:
                args.append(t_p)
                in_specs.append(pl.BlockSpec((tm, tn), lambda j, i: (i, j)))
            else:
                args.append(t_s)
                in_specs.append(pl.BlockSpec((ms, tn), lambda j, i: (0, j)))
    out_shape, out_specs = [], []
    for which in (0, 1):
        for dt in out_dtypes:
            if which == 0:
                out_shape.append(jax.ShapeDtypeStruct((mp, n_cols), dt))
                out_specs.append(pl.BlockSpec((tm, tn), lambda j, i: (i, j)))
            else:
                out_shape.append(jax.ShapeDtypeStruct((ms, n_cols), dt))
                out_specs.append(pl.BlockSpec((ms, tn), lambda j, i: (0, j)))
    n_dots = sum(len(ws) for _, _, ws in groups)
    vmem += (2 * (len(tilex) + len(out_dtypes)) + n_dots + 2) * tm * tn * 4
    body = functools.partial(
        _mm_body, group_sizes=tuple(len(ws) for _, _, ws in groups), n_rx=len(rowx),
        n_tx=len(tilex), n_out=len(out_dtypes), epilogue=epilogue)
    res = pl.pallas_call(
        body, grid=(nj, ni), in_specs=in_specs, out_specs=out_specs, out_shape=out_shape,
        compiler_params=_params(("arbitrary", "arbitrary"), vmem + (6 << 20)),
    )(*args)
    n = len(out_dtypes)
    return res[:n], res[n:]


def _ep_glu(dots, rx, tx, j):
    return [dots[0] * _sigmoid(dots[1])]


def _ep_swiglu(dots, rx, tx, j):
    return [jax.nn.silu(dots[0]) * dots[1]]


def _ep_plain(dots, rx, tx, j):
    return [dots[0]]


def _ep_residual(dots, rx, tx, j):
    return [tx[0] + sum(dots[1:], dots[0])]


def _ep_ple(dots, rx, tx, j):
    return [tx[0] + _sigmoid(dots[0]) * dots[1]]


def _ep_shortconv(dots, rx, tx, j):
    return [dots[0], dots[1] * dots[2]]


def _ep_rope(dots, rx, tx, j):
    z = dots[0]
    cos, sin_lo, sin_hi = rx
    half = ROPE_DIM // 2
    heads = []
    for h in range(z.shape[1] // HEAD_DIM):
        x = z[:, h * HEAD_DIM:(h + 1) * HEAD_DIM]
        heads.append(x * cos + pltpu.roll(x, half, 1) * sin_hi
                     + pltpu.roll(x, HEAD_DIM - half, 1) * sin_lo)
    roped = jnp.concatenate(heads, axis=1)
    slot = j - ATTN_W // z.shape[1]
    is_v = jnp.logical_and(slot >= 0, slot % 2 == 1)
    return [jnp.where(is_v, z, roped)]


def rope_tables(pos):
    half = ROPE_DIM // 2
    inv = ROPE_THETA ** (-2.0 * jnp.arange(half, dtype=F32) / ROPE_DIM)
    ang = pos.astype(F32)[:, None] * inv[None, :]
    cos, sin = jnp.cos(ang), jnp.sin(ang)
    n = pos.shape[0]
    rest = HEAD_DIM - ROPE_DIM
    c = jnp.concatenate([cos, cos, jnp.ones((n, rest), F32)], axis=1)
    s_lo = jnp.concatenate([-sin, jnp.zeros((n, half + rest), F32)], axis=1)
    s_hi = jnp.concatenate([jnp.zeros((n, half), F32), sin, jnp.zeros((n, rest), F32)], axis=1)
    return c, s_lo, s_hi


def _conf_core(xw_ref, cw_ref, cb_ref, g_ref, b_ref, cbuf_ref, o_ref, tt):
    base = HALO - (CONF_W - 1)
    for c in range(CONF_CH // LANES):
        cs = slice(c * LANES, (c + 1) * LANES)
        acc = jnp.zeros((tt, LANES), F32)
        for r in range(SUBLANES):
            offs = [a for a in range(HALO // SUBLANES + 1) if 0 <= SUBLANES * a + r - base < CONF_W]
            xr = xw_ref[pl.ds(r, SUBLANES * max(offs) + tt), cs]
            for a in offs:
                w = SUBLANES * a + r - base
                acc = acc + xr[SUBLANES * a:SUBLANES * a + tt] * cw_ref[w:w + 1, cs]
        cbuf_ref[:, cs] = acc
    c = cbuf_ref[...] + cb_ref[...]
    mu = jnp.mean(c, axis=-1, keepdims=True)
    var = jnp.mean(jnp.square(c - mu), axis=-1, keepdims=True)
    y = (c - mu) * lax.rsqrt(var + LN_EPS) * g_ref[...] + b_ref[...]
    o_ref[0] = jax.nn.silu(y).astype(o_ref.dtype)


def _conf_body(prev_ref, halo_ref, x_ref, cw_ref, cb_ref, g_ref, b_ref, o_ref, xw_ref, cbuf_ref, *, tt):
    first = pl.program_id(1) == 0
    xw_ref[0:HALO] = jnp.where(first, prev_ref[0], halo_ref[0])
    xw_ref[HALO:HALO + tt] = x_ref[0]
    _conf_core(xw_ref, cw_ref, cb_ref, g_ref, b_ref, cbuf_ref, o_ref, tt)


def conf_conv(u, prev, cw, cb, ln_g, ln_b, tt, out_dtype):
    b, t, c = u.shape
    if tt < HALO:
        assert t == tt
        halo_spec = pl.BlockSpec((1, HALO, c), lambda bi, i: (bi, 0, 0))
        halo_arr = prev
    else:
        hb = tt // HALO
        halo_spec = pl.BlockSpec((1, HALO, c), lambda bi, i: (bi, jnp.maximum(i * hb - 1, 0), 0))
        halo_arr = u
    cwp = jnp.pad(cw, ((0, HALO - cw.shape[0]), (0, 0)))
    row = lambda v: v.reshape(1, c)
    const = lambda bi, i: (0, 0)
    return pl.pallas_call(
        functools.partial(_conf_body, tt=tt),
        grid=(b, t // tt),
        in_specs=[pl.BlockSpec((1, HALO, c), lambda bi, i: (bi, 0, 0)), halo_spec,
                  pl.BlockSpec((1, tt, c), lambda bi, i: (bi, i, 0)),
                  pl.BlockSpec((HALO, c), const), pl.BlockSpec((1, c), const),
                  pl.BlockSpec((1, c), const), pl.BlockSpec((1, c), const)],
        out_specs=pl.BlockSpec((1, tt, c), lambda bi, i: (bi, i, 0)),
        out_shape=jax.ShapeDtypeStruct((b, t, c), out_dtype),
        scratch_shapes=[pltpu.VMEM((tt + HALO, c), F32), pltpu.VMEM((tt, c), F32)],
        compiler_params=_params(("arbitrary", "arbitrary"), 10 * (tt + HALO) * c * 4),
    )(prev, halo_arr, u, cwp, row(cb), row(ln_g), row(ln_b))


def _short_body(prev_ref, halo_ref, cv_ref, bg_ref, w_ref, o_ref, xw_ref, *, tt):
    first = pl.program_id(1) == 0
    xw_ref[0:SUBLANES] = jnp.where(first, prev_ref[0], halo_ref[0])
    xw_ref[SUBLANES:SUBLANES + tt] = cv_ref[0]
    base = SUBLANES - (SC_W - 1)
    conv = xw_ref[pl.ds(base, tt), :] * w_ref[0:1, :]
    for k in range(1, SC_W):
        conv = conv + xw_ref[pl.ds(base + k, tt), :] * w_ref[k:k + 1, :]
    o_ref[0] = (bg_ref[0] * conv).astype(o_ref.dtype)


def short_conv(cv, bg, prev, w, tt, out_dtype):
    b, t, c = cv.shape
    hb = tt // SUBLANES
    wp = jnp.pad(w, ((0, SUBLANES - w.shape[0]), (0, 0)))
    return pl.pallas_call(
        functools.partial(_short_body, tt=tt),
        grid=(b, t // tt),
        in_specs=[pl.BlockSpec((1, SUBLANES, c), lambda bi, i: (bi, 0, 0)),
                  pl.BlockSpec((1, SUBLANES, c), lambda bi, i: (bi, jnp.maximum(i * hb - 1, 0), 0)),
                  pl.BlockSpec((1, tt, c), lambda bi, i: (bi, i, 0)),
                  pl.BlockSpec((1, tt, c), lambda bi, i: (bi, i, 0)),
                  pl.BlockSpec((SUBLANES, c), lambda bi, i: (0, 0))],
        out_specs=pl.BlockSpec((1, tt, c), lambda bi, i: (bi, i, 0)),
        out_shape=jax.ShapeDtypeStruct((b, t, c), out_dtype),
        scratch_shapes=[pltpu.VMEM((tt + SUBLANES, c), F32)],
        compiler_params=_params(("arbitrary", "arbitrary"), 10 * (tt + SUBLANES) * c * 4),
    )(prev, cv, cv, bg, wp)


CMP_PAGES = 8
SUB_PITCH = 24


def _cmp_proj_body(pt_ref, *refs, row_major):
    pages = refs[:CMP_PAGES]
    w1_ref, p_ref, kbuf_ref, x_ref = refs[CMP_PAGES:]
    rows = CMP_PAGES * SUB_PER_PAGE
    for kv in range(2):
        for g in range(N_KV):
            kvg = kv * N_KV + g
            for c in range(CMP_PAGES):
                if row_major:
                    tok = pages[c][0, pl.ds(kvg, PAGE_SIZE, stride=2 * N_KV), :]
                else:
                    tok = pages[c][0, :, kvg * HEAD_DIM:(kvg + 1) * HEAD_DIM]
                for m in range(SUB_PER_PAGE):
                    r0 = (c * SUB_PER_PAGE + m) * SUB_PITCH
                    kbuf_ref[g, r0:r0 + CMP_STRIDE, :] = tok[m * CMP_STRIDE:(m + 1) * CMP_STRIDE]
            for s in range(CMP_STRIDE):
                x_ref[g * rows:(g + 1) * rows, s * HEAD_DIM:(s + 1) * HEAD_DIM] = (
                    kbuf_ref[g, pl.ds(s, rows, stride=SUB_PITCH), :])
        p = _dot(x_ref[...].astype(BF16), w1_ref[kv])
        for g in range(N_KV):
            p_ref[0, kv, g] = p[g * rows:(g + 1) * rows]


def cmp_w1_cat(w1):
    w = w1.reshape(2, 2, SUB_FLAT, HEAD_DIM).transpose(0, 2, 1, 3)
    return w.reshape(2, SUB_FLAT, 2 * HEAD_DIM).astype(BF16)


def cmp_project(pages_arr, row_major, col_blk, page_table, w1cat):
    b, n_pages = page_table.shape
    n_sub = n_pages * SUB_PER_PAGE
    steps = n_pages // CMP_PAGES
    assert steps * CMP_PAGES == n_pages
    blk = (1, PAGE_SIZE * 2 * N_KV, HEAD_DIM) if row_major else (1, PAGE_SIZE, PAGE_ROW_W)

    def page_spec(c):
        return pl.BlockSpec(blk, lambda bi, i, pt: (pt[bi, i * CMP_PAGES + c], 0, col_blk))

    rows = CMP_PAGES * SUB_PER_PAGE
    grid_spec = pltpu.PrefetchScalarGridSpec(
        num_scalar_prefetch=1, grid=(b, steps),
        in_specs=[page_spec(c) for c in range(CMP_PAGES)] + [
            pl.BlockSpec((2, SUB_FLAT, 2 * HEAD_DIM), lambda bi, i, pt: (0, 0, 0))],
        out_specs=pl.BlockSpec((1, 2, N_KV, rows, 2 * HEAD_DIM), lambda bi, i, pt: (bi, 0, 0, i, 0)),
        scratch_shapes=[pltpu.VMEM((N_KV, rows * SUB_PITCH, HEAD_DIM), F32),
                        pltpu.VMEM((N_KV * rows, SUB_FLAT), F32)])
    return pl.pallas_call(
        functools.partial(_cmp_proj_body, row_major=row_major), grid_spec=grid_spec,
        out_shape=jax.ShapeDtypeStruct((b, 2, N_KV, n_sub, 2 * HEAD_DIM), F32),
        compiler_params=_params(("arbitrary", "arbitrary"), 32 << 20),
    )(page_table, *([pages_arr] * CMP_PAGES), w1cat)


def _cmp_finish_body(p_ref, pe_ref, w1_ref, b1_ref, w2_ref, o_ref):
    n_sub = p_ref.shape[-2]
    w1 = w1_ref[0]
    pe_term = (_dot(pe_ref[0, 0].astype(BF16), w1)[0:1, 0:HEAD_DIM]
               + _dot(pe_ref[0, 1].astype(BF16), w1)[0:1, HEAD_DIM:])
    p = p_ref[0, 0, 0]
    p1_next = pltpu.roll(p[:, HEAD_DIM:], n_sub - 1, 0)
    pre = p[:, 0:HEAD_DIM] + p1_next + (b1_ref[0] + pe_term)
    o_ref[0, 0, 0] = _dot(jax.nn.gelu(pre).astype(BF16), w2_ref[0].astype(BF16)).astype(o_ref.dtype)


def cmp_finish(p, pe, w1cat, b1, w2):
    b, _, _, n_sub, _ = p.shape
    pe_rows = jnp.broadcast_to(pe.reshape(2, 2, 1, SUB_FLAT), (2, 2, SUBLANES, SUB_FLAT))
    return pl.pallas_call(
        _cmp_finish_body, grid=(b, 2, N_KV),
        in_specs=[pl.BlockSpec((1, 1, 1, n_sub, 2 * HEAD_DIM), lambda bi, kv, g: (bi, kv, g, 0, 0)),
                  pl.BlockSpec((1, 2, SUBLANES, SUB_FLAT), lambda bi, kv, g: (kv, 0, 0, 0)),
                  pl.BlockSpec((1, SUB_FLAT, 2 * HEAD_DIM), lambda bi, kv, g: (kv, 0, 0)),
                  pl.BlockSpec((1, 1, HEAD_DIM), lambda bi, kv, g: (kv, 0, 0)),
                  pl.BlockSpec((1, HEAD_DIM, HEAD_DIM), lambda bi, kv, g: (kv, 0, 0))],
        out_specs=pl.BlockSpec((1, 1, 1, n_sub, HEAD_DIM), lambda bi, kv, g: (bi, kv, g, 0, 0)),
        out_shape=jax.ShapeDtypeStruct((b, 2, N_KV, n_sub, HEAD_DIM), BF16),
        compiler_params=_params(("arbitrary",) * 3, 16 << 20),
    )(p, pe_rows, w1cat, b1.reshape(2, 1, HEAD_DIM), w2)


def _masked_softmax(s, mask):
    s = jnp.where(mask, s, NEG)
    e = jnp.exp(s - jnp.max(s, axis=-1, keepdims=True))
    return jnp.where(mask, e / jnp.sum(e, axis=-1, keepdims=True), 0.0)


def _split3(x):
    x1 = x.astype(BF16)
    r1 = x - x1.astype(F32)
    x2 = r1.astype(BF16)
    x3 = (r1 - x2.astype(F32)).astype(BF16)
    return x1, x2, x3


def _overlap_matrix(n_cmp, n_sel):
    n = lax.broadcasted_iota(jnp.int32, (n_cmp, n_sel), 0) * CMP_STRIDE
    j = lax.broadcasted_iota(jnp.int32, (n_cmp, n_sel), 1) * L_SEL
    return jnp.where(jnp.logical_and(n < j + L_SEL, n + L_CMP > j), 1.0, 0.0).astype(BF16)


def _importance(p_sum, n_sel):
    ov = _overlap_matrix(p_sum.shape[1], n_sel)
    a, b, c = _split3(p_sum)
    return _dot(a, ov) + _dot(b, ov) + _dot(c, ov)


def _importance_t(p_sum_t, n_sel):
    n_cmp = p_sum_t.shape[0]
    j = lax.broadcasted_iota(jnp.int32, (n_sel, n_cmp), 0) * L_SEL
    n = lax.broadcasted_iota(jnp.int32, (n_sel, n_cmp), 1) * CMP_STRIDE
    ov_t = jnp.where(jnp.logical_and(n < j + L_SEL, n + L_CMP > j), 1.0, 0.0).astype(BF16)
    a, b, c = _split3(p_sum_t)
    return _dot(ov_t, a) + _dot(ov_t, b) + _dot(ov_t, c)


def _select_blocks_t(imp_t, t_pos):
    n, r = imp_t.shape
    j = lax.broadcasted_iota(jnp.int32, (n, r), 0)
    cur = t_pos // L_SEL
    forced = (j == 0) | (j == cur) | (j == cur - 1)
    valid = j * L_SEL <= t_pos
    score = jnp.where(forced, FORCE, jnp.where(valid, imp_t, NEG))
    rank = jnp.zeros((n, r), F32)
    for k in range(n):
        row = score[k:k + 1, :]
        ahead = (row > score) | ((row == score) & (j > k))
        rank = rank + jnp.where(ahead, 1.0, 0.0)
    return (rank < N_SEL) & (score > 0.5 * NEG)


def _select_blocks(imp, t_pos):
    r, n = imp.shape
    j = lax.broadcasted_iota(jnp.int32, (r, n), 1)
    cur = t_pos // L_SEL
    forced = (j == 0) | (j == cur) | (j == cur - 1)
    valid = j * L_SEL <= t_pos
    score = jnp.where(forced, FORCE, jnp.where(valid, imp, NEG))
    ok = score > 0.5 * NEG
    picked = jnp.zeros((r, n), jnp.bool_)
    left = score
    jf = j.astype(F32)
    for _ in range(N_SEL):
        top = jnp.max(left, axis=-1, keepdims=True)
        first = jnp.min(jnp.where(left == top, jf, float(n)), axis=-1, keepdims=True)
        hit = jf == first
        picked = picked | hit
        left = jnp.where(hit, -jnp.inf, left)
    return picked & ok


def _flash_update(s, v, m_ref, l_ref, acc_ref):
    m_prev = m_ref[...]
    m_new = jnp.maximum(m_prev, jnp.max(s, axis=-1, keepdims=True))
    alpha = jnp.exp(m_prev - m_new)
    p = jnp.exp(s - m_new)
    l_ref[...] = alpha * l_ref[...] + jnp.sum(p, axis=-1, keepdims=True)
    acc_ref[...] = alpha * acc_ref[...] + _dot(p.astype(BF16), v)
    m_ref[...] = m_new


PQ = 128
PK = 512
P_SEL_PAD = 128
WIN_SPAN = WINDOW + PQ


def _nsa_prompt_body(q_ref, ks_ref, vs_ref, kw_ref, vw_ref, kc_ref, vc_ref, gl_ref, o_ref,
                     kaug_ref, vsb_ref, kwb_ref, vwb_ref, qa_ref, m_ref, l_ref, acc_ref, *, seq):
    i = pl.program_id(2)
    rows = GROUP * PQ
    half = rows // 2
    n_sel = seq // L_SEL
    scale = HEAD_DIM ** -0.5

    @pl.when(i == 0)
    def _():
        kaug_ref[:, 0:HEAD_DIM] = ks_ref[0].astype(BF16)
        key = lax.broadcasted_iota(jnp.int32, (seq, P_SEL_PAD), 0)
        blk = lax.broadcasted_iota(jnp.int32, (seq, P_SEL_PAD), 1)
        kaug_ref[:, HEAD_DIM:] = jnp.where(key // L_SEL == blk, 1.0, 0.0).astype(BF16)
        vsb_ref[...] = vs_ref[0].astype(BF16)
        kwb_ref[...] = kw_ref[0].astype(BF16)
        vwb_ref[...] = vw_ref[0].astype(BF16)

    q = q_ref[0]
    qs = jnp.concatenate([q[:, r * HEAD_DIM:(r + 1) * HEAD_DIM] for r in range(GROUP)], axis=0)
    qb = (qs * scale).astype(BF16)
    qa_ref[:, 0:HEAD_DIM] = qb
    t_q = i * PQ + lax.broadcasted_iota(jnp.int32, (PQ, 1), 0)
    t_rows = jnp.concatenate([t_q] * GROUP, axis=0)

    kc = kc_ref[0, 0, 0]
    n_cmp = kc.shape[0]
    s = _dot_nt(qb, kc)
    cmp_end = lax.broadcasted_iota(jnp.int32, (rows, n_cmp), 1) * CMP_STRIDE + (L_CMP - 1)
    p_cmp = _masked_softmax(s, cmp_end <= t_rows)
    o_cmp = _dot(p_cmp.astype(BF16), vc_ref[0, 0, 0])

    w0 = pl.multiple_of(jnp.clip(i * PQ - WINDOW, 0, seq - WIN_SPAN), PQ)
    s = _dot_nt(qb, kwb_ref[pl.ds(w0, WIN_SPAN), :])
    wpos = w0 + lax.broadcasted_iota(jnp.int32, (rows, WIN_SPAN), 1)
    p_win = _masked_softmax(s, (wpos <= t_rows) & (wpos > t_rows - WINDOW))
    o_win = _dot(p_win.astype(BF16), vwb_ref[pl.ds(w0, WIN_SPAN), :])

    p_sum = p_cmp[0:PQ]
    for r in range(1, GROUP):
        p_sum = p_sum + p_cmp[r * PQ:(r + 1) * PQ]
    t_lane = i * PQ + lax.broadcasted_iota(jnp.int32, (1, PQ), 1)
    sel_t = _select_blocks_t(_importance_t(p_sum.T, n_sel), t_lane)
    pen_t = jnp.concatenate([jnp.where(sel_t, 0.0, NEG), jnp.zeros((P_SEL_PAD - n_sel, PQ), F32)], axis=0)
    pen = pen_t.T.astype(BF16)
    for r in range(GROUP):
        qa_ref[r * PQ:(r + 1) * PQ, HEAD_DIM:] = pen

    m_ref[...] = jnp.full(m_ref.shape, NEG, F32)
    l_ref[...] = jnp.zeros(l_ref.shape, F32)
    acc_ref[...] = jnp.zeros(acc_ref.shape, F32)

    def slc_tile(kt, causal):
        k0 = pl.multiple_of(kt * PK, PK)
        kaug = kaug_ref[pl.ds(k0, PK), :]
        v = vsb_ref[pl.ds(k0, PK), :]
        for h in range(2):
            hs = pl.ds(h * half, half)
            s = _dot_nt(qa_ref[hs, :], kaug)
            if causal:
                kpos = k0 + lax.broadcasted_iota(jnp.int32, (half, PK), 1)
                s = jnp.where(kpos <= t_rows[h * half:(h + 1) * half], s, NEG)
            _flash_update(s, v, m_ref.at[hs], l_ref.at[hs], acc_ref.at[hs])

    n_full = (i * PQ) // PK

    def full_tile(kt, carry):
        slc_tile(kt, False)
        return carry

    lax.fori_loop(0, n_full, full_tile, 0)
    slc_tile(n_full, True)
    o_slc = acc_ref[...] / l_ref[...]

    gate = _sigmoid(gl_ref[0])
    for r in range(GROUP):
        rs = slice(r * PQ, (r + 1) * PQ)
        c0 = r * N_BRANCH
        o = (gate[:, c0:c0 + 1] * o_cmp[rs] + gate[:, c0 + 1:c0 + 2] * o_slc[rs]
             + gate[:, c0 + 2:c0 + 3] * o_win[rs])
        o_ref[0, :, r * HEAD_DIM:(r + 1) * HEAD_DIM] = o.astype(o_ref.dtype)


def nsa_prompt(qkv, cblk, gl):
    b, t, _ = qkv.shape
    qcol = ATTN_W // HEAD_DIM

    def kv_spec(slot):
        return pl.BlockSpec((1, t, HEAD_DIM), lambda bi, g, i: (bi, 0, qcol + slot * N_KV + g))

    n_cmp = cblk.shape[3]
    rows = GROUP * PQ
    return pl.pallas_call(
        functools.partial(_nsa_prompt_body, seq=t),
        grid=(b, N_KV, t // PQ),
        in_specs=[pl.BlockSpec((1, PQ, GROUP * HEAD_DIM), lambda bi, g, i: (bi, i, g)),
                  kv_spec(2), kv_spec(3), kv_spec(4), kv_spec(5),
                  pl.BlockSpec((1, 1, 1, n_cmp, HEAD_DIM), lambda bi, g, i: (bi, 0, g, 0, 0)),
                  pl.BlockSpec((1, 1, 1, n_cmp, HEAD_DIM), lambda bi, g, i: (bi, 1, g, 0, 0)),
                  pl.BlockSpec((1, PQ, LANES), lambda bi, g, i: (bi, i, g))],
        out_specs=pl.BlockSpec((1, PQ, GROUP * HEAD_DIM), lambda bi, g, i: (bi, i, g)),
        out_shape=jax.ShapeDtypeStruct((b, t, ATTN_W), BF16),
        scratch_shapes=[pltpu.VMEM((t, 2 * HEAD_DIM), BF16), pltpu.VMEM((t, HEAD_DIM), BF16),
                        pltpu.VMEM((t, HEAD_DIM), BF16), pltpu.VMEM((t, HEAD_DIM), BF16),
                        pltpu.VMEM((rows, 2 * HEAD_DIM), BF16), pltpu.VMEM((rows, 1), F32), pltpu.VMEM((rows, 1), F32),
                        pltpu.VMEM((rows, HEAD_DIM), F32)],
        compiler_params=_params(("arbitrary",) * 3, 40 << 20),
    )(qkv, qkv, qkv, qkv, qkv, cblk, cblk, gl)


SP = 8
S_KEYS = SP * PAGE_SIZE
S_BLKS = S_KEYS // L_SEL
S_CHUNK = LANES // S_BLKS
S_ROWS = GROUP * S_PAD


def _nsa_sample_body(pt_ref, *refs, t_new, n_steps, n_sel_pad):
    pages = refs[:SP]
    (q_ref, cb_ref, win_ref, gl_ref, o_ref,
     qb_ref, pen_ref, new_ref, ocw_ref, m_ref, l_ref, acc_ref) = refs[SP:]
    i = pl.program_id(1)
    scale = HEAD_DIM ** -0.5
    n_chunks = n_sel_pad // LANES
    n_win = win_ref.shape[1] // (2 * N_KV)
    tok = lax.broadcasted_iota(jnp.int32, (S_PAD, 1), 0)
    t_q = PAST_LEN + tok
    t_rows = jnp.concatenate([t_q] * GROUP, axis=0)
    qcol = ATTN_W

    @pl.when(i == 0)
    def _():
        pad = jnp.zeros((LANES - S_PAD, HEAD_DIM), F32)
        for g in range(N_KV):
            heads = [q_ref[0, :, (g * GROUP + r) * HEAD_DIM:(g * GROUP + r + 1) * HEAD_DIM]
                     for r in range(GROUP)]
            qb_ref[g] = (jnp.concatenate(heads, axis=0) * scale).astype(BF16)
            for slot in range(2, 6):
                c0 = qcol + (slot * N_KV + g) * HEAD_DIM
                new_ref[slot - 2, g] = jnp.concatenate(
                    [q_ref[0, :, c0:c0 + HEAD_DIM], pad], axis=0).astype(BF16)
        m_ref[...] = jnp.full(m_ref.shape, NEG, F32)
        l_ref[...] = jnp.zeros(l_ref.shape, F32)
        acc_ref[...] = jnp.zeros(acc_ref.shape, F32)
        p_sums = []
        for g in range(N_KV):
            qb = qb_ref[g]
            kc = cb_ref[0, 0, g]
            n_cmp = kc.shape[0]
            s = _dot_nt(qb, kc)
            cmp_end = lax.broadcasted_iota(jnp.int32, (S_ROWS, n_cmp), 1) * CMP_STRIDE + (L_CMP - 1)
            p_cmp = _masked_softmax(s, cmp_end <= t_rows)
            ocw_ref[0, g] = _dot(p_cmp.astype(BF16), cb_ref[0, 1, g])
            p_sum = p_cmp[0:S_PAD]
            for r in range(1, GROUP):
                p_sum = p_sum + p_cmp[r * S_PAD:(r + 1) * S_PAD]
            p_sums.append(p_sum)
            kw = win_ref[0, pl.ds(g, n_win, stride=2 * N_KV), :].astype(BF16)
            vw = win_ref[0, pl.ds(N_KV + g, n_win, stride=2 * N_KV), :].astype(BF16)
            s = jnp.concatenate([_dot_nt(qb, kw), _dot_nt(qb, new_ref[2, g])], axis=1)
            lane = lax.broadcasted_iota(jnp.int32, (S_ROWS, n_win + LANES), 1)
            wpos = PAST_LEN - n_win + lane
            mask = (wpos <= t_rows) & (wpos > t_rows - WINDOW) & (lane < n_win + t_new)
            p_win = _masked_softmax(s, mask).astype(BF16)
            ocw_ref[1, g] = _dot(p_win[:, 0:n_win], vw) + _dot(p_win[:, n_win:], new_ref[3, g])
        imp = _importance(jnp.concatenate(p_sums, axis=0), n_sel_pad)
        sel = _select_blocks(imp, jnp.concatenate([t_q] * N_KV, axis=0))
        pen = jnp.where(sel, 0.0, NEG).astype(BF16)
        for g in range(N_KV):
            for ch in range(n_chunks):
                blk = pen[g * S_PAD:(g + 1) * S_PAD, ch * LANES:(ch + 1) * LANES]
                pen_ref[ch, g] = jnp.concatenate([blk] * GROUP, axis=0)

    key = lax.broadcasted_iota(jnp.int32, (S_KEYS, LANES), 0)
    blk = lax.broadcasted_iota(jnp.int32, (S_KEYS, LANES), 1)
    ind = jnp.where((i % S_CHUNK) * S_BLKS + key // L_SEL == blk, 1.0, 0.0).astype(BF16)
    for g in range(N_KV):
        k = jnp.concatenate(
            [p[0, pl.ds(g, PAGE_SIZE, stride=2 * N_KV), :] for p in pages], axis=0)
        v = jnp.concatenate(
            [p[0, pl.ds(N_KV + g, PAGE_SIZE, stride=2 * N_KV), :] for p in pages], axis=0)
        kaug = jnp.concatenate([k.astype(BF16), ind], axis=1)
        qa = jnp.concatenate([qb_ref[g], pen_ref[i // S_CHUNK, g]], axis=1)
        _flash_update(_dot_nt(qa, kaug), v.astype(BF16), m_ref.at[g], l_ref.at[g], acc_ref.at[g])

    @pl.when(i == n_steps - 1)
    def _():
        gate = _sigmoid(gl_ref[0])
        lane = lax.broadcasted_iota(jnp.int32, (S_ROWS, LANES), 1)
        cur_chunk, cur_lane = (PAST_LEN // L_SEL) // LANES, (PAST_LEN // L_SEL) % LANES
        for g in range(N_KV):
            pen_cur = pen_ref[cur_chunk, g][:, cur_lane:cur_lane + 1].astype(F32)
            s = _dot_nt(qb_ref[g], new_ref[0, g]) + pen_cur
            s = jnp.where((PAST_LEN + lane <= t_rows) & (lane < t_new), s, NEG)
            _flash_update(s, new_ref[1, g], m_ref.at[g], l_ref.at[g], acc_ref.at[g])
            o_slc = acc_ref[g] / l_ref[g]
            for r in range(GROUP):
                rs = slice(r * S_PAD, (r + 1) * S_PAD)
                c0 = (g * GROUP + r) * N_BRANCH
                o = (gate[:, c0:c0 + 1] * ocw_ref[0, g][rs] + gate[:, c0 + 1:c0 + 2] * o_slc[rs]
                     + gate[:, c0 + 2:c0 + 3] * ocw_ref[1, g][rs])
                h = g * GROUP + r
                o_ref[0, :, h * HEAD_DIM:(h + 1) * HEAD_DIM] = o.astype(o_ref.dtype)


def nsa_sample(qkv, t_new, cblk, cache_slc, cache_win, page_table, gl):
    b = qkv.shape[0]
    n_pages = page_table.shape[1]
    n_steps = n_pages // SP
    n_cmp = cblk.shape[3]
    n_win = cache_win.shape[1] // (2 * N_KV)
    n_sel = -(-(PAST_LEN + t_new) // L_SEL)
    n_sel_pad = -(-n_sel // LANES) * LANES
    assert n_steps * S_BLKS <= n_sel_pad and n_steps * SP == n_pages

    def page_spec(c):
        return pl.BlockSpec((1, PAGE_SIZE * 2 * N_KV, HEAD_DIM),
                            lambda bi, i, pt: (pt[bi, i * SP + c], 0, 0))

    grid_spec = pltpu.PrefetchScalarGridSpec(
        num_scalar_prefetch=1, grid=(b, n_steps),
        in_specs=[page_spec(c) for c in range(SP)] + [
            pl.BlockSpec((1, S_PAD, QKV_W), lambda bi, i, pt: (bi, 0, 0)),
            pl.BlockSpec((1, 2, N_KV, n_cmp, HEAD_DIM), lambda bi, i, pt: (bi, 0, 0, 0, 0)),
            pl.BlockSpec((1, n_win * 2 * N_KV, HEAD_DIM), lambda bi, i, pt: (bi, 0, 0)),
            pl.BlockSpec((1, S_PAD, LANES), lambda bi, i, pt: (bi, 0, 0))],
        out_specs=pl.BlockSpec((1, S_PAD, ATTN_W), lambda bi, i, pt: (bi, 0, 0)),
        scratch_shapes=[pltpu.VMEM((N_KV, S_ROWS, HEAD_DIM), BF16),
                        pltpu.VMEM((n_sel_pad // LANES, N_KV, S_ROWS, LANES), BF16),
                        pltpu.VMEM((4, N_KV, LANES, HEAD_DIM), BF16),
                        pltpu.VMEM((2, N_KV, S_ROWS, HEAD_DIM), F32),
                        pltpu.VMEM((N_KV, S_ROWS, 1), F32), pltpu.VMEM((N_KV, S_ROWS, 1), F32),
                        pltpu.VMEM((N_KV, S_ROWS, HEAD_DIM), F32)])
    return pl.pallas_call(
        functools.partial(_nsa_sample_body, t_new=t_new, n_steps=n_steps, n_sel_pad=n_sel_pad),
        grid_spec=grid_spec,
        out_shape=jax.ShapeDtypeStruct((b, S_PAD, ATTN_W), F32),
        compiler_params=_params(("arbitrary", "arbitrary"), 40 << 20),
    )(page_table, *([cache_slc] * SP), qkv, cblk, cache_win, gl)


def _pad_rows(x, front, total):
    return jnp.pad(x, ((0, 0), (front, total - front - x.shape[1]), (0, 0)))


def kernel(x_prompt, x_sample, cache_cmp_kv, cache_slc_kv, cache_win_kv, state_conf_conv,
           state_short_conv, page_table, p_prompt, p_sample, g_mix0, w_in0, conv_w0, conv_b0,
           ln_g0, ln_b0, cmp_pe, cmp_w1, cmp_b1, cmp_w2, w_out0, g_mix1, w_in1, sconv_w1, w_out1,
           g_ffn, w_ffn_gate, w_ffn_up, w_ffn_down, g_ple, w_ple_gate, w_ple_proj, g_final):
    bp, seq, d = x_prompt.shape
    bs, t_new, _ = x_sample.shape
    mp, ms = bp * seq, bs * t_new
    d_ff = w_ffn_gate.shape[-1]
    n_pool = cache_cmp_kv.shape[1]
    n_win = cache_win_kv.shape[2]
    tm = 1024

    xp = x_prompt.reshape(mp, d)
    xs = x_sample.reshape(ms, d)
    rope_p = rope_tables(jnp.arange(seq, dtype=jnp.int32))
    rope_s = tuple(jnp.tile(tb, (bs, 1)) for tb in rope_tables(PAST_LEN + jnp.arange(t_new, dtype=jnp.int32)))
    outs = {}
    w_down_bf16 = w_ffn_down.astype(BF16)

    def ffn_ple(xp, xs, i):
        hp, hs = rmsnorm(xp, g_ffn[i], BF16), rmsnorm(xs, g_ffn[i], BF16)
        (gp,), (gs,) = fused_mm([(hp, hs, [W(w_ffn_gate, i), W(w_ffn_up, i)])], _ep_swiglu, [BF16],
                                d_ff, tm, 256)
        (xp,), (xs,) = fused_mm([(gp, gs, [W(w_down_bf16, i)])], _ep_residual, [F32],
                                d, 512, 512, tilex=[(xp, xs)])
        hp, hs = rmsnorm(xp, g_ple[i], BF16), rmsnorm(xs, g_ple[i], BF16)
        pp = p_prompt[i].reshape(mp, -1).astype(BF16)
        ps = p_sample[i].reshape(ms, -1).astype(BF16)
        (xp,), (xs,) = fused_mm([(hp, hs, [W(w_ple_gate, i)]), (pp, ps, [W(w_ple_proj, i)])],
                                _ep_ple, [F32], d, tm, 512, tilex=[(xp, xs)])
        return xp, xs

    e = 0
    hp, hs = rmsnorm(xp, g_mix0[e], BF16), rmsnorm(xs, g_mix0[e], BF16)
    (up,), (us,) = fused_mm([(hp, hs, [W(w_in0, e), W(w_in0, e, col_off=CONF_CH // 256)])], _ep_glu,
                            [F32], CONF_CH, tm, 256)
    (qkv_p,), (qkv_s,) = fused_mm([(hp, hs, [W(w_in0, e, col_off=2 * CONF_CH // KV_W)])], _ep_rope,
                                  [F32], QKV_W, tm, KV_W, rowx=list(zip(rope_p, rope_s)))
    w_gl = w_in0[e][:, 2 * CONF_CH + QKV_W:].reshape(d, N_KV, GROUP * N_BRANCH)
    w_gl = jnp.pad(w_gl, ((0, 0), (0, 0), (0, LANES - GROUP * N_BRANCH))).reshape(1, d, N_KV * LANES)
    (gl_p,), (gl_s,) = fused_mm([(hp, hs, [W(w_gl)])], _ep_plain, [F32], N_KV * LANES, tm, N_KV * LANES)

    up3, us3 = up.reshape(bp, seq, CONF_CH), us.reshape(bs, t_new, CONF_CH)
    conf_w = (conv_w0[e], conv_b0[e], ln_g0[e], ln_b0[e])
    a_p = conf_conv(up3, jnp.zeros((bp, HALO, CONF_CH), F32), *conf_w, tt=128, out_dtype=BF16)
    st = state_conf_conv[e]
    a_s = conf_conv(_pad_rows(us3, 0, S_PAD), _pad_rows(st, HALO - st.shape[1], HALO), *conf_w,
                    tt=S_PAD, out_dtype=F32)
    outs["conf_p"] = up3[:, seq - (CONF_W - 1):]
    outs["conf_s"] = jnp.concatenate([st, us3], axis=1)[:, t_new:]

    qkv_p3, qkv_s3 = qkv_p.reshape(bp, seq, QKV_W), qkv_s.reshape(bs, t_new, QKV_W)
    kv_shape = lambda x: x.reshape(x.shape[0], x.shape[1], 2, N_KV, HEAD_DIM)
    for name, slot in (("cmp", 0), ("slc", 2), ("win", 4)):
        c0 = ATTN_W + slot * KV_W
        outs[name + "_p"] = kv_shape(qkv_p3[:, :, c0:c0 + 2 * KV_W])
        outs[name + "_s"] = kv_shape(qkv_s3[:, :, c0:c0 + 2 * KV_W])
    outs["win_p"] = outs["win_p"][:, seq - min(WINDOW, seq):]
    outs["win_s"] = jnp.concatenate([cache_win_kv[e], outs["win_s"]], axis=1)[:, t_new:]

    pages_p = seq // PAGE_SIZE
    ident = jnp.arange(bp * pages_p, dtype=jnp.int32).reshape(bp, pages_p)
    w1cat = cmp_w1_cat(cmp_w1[e])
    proj_p = cmp_project(qkv_p.reshape(bp * pages_p, PAGE_SIZE, QKV_W), False, ATTN_W // PAGE_ROW_W,
                         ident, w1cat)
    row_view = lambda c: c.reshape(-1, c.shape[-4] * 2 * N_KV, HEAD_DIM)
    pool_pages = page_table + e * n_pool
    proj_s = cmp_project(row_view(cache_cmp_kv), True, 0, pool_pages, w1cat)
    cblk_p = cmp_finish(proj_p, cmp_pe[e], w1cat, cmp_b1[e], cmp_w2[e])
    cblk_s = cmp_finish(proj_s, cmp_pe[e], w1cat, cmp_b1[e], cmp_w2[e])

    attn_p = nsa_prompt(qkv_p3, cblk_p, gl_p.reshape(bp, seq, N_KV * LANES))
    gl_s3 = gl_s.reshape(bs, t_new, N_KV, LANES)[..., :GROUP * N_BRANCH].reshape(bs, t_new, -1)
    gl_s3 = jnp.pad(gl_s3, ((0, 0), (0, S_PAD - t_new), (0, LANES - gl_s3.shape[-1])))
    attn_s = nsa_sample(_pad_rows(qkv_s3, 0, S_PAD), t_new, cblk_s, row_view(cache_slc_kv),
                        row_view(cache_win_kv[e]), pool_pages, gl_s3)
    a_p2, a_s2 = a_p.reshape(mp, CONF_CH), a_s[:, :t_new].reshape(ms, CONF_CH).astype(BF16)
    at_p2, at_s2 = attn_p.reshape(mp, ATTN_W), attn_s[:, :t_new].reshape(ms, ATTN_W).astype(BF16)
    (xp,), (xs,) = fused_mm(
        [(a_p2, a_s2, [W(w_out0, e, k_rows=CONF_CH, row_blk=0)]),
         (at_p2, at_s2, [W(w_out0, e, k_rows=ATTN_W, row_blk=CONF_CH // ATTN_W)])],
        _ep_residual, [F32], d, tm, 512, tilex=[(xp, xs)])
    xp, xs = ffn_ple(xp, xs, 0)

    o = 0
    hp, hs = rmsnorm(xp, g_mix1[o], BF16), rmsnorm(xs, g_mix1[o], BF16)
    (bg_p, cv_p), (bg_s, cv_s) = fused_mm(
        [(hp, hs, [W(w_in1, o), W(w_in1, o, col_off=SC_CH // 256), W(w_in1, o, col_off=2 * SC_CH // 256)])],
        _ep_shortconv, [F32, F32], SC_CH, 512, 256)
    cv_p3, cv_s3 = cv_p.reshape(bp, seq, SC_CH), cv_s.reshape(bs, t_new, SC_CH)
    y_p = short_conv(cv_p3, bg_p.reshape(bp, seq, SC_CH), jnp.zeros((bp, SUBLANES, SC_CH), F32),
                     sconv_w1[o], tt=256, out_dtype=BF16)
    st = state_short_conv[o]
    y_s = short_conv(_pad_rows(cv_s3, 0, S_PAD), _pad_rows(bg_s.reshape(bs, t_new, SC_CH), 0, S_PAD),
                     _pad_rows(st, SUBLANES - st.shape[1], SUBLANES), sconv_w1[o], tt=S_PAD,
                     out_dtype=F32)
    outs["sc_p"] = cv_p3[:, seq - (SC_W - 1):]
    outs["sc_s"] = jnp.concatenate([st, cv_s3], axis=1)[:, t_new:]
    (xp,), (xs,) = fused_mm(
        [(y_p.reshape(mp, SC_CH), y_s[:, :t_new].reshape(ms, SC_CH).astype(BF16), [W(w_out1, o)])],
        _ep_residual, [F32], d, tm, 512, tilex=[(xp, xs)])
    xp, xs = ffn_ple(xp, xs, 1)

    y_p = rmsnorm(xp, g_final, F32).reshape(bp, seq, d)
    y_s = rmsnorm(xs, g_final, F32).reshape(bs, t_new, d)
    st1 = lambda x: x[None]
    return (y_p, y_s, st1(outs["cmp_p"]), st1(outs["cmp_s"]), st1(outs["slc_p"]), st1(outs["slc_s"]),
            st1(outs["win_p"]), st1(outs["win_s"]), st1(outs["conf_p"]), st1(outs["conf_s"]),
            st1(outs["sc_p"]), st1(outs["sc_s"]))
```

```python
import functools

import jax
import jax.numpy as jnp
from jax import lax
from jax.experimental import pallas as pl
from jax.experimental.pallas import tpu as pltpu

F32 = jnp.float32
BF16 = jnp.bfloat16

V7X_VMEM_BYTES = 64 * 1024 * 1024
LANES = 128
SUBLANES = 8

D_MODEL = 4096
PAST_LEN = 16384
PAGE_SIZE = 128
N_HEADS = 16
HEAD_DIM = 128
N_KV = 4
GROUP = N_HEADS // N_KV
ATTN_W = N_HEADS * HEAD_DIM
KV_W = N_KV * HEAD_DIM
ROPE_DIM = HEAD_DIM // 4
ROPE_THETA = 500000.0
L_CMP = 32
CMP_STRIDE = 16
L_SEL = 64
N_SEL = 16
WINDOW = 512
N_BRANCH = 3
CONF_CH = D_MODEL // 2
CONF_W = 31
SC_CH = D_MODEL
SC_W = 3
RMS_EPS = 1e-6
LN_EPS = 1e-5
NEG = -1e30
FORCE = 1e9
QKV_W = ATTN_W + 6 * KV_W
QKVG_W = QKV_W + KV_W
SUB_PER_PAGE = PAGE_SIZE // CMP_STRIDE
SUB_FLAT = CMP_STRIDE * HEAD_DIM
PAGE_ROW_W = 2 * KV_W
HALO = 32
S_PAD = 8


def _vmem_limit(n_bytes):
    return int(min(V7X_VMEM_BYTES - (4 << 20), max(n_bytes, 16 << 20)))


def _params(sem, vmem):
    return pltpu.CompilerParams(dimension_semantics=sem, vmem_limit_bytes=_vmem_limit(vmem))


def _dot(a, b):
    return jnp.dot(a, b, preferred_element_type=F32)


def _dot_nt(a, b):
    return lax.dot_general(a, b, (((1,), (1,)), ((), ())), preferred_element_type=F32)


def _sigmoid(x):
    return jax.nn.sigmoid(x)


def _rmsnorm_body(x_ref, g_ref, o_ref):
    x = x_ref[...]
    y = x * lax.rsqrt(jnp.mean(x * x, axis=-1, keepdims=True) + RMS_EPS)
    o_ref[...] = (y * g_ref[...]).astype(o_ref.dtype)


def rmsnorm(x, g, out_dtype, rows=256):
    m, d = x.shape
    tr = min(rows, m)
    return pl.pallas_call(
        _rmsnorm_body,
        grid=(m // tr,),
        in_specs=[pl.BlockSpec((tr, d), lambda i: (i, 0)),
                  pl.BlockSpec((1, d), lambda i: (0, 0))],
        out_specs=pl.BlockSpec((tr, d), lambda i: (i, 0)),
        out_shape=jax.ShapeDtypeStruct((m, d), out_dtype),
        compiler_params=_params(("arbitrary",), 6 * tr * d * 4),
    )(x, g.reshape(1, d))


class W:
    def __init__(self, arr, layer=0, k_rows=None, row_blk=0, col_off=0):
        self.arr = arr
        self.layer = layer
        self.k_rows = arr.shape[1] if k_rows is None else k_rows
        self.row_blk = row_blk
        self.col_off = col_off


class Lhs:
    def __init__(self, a_p, a_s, ws, gain=None, ssq=None):
        self.a_p, self.a_s, self.ws, self.gain, self.ssq = a_p, a_s, ws, gain, ssq


def _fold_lanes(x):
    out = x[:, 0:LANES]
    for c in range(1, x.shape[1] // LANES):
        out = out + x[:, c * LANES:(c + 1) * LANES]
    return out


def _mm_body(*refs, meta, n_rx, n_tx, n_main, emit_bf16, emit_ssq, epilogue, m_axis):
    it = iter(refs)
    groups = []
    for n_w, has_gain, has_ssq in meta:
        a = (next(it), next(it))
        gain = next(it) if has_gain else None
        ssq = (next(it), next(it)) if has_ssq else None
        groups.append((a, gain, ssq, [next(it) for _ in range(n_w)]))
    rx = [[next(it) for _ in range(n_rx)] for _ in range(2)]
    tx = [[next(it) for _ in range(n_tx)] for _ in range(2)]
    n_out = n_main + emit_bf16 + emit_ssq
    outs = [[next(it) for _ in range(n_out)] for _ in range(2)]
    i = pl.program_id(m_axis)
    j = pl.program_id(1 - m_axis)
    wb = []
    for _, gain, _, ws in groups:
        if gain is None:
            wb.append([w[...].astype(BF16) for w in ws])
        else:
            wb.append([(w[...] * gain[...]).astype(BF16) for w in ws])

    def run(which):
        dots = []
        for (a, _, ssq, _), wbs in zip(groups, wb):
            lhs = a[which][...]
            scale = None
            if ssq is not None:
                total = jnp.sum(ssq[which][...], axis=-1, keepdims=True)
                scale = lax.rsqrt(total / lhs.shape[1] + RMS_EPS)
            for w in wbs:
                d = _dot(lhs, w)
                dots.append(d if scale is None else d * scale)
        res = list(epilogue(dots, [r[...] for r in rx[which]], [t[...] for t in tx[which]], j))
        if emit_bf16:
            res.append(res[0])
        for o, r in zip(outs[which], res):
            o[...] = r.astype(o.dtype)
        if emit_ssq:
            part = _fold_lanes(res[0] * res[0])
            acc = outs[which][-1]

            @pl.when(j == 0)
            def _():
                acc[...] = part

            @pl.when(j > 0)
            def _():
                acc[...] += part

    run(0)

    @pl.when(i == 0)
    def _():
        run(1)

    if m_axis == 0:
        @pl.when(i == 1)
        def _():
            for o in outs[1]:
                o[...] = jnp.zeros(o.shape, o.dtype)


def fused_mm(groups, epilogue, out_dtypes, n_cols, tm, tn, rowx=(), tilex=(), m_outer=False,
             emit_bf16=False, emit_ssq=False):
    mp = groups[0].a_p.shape[0]
    ms = groups[0].a_s.shape[0]
    nj, ni = n_cols // tn, mp // tm
    assert nj * tn == n_cols and ni * tm == mp
    m_axis = 0 if m_outer else 1
    assert m_outer or not emit_ssq
    spare = 1 if m_outer else 0

    def spec(shape, fn, **kw):
        return pl.BlockSpec(shape, lambda *g: fn(g[m_axis], g[1 - m_axis]), **kw)

    resident = dict(pipeline_mode=pl.Buffered(1)) if m_outer else {}
    args, in_specs, meta = [], [], []
    vmem = 0
    for g in groups:
        k = g.a_p.shape[1]
        args += [g.a_p, g.a_s]
        in_specs += [spec((tm, k), lambda i, j: (i, 0), **resident), spec((ms, k), lambda i, j: (0, 0))]
        vmem += ((1 if m_outer else 2) * tm + 2 * ms) * k * g.a_p.dtype.itemsize
        if g.gain is not None:
            args.append(g.gain.reshape(k, 1))
            in_specs.append(spec((k, 1), lambda i, j: (0, 0)))
            vmem += 2 * k * LANES * 4
        if g.ssq is not None:
            args += list(g.ssq)
            in_specs += [spec((tm, g.ssq[0].shape[1]), lambda i, j: (i, 0)),
                         spec(g.ssq[1].shape, lambda i, j: (0, 0))]
            vmem += 2 * (tm + ms) * g.ssq[0].shape[1] * 4
        for w in g.ws:
            assert w.k_rows == k
            args.append(w.arr)
            in_specs.append(spec((None, k, tn), functools.partial(
                lambda i, j, la, rb, co: (la, rb, co + j), la=w.layer, rb=w.row_blk, co=w.col_off)))
            vmem += k * tn * (2 * w.arr.dtype.itemsize + 2)
        meta.append((len(g.ws), g.gain is not None, g.ssq is not None))
    for which in (0, 1):
        for tab_p, tab_s in rowx:
            if which == 0:
                per = tab_p.shape[0] // tm
                args.append(tab_p)
                in_specs.append(spec((tm, tab_p.shape[1]),
                                     functools.partial(lambda i, j, per: (i % per, 0), per=per)))
                vmem += 2 * tm * tab_p.shape[1] * 4
            else:
                args.append(tab_s)
                in_specs.append(spec(tab_s.shape, lambda i, j: (0, 0)))
    for which in (0, 1):
        for t_p, t_s in tilex:
            if which == 0:
                args.append(t_p)
                in_specs.append(spec((tm, tn), lambda i, j: (i, j)))
            else:
                args.append(t_s)
                in_specs.append(spec((ms, tn), lambda i, j: (0, j)))
    kinds = [(dt, tn, nj) for dt in out_dtypes]
    if emit_bf16:
        kinds.append((BF16, tn, nj))
    if emit_ssq:
        kinds.append((F32, LANES, 1))
    out_shape, out_specs = [], []
    for which in (0, 1):
        for dt, width, nblk in kinds:
            col = (lambda i, j: j) if nblk > 1 else (lambda i, j: 0)
            if which == 0:
                out_shape.append(jax.ShapeDtypeStruct((mp, nblk * width), dt))
                out_specs.append(spec((tm, width), functools.partial(lambda i, j, col: (i, col(i, j)), col=col)))
            else:
                out_shape.append(jax.ShapeDtypeStruct((ms, (nblk + spare) * width), dt))
                out_specs.append(spec((ms, width), functools.partial(
                    lambda i, j, col, nblk: (0, jnp.where(i == 0, col(i, j), nblk) if m_outer else col(i, j)),
                    col=col, nblk=nblk)))
    n_dots = sum(len(g.ws) for g in groups)
    vmem += (2 * (len(tilex) + len(kinds)) + n_dots + 2) * tm * tn * 4
    body = functools.partial(
        _mm_body, meta=tuple(meta), n_rx=len(rowx), n_tx=len(tilex), n_main=len(out_dtypes),
        emit_bf16=emit_bf16, emit_ssq=emit_ssq, epilogue=epilogue, m_axis=m_axis)
    res = pl.pallas_call(
        body, grid=(ni, nj) if m_outer else (nj, ni), in_specs=in_specs, out_specs=out_specs,
        out_shape=out_shape, compiler_params=_params(("arbitrary", "arbitrary"), vmem + (6 << 20)),
    )(*args)
    n = len(kinds)
    sample = [r[:, :nblk * width] for r, (_, width, nblk) in zip(res[n:], kinds)]
    return res[:n], sample


def _ep_glu(dots, rx, tx, j):
    return [dots[0] * _sigmoid(dots[1])]


def _ep_swiglu(dots, rx, tx, j):
    return [jax.nn.silu(dots[0]) * dots[1]]


def _ep_residual(dots, rx, tx, j):
    return [tx[0] + sum(dots[1:], dots[0])]


def _ep_ple(dots, rx, tx, j):
    return [tx[0] + _sigmoid(dots[0]) * dots[1]]


def _ep_shortconv(dots, rx, tx, j):
    return [dots[0], dots[1] * dots[2]]


def _ep_rope(dots, rx, tx, j):
    z = dots[0]
    cos, sin_lo, sin_hi = rx
    half = ROPE_DIM // 2
    heads = []
    for h in range(z.shape[1] // HEAD_DIM):
        x = z[:, h * HEAD_DIM:(h + 1) * HEAD_DIM]
        heads.append(x * cos + pltpu.roll(x, half, 1) * sin_hi
                     + pltpu.roll(x, HEAD_DIM - half, 1) * sin_lo)
    roped = jnp.concatenate(heads, axis=1)
    slot = j - ATTN_W // z.shape[1]
    is_v = jnp.logical_and(slot >= 0, slot % 2 == 1)
    lane = lax.broadcasted_iota(jnp.int32, z.shape, 1)
    gates = jnp.where(lane < N_HEADS * N_BRANCH, z, 0.0)
    return [jnp.where(slot == 6, gates, jnp.where(is_v, z, roped))]


def rope_tables(pos):
    half = ROPE_DIM // 2
    inv = ROPE_THETA ** (-2.0 * jnp.arange(half, dtype=F32) / ROPE_DIM)
    ang = pos.astype(F32)[:, None] * inv[None, :]
    cos, sin = jnp.cos(ang), jnp.sin(ang)
    n = pos.shape[0]
    rest = HEAD_DIM - ROPE_DIM
    c = jnp.concatenate([cos, cos, jnp.ones((n, rest), F32)], axis=1)
    s_lo = jnp.concatenate([-sin, jnp.zeros((n, half + rest), F32)], axis=1)
    s_hi = jnp.concatenate([jnp.zeros((n, half), F32), sin, jnp.zeros((n, rest), F32)], axis=1)
    return c, s_lo, s_hi


def _conf_core(xw_ref, cw_ref, cb_ref, g_ref, b_ref, cbuf_ref, o_ref, tt):
    last = CONF_W - 1
    for c in range(CONF_CH // LANES):
        cs = slice(c * LANES, (c + 1) * LANES)
        acc = None
        for r in range(SUBLANES):
            y = None
            for a in range((last - r) // SUBLANES + 1):
                lo = HALO - SUBLANES * (a + 1)
                term = xw_ref[lo:lo + tt + SUBLANES, cs] * cw_ref[last - SUBLANES * a - r:last - SUBLANES * a - r + 1, cs]
                y = term if y is None else y + term
            part = y[SUBLANES - r:SUBLANES - r + tt]
            acc = part if acc is None else acc + part
        cbuf_ref[:, cs] = acc
    c = cbuf_ref[...] + cb_ref[...]
    mu = jnp.mean(c, axis=-1, keepdims=True)
    var = jnp.mean(jnp.square(c - mu), axis=-1, keepdims=True)
    y = (c - mu) * lax.rsqrt(var + LN_EPS) * g_ref[...] + b_ref[...]
    o_ref[0] = jax.nn.silu(y).astype(o_ref.dtype)


def _conf_body(prev_ref, halo_ref, x_ref, cw_ref, cb_ref, g_ref, b_ref, o_ref, xw_ref, cbuf_ref, *, tt):
    first = pl.program_id(1) == 0
    xw_ref[0:HALO] = jnp.where(first, prev_ref[0], halo_ref[0])
    xw_ref[HALO:HALO + tt] = x_ref[0]
    _conf_core(xw_ref, cw_ref, cb_ref, g_ref, b_ref, cbuf_ref, o_ref, tt)


def conf_conv(u, prev, cw, cb, ln_g, ln_b, tt, out_dtype):
    b, t, c = u.shape
    if tt < HALO:
        assert t == tt
        halo_spec = pl.BlockSpec((1, HALO, c), lambda bi, i: (bi, 0, 0))
        halo_arr = prev
    else:
        hb = tt // HALO
        halo_spec = pl.BlockSpec((1, HALO, c), lambda bi, i: (bi, jnp.maximum(i * hb - 1, 0), 0))
        halo_arr = u
    cwp = jnp.pad(cw, ((0, HALO - cw.shape[0]), (0, 0)))
    row = lambda v: v.reshape(1, c)
    const = lambda bi, i: (0, 0)
    return pl.pallas_call(
        functools.partial(_conf_body, tt=tt),
        grid=(b, t // tt),
        in_specs=[pl.BlockSpec((1, HALO, c), lambda bi, i: (bi, 0, 0)), halo_spec,
                  pl.BlockSpec((1, tt, c), lambda bi, i: (bi, i, 0)),
                  pl.BlockSpec((HALO, c), const), pl.BlockSpec((1, c), const),
                  pl.BlockSpec((1, c), const), pl.BlockSpec((1, c), const)],
        out_specs=pl.BlockSpec((1, tt, c), lambda bi, i: (bi, i, 0)),
        out_shape=jax.ShapeDtypeStruct((b, t, c), out_dtype),
        scratch_shapes=[pltpu.VMEM((tt + HALO, c), F32), pltpu.VMEM((tt, c), F32)],
        compiler_params=_params(("arbitrary", "arbitrary"), 10 * (tt + HALO) * c * 4),
    )(prev, halo_arr, u, cwp, row(cb), row(ln_g), row(ln_b))


def _short_body(prev_ref, halo_ref, cv_ref, bg_ref, w_ref, o_ref, xw_ref, *, tt):
    first = pl.program_id(1) == 0
    xw_ref[0:SUBLANES] = jnp.where(first, prev_ref[0], halo_ref[0])
    xw_ref[SUBLANES:SUBLANES + tt] = cv_ref[0]
    base = SUBLANES - (SC_W - 1)
    conv = xw_ref[pl.ds(base, tt), :] * w_ref[0:1, :]
    for k in range(1, SC_W):
        conv = conv + xw_ref[pl.ds(base + k, tt), :] * w_ref[k:k + 1, :]
    o_ref[0] = (bg_ref[0] * conv).astype(o_ref.dtype)


def short_conv(cv, bg, prev, w, tt, out_dtype):
    b, t, c = cv.shape
    hb = tt // SUBLANES
    wp = jnp.pad(w, ((0, SUBLANES - w.shape[0]), (0, 0)))
    return pl.pallas_call(
        functools.partial(_short_body, tt=tt),
        grid=(b, t // tt),
        in_specs=[pl.BlockSpec((1, SUBLANES, c), lambda bi, i: (bi, 0, 0)),
                  pl.BlockSpec((1, SUBLANES, c), lambda bi, i: (bi, jnp.maximum(i * hb - 1, 0), 0)),
                  pl.BlockSpec((1, tt, c), lambda bi, i: (bi, i, 0)),
                  pl.BlockSpec((1, tt, c), lambda bi, i: (bi, i, 0)),
                  pl.BlockSpec((SUBLANES, c), lambda bi, i: (0, 0))],
        out_specs=pl.BlockSpec((1, tt, c), lambda bi, i: (bi, i, 0)),
        out_shape=jax.ShapeDtypeStruct((b, t, c), out_dtype),
        scratch_shapes=[pltpu.VMEM((tt + SUBLANES, c), F32)],
        compiler_params=_params(("arbitrary", "arbitrary"), 10 * (tt + SUBLANES) * c * 4),
    )(prev, cv, cv, bg, wp)


CMP_PAGES = 8
SUB_PITCH = 24


def _cmp_proj_body(pt_ref, *refs, row_major):
    pages = refs[:CMP_PAGES]
    w1_ref, p_ref, kbuf_ref, x_ref = refs[CMP_PAGES:]
    rows = CMP_PAGES * SUB_PER_PAGE
    for kv in range(2):
        for g in range(N_KV):
            kvg = kv * N_KV + g
            for c in range(CMP_PAGES):
                if row_major:
                    tok = pages[c][0, pl.ds(kvg, PAGE_SIZE, stride=2 * N_KV), :]
                else:
                    tok = pages[c][0, :, kvg * HEAD_DIM:(kvg + 1) * HEAD_DIM]
                for m in range(SUB_PER_PAGE):
                    r0 = (c * SUB_PER_PAGE + m) * SUB_PITCH
                    kbuf_ref[g, r0:r0 + CMP_STRIDE, :] = tok[m * CMP_STRIDE:(m + 1) * CMP_STRIDE]
            for s in range(CMP_STRIDE):
                x_ref[g * rows:(g + 1) * rows, s * HEAD_DIM:(s + 1) * HEAD_DIM] = (
                    kbuf_ref[g, pl.ds(s, rows, stride=SUB_PITCH), :])
        p = _dot(x_ref[...].astype(BF16), w1_ref[kv])
        for g in range(N_KV):
            p_ref[0, kv, g] = p[g * rows:(g + 1) * rows]


def cmp_w1_cat(w1):
    w = w1.reshape(2, 2, SUB_FLAT, HEAD_DIM).transpose(0, 2, 1, 3)
    return w.reshape(2, SUB_FLAT, 2 * HEAD_DIM).astype(BF16)


def cmp_project(pages_arr, row_major, col_blk, page_table, w1cat):
    b, n_pages = page_table.shape
    n_sub = n_pages * SUB_PER_PAGE
    steps = n_pages // CMP_PAGES
    assert steps * CMP_PAGES == n_pages
    blk = (1, PAGE_SIZE * 2 * N_KV, HEAD_DIM) if row_major else (1, PAGE_SIZE, PAGE_ROW_W)

    def page_spec(c):
        return pl.BlockSpec(blk, lambda bi, i, pt: (pt[bi, i * CMP_PAGES + c], 0, col_blk))

    rows = CMP_PAGES * SUB_PER_PAGE
    grid_spec = pltpu.PrefetchScalarGridSpec(
        num_scalar_prefetch=1, grid=(b, steps),
        in_specs=[page_spec(c) for c in range(CMP_PAGES)] + [
            pl.BlockSpec((2, SUB_FLAT, 2 * HEAD_DIM), lambda bi, i, pt: (0, 0, 0))],
        out_specs=pl.BlockSpec((1, 2, N_KV, rows, 2 * HEAD_DIM), lambda bi, i, pt: (bi, 0, 0, i, 0)),
        scratch_shapes=[pltpu.VMEM((N_KV, rows * SUB_PITCH, HEAD_DIM), F32),
                        pltpu.VMEM((N_KV * rows, SUB_FLAT), F32)])
    return pl.pallas_call(
        functools.partial(_cmp_proj_body, row_major=row_major), grid_spec=grid_spec,
        out_shape=jax.ShapeDtypeStruct((b, 2, N_KV, n_sub, 2 * HEAD_DIM), F32),
        compiler_params=_params(("arbitrary", "arbitrary"), 32 << 20),
    )(page_table, *([pages_arr] * CMP_PAGES), w1cat)


def _cmp_finish_body(p_ref, pe_ref, w1_ref, b1_ref, w2_ref, o_ref):
    n_sub = p_ref.shape[-2]
    w1 = w1_ref[0]
    pe_term = (_dot(pe_ref[0, 0].astype(BF16), w1)[0:1, 0:HEAD_DIM]
               + _dot(pe_ref[0, 1].astype(BF16), w1)[0:1, HEAD_DIM:])
    p = p_ref[0, 0, 0]
    p1_next = pltpu.roll(p[:, HEAD_DIM:], n_sub - 1, 0)
    pre = p[:, 0:HEAD_DIM] + p1_next + (b1_ref[0] + pe_term)
    o_ref[0, 0, 0] = _dot(jax.nn.gelu(pre).astype(BF16), w2_ref[0].astype(BF16)).astype(o_ref.dtype)


def cmp_finish(p, pe, w1cat, b1, w2):
    b, _, _, n_sub, _ = p.shape
    pe_rows = jnp.broadcast_to(pe.reshape(2, 2, 1, SUB_FLAT), (2, 2, SUBLANES, SUB_FLAT))
    return pl.pallas_call(
        _cmp_finish_body, grid=(b, 2, N_KV),
        in_specs=[pl.BlockSpec((1, 1, 1, n_sub, 2 * HEAD_DIM), lambda bi, kv, g: (bi, kv, g, 0, 0)),
                  pl.BlockSpec((1, 2, SUBLANES, SUB_FLAT), lambda bi, kv, g: (kv, 0, 0, 0)),
                  pl.BlockSpec((1, SUB_FLAT, 2 * HEAD_DIM), lambda bi, kv, g: (kv, 0, 0)),
                  pl.BlockSpec((1, 1, HEAD_DIM), lambda bi, kv, g: (kv, 0, 0)),
                  pl.BlockSpec((1, HEAD_DIM, HEAD_DIM), lambda bi, kv, g: (kv, 0, 0))],
        out_specs=pl.BlockSpec((1, 1, 1, n_sub, HEAD_DIM), lambda bi, kv, g: (bi, kv, g, 0, 0)),
        out_shape=jax.ShapeDtypeStruct((b, 2, N_KV, n_sub, HEAD_DIM), BF16),
        compiler_params=_params(("arbitrary",) * 3, 16 << 20),
    )(p, pe_rows, w1cat, b1.reshape(2, 1, HEAD_DIM), w2)


def _masked_softmax(s, mask):
    s = jnp.where(mask, s, NEG)
    e = jnp.exp(s - jnp.max(s, axis=-1, keepdims=True))
    return jnp.where(mask, e / jnp.sum(e, axis=-1, keepdims=True), 0.0)


def _split3(x):
    x1 = x.astype(BF16)
    r1 = x - x1.astype(F32)
    x2 = r1.astype(BF16)
    x3 = (r1 - x2.astype(F32)).astype(BF16)
    return x1, x2, x3


def _overlap_matrix(n_cmp, n_sel):
    n = lax.broadcasted_iota(jnp.int32, (n_cmp, n_sel), 0) * CMP_STRIDE
    j = lax.broadcasted_iota(jnp.int32, (n_cmp, n_sel), 1) * L_SEL
    return jnp.where(jnp.logical_and(n < j + L_SEL, n + L_CMP > j), 1.0, 0.0).astype(BF16)


def _importance(p_sum, n_sel):
    ov = _overlap_matrix(p_sum.shape[1], n_sel)
    a, b, c = _split3(p_sum)
    return _dot(a, ov) + _dot(b, ov) + _dot(c, ov)


def _importance_t(p_sum_t, n_sel):
    n_cmp = p_sum_t.shape[0]
    j = lax.broadcasted_iota(jnp.int32, (n_sel, n_cmp), 0) * L_SEL
    n = lax.broadcasted_iota(jnp.int32, (n_sel, n_cmp), 1) * CMP_STRIDE
    ov_t = jnp.where(jnp.logical_and(n < j + L_SEL, n + L_CMP > j), 1.0, 0.0).astype(BF16)
    a, b, c = _split3(p_sum_t)
    return _dot(ov_t, a) + _dot(ov_t, b) + _dot(ov_t, c)


def _select_blocks_t(imp_t, t_pos):
    n, r = imp_t.shape
    j = lax.broadcasted_iota(jnp.int32, (n, r), 0)
    cur = t_pos // L_SEL
    forced = (j == 0) | (j == cur) | (j == cur - 1)
    valid = j * L_SEL <= t_pos
    score = jnp.where(forced, FORCE, jnp.where(valid, imp_t, NEG))
    rank = jnp.zeros((n, r), F32)
    for k in range(n):
        row = score[k:k + 1, :]
        ahead = (row > score) | ((row == score) & (j > k))
        rank = rank + jnp.where(ahead, 1.0, 0.0)
    return (rank < N_SEL) & (score > 0.5 * NEG)


def _select_blocks(imp, t_pos):
    r, n = imp.shape
    j = lax.broadcasted_iota(jnp.int32, (r, n), 1)
    cur = t_pos // L_SEL
    forced = (j == 0) | (j == cur) | (j == cur - 1)
    valid = j * L_SEL <= t_pos
    score = jnp.where(forced, FORCE, jnp.where(valid, imp, NEG))
    ok = score > 0.5 * NEG
    picked = jnp.zeros((r, n), jnp.bool_)
    left = score
    jf = j.astype(F32)
    for _ in range(N_SEL):
        top = jnp.max(left, axis=-1, keepdims=True)
        first = jnp.min(jnp.where(left == top, jf, float(n)), axis=-1, keepdims=True)
        hit = jf == first
        picked = picked | hit
        left = jnp.where(hit, -jnp.inf, left)
    return picked & ok


def _flash_update(s, v, m_ref, l_ref, acc_ref):
    m_prev = m_ref[...]
    m_new = jnp.maximum(m_prev, jnp.max(s, axis=-1, keepdims=True))
    alpha = jnp.exp(m_prev - m_new)
    p = jnp.exp(s - m_new)
    l_ref[...] = alpha * l_ref[...] + jnp.sum(p, axis=-1, keepdims=True)
    acc_ref[...] = alpha * acc_ref[...] + _dot(p.astype(BF16), v)
    m_ref[...] = m_new


PQ = 128
PK = 512
P_SEL_PAD = 128
WIN_SPAN = WINDOW + PQ


def _nsa_prompt_body(q_ref, ks_ref, vs_ref, kw_ref, vw_ref, kc_ref, vc_ref, gl_ref, o_ref,
                     kaug_ref, vsb_ref, kwb_ref, vwb_ref, qa_ref, m_ref, l_ref, acc_ref, *, seq):
    i = pl.program_id(2)
    rows = GROUP * PQ
    half = rows // 2
    n_sel = seq // L_SEL
    scale = HEAD_DIM ** -0.5

    @pl.when(i == 0)
    def _():
        kaug_ref[:, 0:HEAD_DIM] = ks_ref[0].astype(BF16)
        key = lax.broadcasted_iota(jnp.int32, (seq, P_SEL_PAD), 0)
        blk = lax.broadcasted_iota(jnp.int32, (seq, P_SEL_PAD), 1)
        kaug_ref[:, HEAD_DIM:] = jnp.where(key // L_SEL == blk, 1.0, 0.0).astype(BF16)
        vsb_ref[...] = vs_ref[0].astype(BF16)
        kwb_ref[...] = kw_ref[0].astype(BF16)
        vwb_ref[...] = vw_ref[0].astype(BF16)

    q = q_ref[0]
    qs = jnp.concatenate([q[:, r * HEAD_DIM:(r + 1) * HEAD_DIM] for r in range(GROUP)], axis=0)
    qb = (qs * scale).astype(BF16)
    qa_ref[:, 0:HEAD_DIM] = qb
    t_q = i * PQ + lax.broadcasted_iota(jnp.int32, (PQ, 1), 0)
    t_rows = jnp.concatenate([t_q] * GROUP, axis=0)

    kc = kc_ref[0, 0, 0]
    n_cmp = kc.shape[0]
    s = _dot_nt(qb, kc)
    cmp_end = lax.broadcasted_iota(jnp.int32, (rows, n_cmp), 1) * CMP_STRIDE + (L_CMP - 1)
    p_cmp = _masked_softmax(s, cmp_end <= t_rows)
    o_cmp = _dot(p_cmp.astype(BF16), vc_ref[0, 0, 0])

    w0 = pl.multiple_of(jnp.clip(i * PQ - WINDOW, 0, seq - WIN_SPAN), PQ)
    s = _dot_nt(qb, kwb_ref[pl.ds(w0, WIN_SPAN), :])
    wpos = w0 + lax.broadcasted_iota(jnp.int32, (rows, WIN_SPAN), 1)
    p_win = _masked_softmax(s, (wpos <= t_rows) & (wpos > t_rows - WINDOW))
    o_win = _dot(p_win.astype(BF16), vwb_ref[pl.ds(w0, WIN_SPAN), :])

    p_sum = p_cmp[0:PQ]
    for r in range(1, GROUP):
        p_sum = p_sum + p_cmp[r * PQ:(r + 1) * PQ]
    t_lane = i * PQ + lax.broadcasted_iota(jnp.int32, (1, PQ), 1)
    sel_t = _select_blocks_t(_importance_t(p_sum.T, n_sel), t_lane)
    pen_t = jnp.concatenate([jnp.where(sel_t, 0.0, NEG), jnp.zeros((P_SEL_PAD - n_sel, PQ), F32)], axis=0)
    pen = pen_t.T.astype(BF16)
    for r in range(GROUP):
        qa_ref[r * PQ:(r + 1) * PQ, HEAD_DIM:] = pen

    m_ref[...] = jnp.full(m_ref.shape, NEG, F32)
    l_ref[...] = jnp.zeros(l_ref.shape, F32)
    acc_ref[...] = jnp.zeros(acc_ref.shape, F32)

    def slc_tile(kt, causal):
        k0 = pl.multiple_of(kt * PK, PK)
        kaug = kaug_ref[pl.ds(k0, PK), :]
        v = vsb_ref[pl.ds(k0, PK), :]
        for h in range(2):
            hs = pl.ds(h * half, half)
            s = _dot_nt(qa_ref[hs, :], kaug)
            if causal:
                kpos = k0 + lax.broadcasted_iota(jnp.int32, (half, PK), 1)
                s = jnp.where(kpos <= t_rows[h * half:(h + 1) * half], s, NEG)
            _flash_update(s, v, m_ref.at[hs], l_ref.at[hs], acc_ref.at[hs])

    n_full = (i * PQ) // PK

    def full_tile(kt, carry):
        slc_tile(kt, False)
        return carry

    lax.fori_loop(0, n_full, full_tile, 0)
    slc_tile(n_full, True)
    o_slc = acc_ref[...] / l_ref[...]

    per_group = GROUP * N_BRANCH
    gate = pltpu.roll(_sigmoid(gl_ref[0]), (LANES - pl.program_id(1) * per_group) % LANES, 1)
    for r in range(GROUP):
        rs = slice(r * PQ, (r + 1) * PQ)
        c0 = r * N_BRANCH
        o = (gate[:, c0:c0 + 1] * o_cmp[rs] + gate[:, c0 + 1:c0 + 2] * o_slc[rs]
             + gate[:, c0 + 2:c0 + 3] * o_win[rs])
        o_ref[0, :, r * HEAD_DIM:(r + 1) * HEAD_DIM] = o.astype(o_ref.dtype)


def nsa_prompt(qkv, cblk):
    b, t, _ = qkv.shape
    qcol = ATTN_W // HEAD_DIM

    def kv_spec(slot):
        return pl.BlockSpec((1, t, HEAD_DIM), lambda bi, g, i: (bi, 0, qcol + slot * N_KV + g))

    n_cmp = cblk.shape[3]
    rows = GROUP * PQ
    return pl.pallas_call(
        functools.partial(_nsa_prompt_body, seq=t),
        grid=(b, N_KV, t // PQ),
        in_specs=[pl.BlockSpec((1, PQ, GROUP * HEAD_DIM), lambda bi, g, i: (bi, i, g)),
                  kv_spec(2), kv_spec(3), kv_spec(4), kv_spec(5),
                  pl.BlockSpec((1, 1, 1, n_cmp, HEAD_DIM), lambda bi, g, i: (bi, 0, g, 0, 0)),
                  pl.BlockSpec((1, 1, 1, n_cmp, HEAD_DIM), lambda bi, g, i: (bi, 1, g, 0, 0)),
                  pl.BlockSpec((1, PQ, LANES), lambda bi, g, i: (bi, i, QKV_W // LANES))],
        out_specs=pl.BlockSpec((1, PQ, GROUP * HEAD_DIM), lambda bi, g, i: (bi, i, g)),
        out_shape=jax.ShapeDtypeStruct((b, t, ATTN_W), BF16),
        scratch_shapes=[pltpu.VMEM((t, 2 * HEAD_DIM), BF16), pltpu.VMEM((t, HEAD_DIM), BF16),
                        pltpu.VMEM((t, HEAD_DIM), BF16), pltpu.VMEM((t, HEAD_DIM), BF16),
                        pltpu.VMEM((rows, 2 * HEAD_DIM), BF16), pltpu.VMEM((rows, 1), F32), pltpu.VMEM((rows, 1), F32),
                        pltpu.VMEM((rows, HEAD_DIM), F32)],
        compiler_params=_params(("arbitrary",) * 3, 40 << 20),
    )(qkv, qkv, qkv, qkv, qkv, cblk, cblk, qkv)


SP = 8
S_KEYS = SP * PAGE_SIZE
S_BLKS = S_KEYS // L_SEL
S_CHUNK = LANES // S_BLKS
S_ROWS = GROUP * S_PAD


def _nsa_sample_body(pt_ref, *refs, t_new, n_steps, n_sel_pad):
    pages = refs[:SP]
    (q_ref, cb_ref, win_ref, gl_ref, o_ref,
     qb_ref, pen_ref, new_ref, ocw_ref, m_ref, l_ref, acc_ref) = refs[SP:]
    i = pl.program_id(1)
    scale = HEAD_DIM ** -0.5
    n_chunks = n_sel_pad // LANES
    n_win = win_ref.shape[1] // (2 * N_KV)
    tok = lax.broadcasted_iota(jnp.int32, (S_PAD, 1), 0)
    t_q = PAST_LEN + tok
    t_rows = jnp.concatenate([t_q] * GROUP, axis=0)
    qcol = ATTN_W

    @pl.when(i == 0)
    def _():
        pad = jnp.zeros((LANES - S_PAD, HEAD_DIM), F32)
        for g in range(N_KV):
            heads = [q_ref[0, :, (g * GROUP + r) * HEAD_DIM:(g * GROUP + r + 1) * HEAD_DIM]
                     for r in range(GROUP)]
            qb_ref[g] = (jnp.concatenate(heads, axis=0) * scale).astype(BF16)
            for slot in range(2, 6):
                c0 = qcol + (slot * N_KV + g) * HEAD_DIM
                new_ref[slot - 2, g] = jnp.concatenate(
                    [q_ref[0, :, c0:c0 + HEAD_DIM], pad], axis=0).astype(BF16)
        m_ref[...] = jnp.full(m_ref.shape, NEG, F32)
        l_ref[...] = jnp.zeros(l_ref.shape, F32)
        acc_ref[...] = jnp.zeros(acc_ref.shape, F32)
        p_sums = []
        for g in range(N_KV):
            qb = qb_ref[g]
            kc = cb_ref[0, 0, g]
            n_cmp = kc.shape[0]
            s = _dot_nt(qb, kc)
            cmp_end = lax.broadcasted_iota(jnp.int32, (S_ROWS, n_cmp), 1) * CMP_STRIDE + (L_CMP - 1)
            p_cmp = _masked_softmax(s, cmp_end <= t_rows)
            ocw_ref[0, g] = _dot(p_cmp.astype(BF16), cb_ref[0, 1, g])
            p_sum = p_cmp[0:S_PAD]
            for r in range(1, GROUP):
                p_sum = p_sum + p_cmp[r * S_PAD:(r + 1) * S_PAD]
            p_sums.append(p_sum)
            kw = win_ref[0, pl.ds(g, n_win, stride=2 * N_KV), :].astype(BF16)
            vw = win_ref[0, pl.ds(N_KV + g, n_win, stride=2 * N_KV), :].astype(BF16)
            s = jnp.concatenate([_dot_nt(qb, kw), _dot_nt(qb, new_ref[2, g])], axis=1)
            lane = lax.broadcasted_iota(jnp.int32, (S_ROWS, n_win + LANES), 1)
            wpos = PAST_LEN - n_win + lane
            mask = (wpos <= t_rows) & (wpos > t_rows - WINDOW) & (lane < n_win + t_new)
            p_win = _masked_softmax(s, mask).astype(BF16)
            ocw_ref[1, g] = _dot(p_win[:, 0:n_win], vw) + _dot(p_win[:, n_win:], new_ref[3, g])
        imp = _importance(jnp.concatenate(p_sums, axis=0), n_sel_pad)
        sel = _select_blocks(imp, jnp.concatenate([t_q] * N_KV, axis=0))
        pen = jnp.where(sel, 0.0, NEG).astype(BF16)
        for g in range(N_KV):
            for ch in range(n_chunks):
                blk = pen[g * S_PAD:(g + 1) * S_PAD, ch * LANES:(ch + 1) * LANES]
                pen_ref[ch, g] = jnp.concatenate([blk] * GROUP, axis=0)

    key = lax.broadcasted_iota(jnp.int32, (S_KEYS, LANES), 0)
    blk = lax.broadcasted_iota(jnp.int32, (S_KEYS, LANES), 1)
    ind = jnp.where((i % S_CHUNK) * S_BLKS + key // L_SEL == blk, 1.0, 0.0).astype(BF16)
    for g in range(N_KV):
        k = jnp.concatenate(
            [p[0, pl.ds(g, PAGE_SIZE, stride=2 * N_KV), :] for p in pages], axis=0)
        v = jnp.concatenate(
            [p[0, pl.ds(N_KV + g, PAGE_SIZE, stride=2 * N_KV), :] for p in pages], axis=0)
        kaug = jnp.concatenate([k.astype(BF16), ind], axis=1)
        qa = jnp.concatenate([qb_ref[g], pen_ref[i // S_CHUNK, g]], axis=1)
        _flash_update(_dot_nt(qa, kaug), v.astype(BF16), m_ref.at[g], l_ref.at[g], acc_ref.at[g])

    @pl.when(i == n_steps - 1)
    def _():
        gate = _sigmoid(gl_ref[0])
        lane = lax.broadcasted_iota(jnp.int32, (S_ROWS, LANES), 1)
        cur_chunk, cur_lane = (PAST_LEN // L_SEL) // LANES, (PAST_LEN // L_SEL) % LANES
        for g in range(N_KV):
            pen_cur = pen_ref[cur_chunk, g][:, cur_lane:cur_lane + 1].astype(F32)
            s = _dot_nt(qb_ref[g], new_ref[0, g]) + pen_cur
            s = jnp.where((PAST_LEN + lane <= t_rows) & (lane < t_new), s, NEG)
            _flash_update(s, new_ref[1, g], m_ref.at[g], l_ref.at[g], acc_ref.at[g])
            o_slc = acc_ref[g] / l_ref[g]
            for r in range(GROUP):
                rs = slice(r * S_PAD, (r + 1) * S_PAD)
                c0 = (g * GROUP + r) * N_BRANCH
                o = (gate[:, c0:c0 + 1] * ocw_ref[0, g][rs] + gate[:, c0 + 1:c0 + 2] * o_slc[rs]
                     + gate[:, c0 + 2:c0 + 3] * ocw_ref[1, g][rs])
                h = g * GROUP + r
                o_ref[0, :, h * HEAD_DIM:(h + 1) * HEAD_DIM] = o.astype(o_ref.dtype)


def nsa_sample(qkv, t_new, cblk, cache_slc, cache_win, page_table, gl):
    b = qkv.shape[0]
    n_pages = page_table.shape[1]
    n_steps = n_pages // SP
    n_cmp = cblk.shape[3]
    n_win = cache_win.shape[1] // (2 * N_KV)
    n_sel = -(-(PAST_LEN + t_new) // L_SEL)
    n_sel_pad = -(-n_sel // LANES) * LANES
    assert n_steps * S_BLKS <= n_sel_pad and n_steps * SP == n_pages

    def page_spec(c):
        return pl.BlockSpec((1, PAGE_SIZE * 2 * N_KV, HEAD_DIM),
                            lambda bi, i, pt: (pt[bi, i * SP + c], 0, 0))

    grid_spec = pltpu.PrefetchScalarGridSpec(
        num_scalar_prefetch=1, grid=(b, n_steps),
        in_specs=[page_spec(c) for c in range(SP)] + [
            pl.BlockSpec((1, S_PAD, QKV_W), lambda bi, i, pt: (bi, 0, 0)),
            pl.BlockSpec((1, 2, N_KV, n_cmp, HEAD_DIM), lambda bi, i, pt: (bi, 0, 0, 0, 0)),
            pl.BlockSpec((1, n_win * 2 * N_KV, HEAD_DIM), lambda bi, i, pt: (bi, 0, 0)),
            pl.BlockSpec((1, S_PAD, LANES), lambda bi, i, pt: (bi, 0, 0))],
        out_specs=pl.BlockSpec((1, S_PAD, ATTN_W), lambda bi, i, pt: (bi, 0, 0)),
        scratch_shapes=[pltpu.VMEM((N_KV, S_ROWS, HEAD_DIM), BF16),
                        pltpu.VMEM((n_sel_pad // LANES, N_KV, S_ROWS, LANES), BF16),
                        pltpu.VMEM((4, N_KV, LANES, HEAD_DIM), BF16),
                        pltpu.VMEM((2, N_KV, S_ROWS, HEAD_DIM), F32),
                        pltpu.VMEM((N_KV, S_ROWS, 1), F32), pltpu.VMEM((N_KV, S_ROWS, 1), F32),
                        pltpu.VMEM((N_KV, S_ROWS, HEAD_DIM), F32)])
    return pl.pallas_call(
        functools.partial(_nsa_sample_body, t_new=t_new, n_steps=n_steps, n_sel_pad=n_sel_pad),
        grid_spec=grid_spec,
        out_shape=jax.ShapeDtypeStruct((b, S_PAD, ATTN_W), F32),
        compiler_params=_params(("arbitrary", "arbitrary"), 40 << 20),
    )(page_table, *([cache_slc] * SP), qkv, cblk, cache_win, gl)


def _pad_rows(x, front, total):
    return jnp.pad(x, ((0, 0), (front, total - front - x.shape[1]), (0, 0)))


def kernel(x_prompt, x_sample, cache_cmp_kv, cache_slc_kv, cache_win_kv, state_conf_conv,
           state_short_conv, page_table, p_prompt, p_sample, g_mix0, w_in0, conv_w0, conv_b0,
           ln_g0, ln_b0, cmp_pe, cmp_w1, cmp_b1, cmp_w2, w_out0, g_mix1, w_in1, sconv_w1, w_out1,
           g_ffn, w_ffn_gate, w_ffn_up, w_ffn_down, g_ple, w_ple_gate, w_ple_proj, g_final):
    bp, seq, d = x_prompt.shape
    bs, t_new, _ = x_sample.shape
    mp, ms = bp * seq, bs * t_new
    d_ff = w_ffn_gate.shape[-1]
    n_pool = cache_cmp_kv.shape[1]
    tm, tm_big = 1024, 2048

    xp = x_prompt.reshape(mp, d)
    xs = x_sample.reshape(ms, d)
    rope_p = rope_tables(jnp.arange(seq, dtype=jnp.int32))
    rope_s = tuple(jnp.tile(tb, (bs, 1)) for tb in rope_tables(PAST_LEN + jnp.arange(t_new, dtype=jnp.int32)))
    outs = {}
    w_down_bf16 = w_ffn_down.astype(BF16)

    def residual(groups, x, tm_, tn_, emit=True):
        res_p, res_s = fused_mm(groups, _ep_residual, [F32], d, tm_, tn_, tilex=[x], m_outer=True,
                                emit_bf16=emit, emit_ssq=emit)
        return tuple(zip(res_p, res_s))

    def normed(stream, gain, ws):
        (_, _), (b_p, b_s), ssq = stream
        return Lhs(b_p, b_s, ws, gain=gain, ssq=ssq)

    def ffn_ple(stream, i, last):
        (gp,), (gs,) = fused_mm([normed(stream, g_ffn[i], [W(w_ffn_gate, i), W(w_ffn_up, i)])],
                                _ep_swiglu, [BF16], d_ff, tm_big, 256, m_outer=True)
        stream = residual([Lhs(gp, gs, [W(w_down_bf16, i)])], stream[0], tm, 256)
        pp = p_prompt[i].reshape(mp, -1).astype(BF16)
        ps = p_sample[i].reshape(ms, -1).astype(BF16)
        res_p, res_s = fused_mm([normed(stream, g_ple[i], [W(w_ple_gate, i)]), Lhs(pp, ps, [W(w_ple_proj, i)])],
                                _ep_ple, [F32], d, tm_big, 256, tilex=[stream[0]], m_outer=True,
                                emit_bf16=not last, emit_ssq=not last)
        return tuple(zip(res_p, res_s))

    e = 0
    hp, hs = rmsnorm(xp, g_mix0[e], BF16), rmsnorm(xs, g_mix0[e], BF16)
    (up,), (us,) = fused_mm([Lhs(hp, hs, [W(w_in0, e), W(w_in0, e, col_off=CONF_CH // 256)])], _ep_glu,
                            [F32], CONF_CH, tm_big, 256, m_outer=True)
    (qkv_p,), (qkv_s,) = fused_mm([Lhs(hp, hs, [W(w_in0, e, col_off=2 * CONF_CH // KV_W)])], _ep_rope,
                                  [F32], QKVG_W, tm, KV_W, rowx=list(zip(rope_p, rope_s)))

    up3, us3 = up.reshape(bp, seq, CONF_CH), us.reshape(bs, t_new, CONF_CH)
    conf_w = (conv_w0[e], conv_b0[e], ln_g0[e], ln_b0[e])
    a_p = conf_conv(up3, jnp.zeros((bp, HALO, CONF_CH), F32), *conf_w, tt=128, out_dtype=BF16)
    st = state_conf_conv[e]
    a_s = conf_conv(_pad_rows(us3, 0, S_PAD), _pad_rows(st, HALO - st.shape[1], HALO), *conf_w,
                    tt=S_PAD, out_dtype=F32)
    outs["conf_p"] = up3[:, seq - (CONF_W - 1):]
    outs["conf_s"] = jnp.concatenate([st, us3], axis=1)[:, t_new:]

    qkv_p3, qkv_s3 = qkv_p.reshape(bp, seq, QKVG_W), qkv_s.reshape(bs, t_new, QKVG_W)
    kv_shape = lambda x: x.reshape(x.shape[0], x.shape[1], 2, N_KV, HEAD_DIM)
    for name, slot in (("cmp", 0), ("slc", 2), ("win", 4)):
        c0 = ATTN_W + slot * KV_W
        outs[name + "_p"] = kv_shape(qkv_p3[:, :, c0:c0 + 2 * KV_W])
        outs[name + "_s"] = kv_shape(qkv_s3[:, :, c0:c0 + 2 * KV_W])
    outs["win_p"] = outs["win_p"][:, seq - min(WINDOW, seq):]
    outs["win_s"] = jnp.concatenate([cache_win_kv[e], outs["win_s"]], axis=1)[:, t_new:]

    pages_p = seq // PAGE_SIZE
    ident = jnp.arange(bp * pages_p, dtype=jnp.int32).reshape(bp, pages_p)
    w1cat = cmp_w1_cat(cmp_w1[e])
    proj_p = cmp_project(qkv_p.reshape(bp * pages_p, PAGE_SIZE, QKVG_W), False, ATTN_W // PAGE_ROW_W,
                         ident, w1cat)
    row_view = lambda c: c.reshape(-1, c.shape[-4] * 2 * N_KV, HEAD_DIM)
    pool_pages = page_table + e * n_pool
    proj_s = cmp_project(row_view(cache_cmp_kv), True, 0, pool_pages, w1cat)
    cblk_p = cmp_finish(proj_p, cmp_pe[e], w1cat, cmp_b1[e], cmp_w2[e])
    cblk_s = cmp_finish(proj_s, cmp_pe[e], w1cat, cmp_b1[e], cmp_w2[e])

    attn_p = nsa_prompt(qkv_p3, cblk_p)
    gl_s3 = jnp.pad(qkv_s3[:, :, QKV_W:QKV_W + LANES], ((0, 0), (0, S_PAD - t_new), (0, 0)))
    attn_s = nsa_sample(_pad_rows(qkv_s3[:, :, :QKV_W], 0, S_PAD), t_new, cblk_s, row_view(cache_slc_kv),
                        row_view(cache_win_kv[e]), pool_pages, gl_s3)
    a_p2, a_s2 = a_p.reshape(mp, CONF_CH), a_s[:, :t_new].reshape(ms, CONF_CH).astype(BF16)
    at_p2, at_s2 = attn_p.reshape(mp, ATTN_W), attn_s[:, :t_new].reshape(ms, ATTN_W).astype(BF16)
    stream = residual(
        [Lhs(a_p2, a_s2, [W(w_out0, e, k_rows=CONF_CH, row_blk=0)]),
         Lhs(at_p2, at_s2, [W(w_out0, e, k_rows=ATTN_W, row_blk=CONF_CH // ATTN_W)])],
        (xp, xs), tm_big, 256)
    stream = ffn_ple(stream, 0, last=False)

    o = 0
    (bg_p, cv_p), (bg_s, cv_s) = fused_mm(
        [normed(stream, g_mix1[o], [W(w_in1, o), W(w_in1, o, col_off=SC_CH // 256),
                                    W(w_in1, o, col_off=2 * SC_CH // 256)])],
        _ep_shortconv, [F32, F32], SC_CH, 512, 256)
    cv_p3, cv_s3 = cv_p.reshape(bp, seq, SC_CH), cv_s.reshape(bs, t_new, SC_CH)
    y_p = short_conv(cv_p3, bg_p.reshape(bp, seq, SC_CH), jnp.zeros((bp, SUBLANES, SC_CH), F32),
                     sconv_w1[o], tt=256, out_dtype=BF16)
    st = state_short_conv[o]
    y_s = short_conv(_pad_rows(cv_s3, 0, S_PAD), _pad_rows(bg_s.reshape(bs, t_new, SC_CH), 0, S_PAD),
                     _pad_rows(st, SUBLANES - st.shape[1], SUBLANES), sconv_w1[o], tt=S_PAD,
                     out_dtype=F32)
    outs["sc_p"] = cv_p3[:, seq - (SC_W - 1):]
    outs["sc_s"] = jnp.concatenate([st, cv_s3], axis=1)[:, t_new:]
    stream = residual(
        [Lhs(y_p.reshape(mp, SC_CH), y_s[:, :t_new].reshape(ms, SC_CH).astype(BF16), [W(w_out1, o)])],
        stream[0], tm_big, 256)
    (xp, xs), = ffn_ple(stream, 1, last=True)

    y_p = rmsnorm(xp, g_final, F32).reshape(bp, seq, d)
    y_s = rmsnorm(xs, g_final, F32).reshape(bs, t_new, d)
    st1 = lambda x: x[None]
    return (y_p, y_s, st1(outs["cmp_p"]), st1(outs["cmp_s"]), st1(outs["slc_p"]), st1(outs["slc_s"]),
            st1(outs["win_p"]), st1(outs["win_s"]), st1(outs["conf_p"]), st1(outs["conf_s"]),
            st1(outs["sc_p"]), st1(outs["sc_s"]))
```

```python
import functools

import jax
import jax.numpy as jnp
from jax import lax
from jax.experimental import pallas as pl
from jax.experimental.pallas import tpu as pltpu

F32 = jnp.float32
BF16 = jnp.bfloat16

V7X_VMEM_BYTES = 64 * 1024 * 1024
LANES = 128
SUBLANES = 8

D_MODEL = 4096
PAST_LEN = 16384
PAGE_SIZE = 128
N_HEADS = 16
HEAD_DIM = 128
N_KV = 4
GROUP = N_HEADS // N_KV
ATTN_W = N_HEADS * HEAD_DIM
KV_W = N_KV * HEAD_DIM
ROPE_DIM = HEAD_DIM // 4
ROPE_THETA = 500000.0
L_CMP = 32
CMP_STRIDE = 16
L_SEL = 64
N_SEL = 16
WINDOW = 512
N_BRANCH = 3
CONF_CH = D_MODEL // 2
CONF_W = 31
SC_CH = D_MODEL
SC_W = 3
RMS_EPS = 1e-6
LN_EPS = 1e-5
NEG = -1e30
FORCE = 1e9
QKV_W = ATTN_W + 6 * KV_W
QKVG_W = QKV_W + KV_W
SUB_PER_PAGE = PAGE_SIZE // CMP_STRIDE
SUB_FLAT = CMP_STRIDE * HEAD_DIM
PAGE_ROW_W = 2 * KV_W
HALO = 32
S_PAD = 8


def _vmem_limit(n_bytes):
    return int(min(V7X_VMEM_BYTES - (6 << 20), max(n_bytes, 16 << 20)))


def _params(sem, vmem):
    return pltpu.CompilerParams(dimension_semantics=sem, vmem_limit_bytes=_vmem_limit(vmem))


def _dot(a, b):
    return jnp.dot(a, b, preferred_element_type=F32)


def _dot_nt(a, b):
    return lax.dot_general(a, b, (((1,), (1,)), ((), ())), preferred_element_type=F32)


def _sigmoid(x):
    return jax.nn.sigmoid(x)


def _rmsnorm_body(x_ref, g_ref, o_ref):
    x = x_ref[...]
    y = x * lax.rsqrt(jnp.mean(x * x, axis=-1, keepdims=True) + RMS_EPS)
    o_ref[...] = (y * g_ref[...]).astype(o_ref.dtype)


def rmsnorm(x, g, out_dtype, rows=256):
    m, d = x.shape
    tr = min(rows, m)
    return pl.pallas_call(
        _rmsnorm_body,
        grid=(m // tr,),
        in_specs=[pl.BlockSpec((tr, d), lambda i: (i, 0)),
                  pl.BlockSpec((1, d), lambda i: (0, 0))],
        out_specs=pl.BlockSpec((tr, d), lambda i: (i, 0)),
        out_shape=jax.ShapeDtypeStruct((m, d), out_dtype),
        compiler_params=_params(("arbitrary",), 6 * tr * d * 4),
    )(x, g.reshape(1, d))


class W:
    def __init__(self, arr, layer=0, k_rows=None, row_blk=0, col_off=0, transposed=False):
        self.arr = arr
        self.layer = layer
        self.k_rows = arr.shape[2 if transposed else 1] if k_rows is None else k_rows
        self.row_blk = row_blk
        self.col_off = col_off
        self.transposed = transposed


class Lhs:
    def __init__(self, a_p, a_s, ws, gain=None, ssq=None):
        self.a_p, self.a_s, self.ws, self.gain, self.ssq = a_p, a_s, ws, gain, ssq


def _fold_lanes(x):
    out = x[:, 0:LANES]
    for c in range(1, x.shape[1] // LANES):
        out = out + x[:, c * LANES:(c + 1) * LANES]
    return out


def _mm_body(*refs, meta, n_rx, n_tx, n_main, emit_bf16, emit_ssq, epilogue, m_axis):
    it = iter(refs)
    groups = []
    for w_transposed, has_gain, has_ssq in meta:
        a = (next(it), next(it))
        gain = next(it) if has_gain else None
        ssq = (next(it), next(it)) if has_ssq else None
        groups.append((a, gain, ssq, [(next(it), t) for t in w_transposed]))
    rx = [[next(it) for _ in range(n_rx)] for _ in range(2)]
    tx = [[next(it) for _ in range(n_tx)] for _ in range(2)]
    n_out = n_main + emit_bf16 + emit_ssq
    outs = [[next(it) for _ in range(n_out)] for _ in range(2)]
    i = pl.program_id(m_axis)
    j = pl.program_id(1 - m_axis)
    wb = []
    for _, gain, _, ws in groups:
        if gain is None:
            wb.append([(w[...].astype(BF16), t) for w, t in ws])
        else:
            wb.append([((w[...] * gain[...]).astype(BF16), t) for w, t in ws])

    def run(which):
        dots = []
        for (a, _, ssq, _), wbs in zip(groups, wb):
            lhs = a[which][...]
            scale = None
            if ssq is not None:
                total = jnp.sum(ssq[which][...], axis=-1, keepdims=True)
                scale = lax.rsqrt(total / lhs.shape[1] + RMS_EPS)
            for w, transposed in wbs:
                d = _dot_nt(lhs, w) if transposed else _dot(lhs, w)
                dots.append(d if scale is None else d * scale)
        res = list(epilogue(dots, [r[...] for r in rx[which]], [t[...] for t in tx[which]], j))
        if emit_bf16:
            res.append(res[0])
        for o, r in zip(outs[which], res):
            o[...] = r.astype(o.dtype)
        if emit_ssq:
            part = _fold_lanes(res[0] * res[0])
            acc = outs[which][-1]
            if m_axis == 1:
                acc[...] = part
            else:
                @pl.when(j == 0)
                def _():
                    acc[...] = part

                @pl.when(j > 0)
                def _():
                    acc[...] += part

    run(0)

    @pl.when(i == 0)
    def _():
        run(1)

    if m_axis == 0:
        @pl.when(i == 1)
        def _():
            for o in outs[1]:
                o[...] = jnp.zeros(o.shape, o.dtype)


def fused_mm(groups, epilogue, out_dtypes, n_cols, tm, tn, rowx=(), tilex=(), m_outer=False,
             emit_bf16=False, emit_ssq=False):
    mp = groups[0].a_p.shape[0]
    ms = groups[0].a_s.shape[0]
    nj, ni = n_cols // tn, mp // tm
    assert nj * tn == n_cols and ni * tm == mp
    m_axis = 0 if m_outer else 1
    spare = 1 if m_outer else 0

    def spec(shape, fn):
        return pl.BlockSpec(shape, lambda *g: fn(g[m_axis], g[1 - m_axis]))

    args, in_specs, meta = [], [], []
    vmem = 0
    for g in groups:
        k = g.a_p.shape[1]
        args += [g.a_p, g.a_s]
        in_specs += [spec((tm, k), lambda i, j: (i, 0)), spec((ms, k), lambda i, j: (0, 0))]
        vmem += 2 * (tm + ms) * k * g.a_p.dtype.itemsize
        if g.gain is not None:
            args.append(g.gain.reshape(k, 1))
            in_specs.append(spec((k, 1), lambda i, j: (0, 0)))
            vmem += 2 * k * LANES * 4
        if g.ssq is not None:
            args += list(g.ssq)
            in_specs += [spec((tm, g.ssq[0].shape[1]), lambda i, j: (i, 0)),
                         spec(g.ssq[1].shape, lambda i, j: (0, 0))]
            vmem += 2 * (tm + ms) * g.ssq[0].shape[1] * 4
        for w in g.ws:
            assert w.k_rows == k and not (w.transposed and g.gain is not None)
            args.append(w.arr)
            if w.transposed:
                in_specs.append(spec((None, tn, k), functools.partial(
                    lambda i, j, la, rb, co: (la, co + j, rb), la=w.layer, rb=w.row_blk, co=w.col_off)))
            else:
                in_specs.append(spec((None, k, tn), functools.partial(
                    lambda i, j, la, rb, co: (la, rb, co + j), la=w.layer, rb=w.row_blk, co=w.col_off)))
            vmem += k * tn * (2 * w.arr.dtype.itemsize + 2)
        meta.append((tuple(w.transposed for w in g.ws), g.gain is not None, g.ssq is not None))
    for which in (0, 1):
        for tab_p, tab_s in rowx:
            if which == 0:
                per = tab_p.shape[0] // tm
                args.append(tab_p)
                in_specs.append(spec((tm, tab_p.shape[1]),
                                     functools.partial(lambda i, j, per: (i % per, 0), per=per)))
                vmem += 2 * tm * tab_p.shape[1] * 4
            else:
                args.append(tab_s)
                in_specs.append(spec(tab_s.shape, lambda i, j: (0, 0)))
    for which in (0, 1):
        for t_p, t_s in tilex:
            if which == 0:
                args.append(t_p)
                in_specs.append(spec((tm, tn), lambda i, j: (i, j)))
            else:
                args.append(t_s)
                in_specs.append(spec((ms, tn), lambda i, j: (0, j)))
    kinds = [(dt, tn, nj) for dt in out_dtypes]
    if emit_bf16:
        kinds.append((BF16, tn, nj))
    if emit_ssq:
        kinds.append((F32, LANES, 1 if m_outer else nj))
    out_shape, out_specs = [], []
    for which in (0, 1):
        for dt, width, nblk in kinds:
            col = (lambda i, j: j) if nblk > 1 else (lambda i, j: 0)
            if which == 0:
                out_shape.append(jax.ShapeDtypeStruct((mp, nblk * width), dt))
                out_specs.append(spec((tm, width), functools.partial(lambda i, j, col: (i, col(i, j)), col=col)))
            else:
                out_shape.append(jax.ShapeDtypeStruct((ms, (nblk + spare) * width), dt))
                out_specs.append(spec((ms, width), functools.partial(
                    lambda i, j, col, nblk: (0, jnp.where(i == 0, col(i, j), nblk) if m_outer else col(i, j)),
                    col=col, nblk=nblk)))
    n_dots = sum(len(g.ws) for g in groups)
    vmem += (2 * (len(tilex) + len(kinds)) + n_dots + 2) * tm * tn * 4
    body = functools.partial(
        _mm_body, meta=tuple(meta), n_rx=len(rowx), n_tx=len(tilex), n_main=len(out_dtypes),
        emit_bf16=emit_bf16, emit_ssq=emit_ssq, epilogue=epilogue, m_axis=m_axis)
    res = pl.pallas_call(
        body, grid=(ni, nj) if m_outer else (nj, ni), in_specs=in_specs, out_specs=out_specs,
        out_shape=out_shape, compiler_params=_params(("arbitrary", "arbitrary"), vmem + (6 << 20)),
    )(*args)
    n = len(kinds)
    sample = [r[:, :nblk * width] for r, (_, width, nblk) in zip(res[n:], kinds)]
    return res[:n], sample


def _ep_glu(dots, rx, tx, j):
    return [dots[0] * _sigmoid(dots[1])]


def _ep_swiglu(dots, rx, tx, j):
    return [jax.nn.silu(dots[0]) * dots[1]]


def _ep_residual(dots, rx, tx, j):
    return [tx[0] + sum(dots[1:], dots[0])]


def _ep_ple(dots, rx, tx, j):
    return [tx[0] + _sigmoid(dots[0]) * dots[1]]


def _ep_shortconv(dots, rx, tx, j):
    return [dots[0], dots[1] * dots[2]]


def _ep_rope(dots, rx, tx, j):
    z = dots[0]
    cos, sin_lo, sin_hi = rx
    half = ROPE_DIM // 2
    heads = []
    for h in range(z.shape[1] // HEAD_DIM):
        x = z[:, h * HEAD_DIM:(h + 1) * HEAD_DIM]
        heads.append(x * cos + pltpu.roll(x, half, 1) * sin_hi
                     + pltpu.roll(x, HEAD_DIM - half, 1) * sin_lo)
    roped = jnp.concatenate(heads, axis=1)
    slot = j - ATTN_W // z.shape[1]
    is_v = jnp.logical_and(slot >= 0, slot % 2 == 1)
    lane = lax.broadcasted_iota(jnp.int32, z.shape, 1)
    gates = jnp.where(lane < N_HEADS * N_BRANCH, z, 0.0)
    return [jnp.where(slot == 6, gates, jnp.where(is_v, z, roped))]


def rope_tables(pos):
    half = ROPE_DIM // 2
    inv = ROPE_THETA ** (-2.0 * jnp.arange(half, dtype=F32) / ROPE_DIM)
    ang = pos.astype(F32)[:, None] * inv[None, :]
    cos, sin = jnp.cos(ang), jnp.sin(ang)
    n = pos.shape[0]
    rest = HEAD_DIM - ROPE_DIM
    c = jnp.concatenate([cos, cos, jnp.ones((n, rest), F32)], axis=1)
    s_lo = jnp.concatenate([-sin, jnp.zeros((n, half + rest), F32)], axis=1)
    s_hi = jnp.concatenate([jnp.zeros((n, half), F32), sin, jnp.zeros((n, rest), F32)], axis=1)
    return c, s_lo, s_hi


def _conf_core(xw_ref, cw_ref, cb_ref, g_ref, b_ref, cbuf_ref, o_ref, tt):
    last = CONF_W - 1
    for c in range(CONF_CH // LANES):
        cs = slice(c * LANES, (c + 1) * LANES)
        acc = None
        for r in range(SUBLANES):
            y = None
            for a in range((last - r) // SUBLANES + 1):
                lo = HALO - SUBLANES * (a + 1)
                term = xw_ref[lo:lo + tt + SUBLANES, cs] * cw_ref[last - SUBLANES * a - r:last - SUBLANES * a - r + 1, cs]
                y = term if y is None else y + term
            part = y[SUBLANES - r:SUBLANES - r + tt]
            acc = part if acc is None else acc + part
        cbuf_ref[:, cs] = acc
    c = cbuf_ref[...] + cb_ref[...]
    mu = jnp.mean(c, axis=-1, keepdims=True)
    var = jnp.mean(jnp.square(c - mu), axis=-1, keepdims=True)
    y = (c - mu) * lax.rsqrt(var + LN_EPS) * g_ref[...] + b_ref[...]
    o_ref[0] = jax.nn.silu(y).astype(o_ref.dtype)


def _conf_body(prev_ref, halo_ref, x_ref, cw_ref, cb_ref, g_ref, b_ref, o_ref, xw_ref, cbuf_ref, *, tt):
    first = pl.program_id(1) == 0
    xw_ref[0:HALO] = jnp.where(first, prev_ref[0], halo_ref[0])
    xw_ref[HALO:HALO + tt] = x_ref[0]
    _conf_core(xw_ref, cw_ref, cb_ref, g_ref, b_ref, cbuf_ref, o_ref, tt)


def conf_conv(u, prev, cw, cb, ln_g, ln_b, tt, out_dtype):
    b, t, c = u.shape
    if tt < HALO:
        assert t == tt
        halo_spec = pl.BlockSpec((1, HALO, c), lambda bi, i: (bi, 0, 0))
        halo_arr = prev
    else:
        hb = tt // HALO
        halo_spec = pl.BlockSpec((1, HALO, c), lambda bi, i: (bi, jnp.maximum(i * hb - 1, 0), 0))
        halo_arr = u
    cwp = jnp.pad(cw, ((0, HALO - cw.shape[0]), (0, 0)))
    row = lambda v: v.reshape(1, c)
    const = lambda bi, i: (0, 0)
    return pl.pallas_call(
        functools.partial(_conf_body, tt=tt),
        grid=(b, t // tt),
        in_specs=[pl.BlockSpec((1, HALO, c), lambda bi, i: (bi, 0, 0)), halo_spec,
                  pl.BlockSpec((1, tt, c), lambda bi, i: (bi, i, 0)),
                  pl.BlockSpec((HALO, c), const), pl.BlockSpec((1, c), const),
                  pl.BlockSpec((1, c), const), pl.BlockSpec((1, c), const)],
        out_specs=pl.BlockSpec((1, tt, c), lambda bi, i: (bi, i, 0)),
        out_shape=jax.ShapeDtypeStruct((b, t, c), out_dtype),
        scratch_shapes=[pltpu.VMEM((tt + HALO, c), F32), pltpu.VMEM((tt, c), F32)],
        compiler_params=_params(("arbitrary", "arbitrary"), 10 * (tt + HALO) * c * 4),
    )(prev, halo_arr, u, cwp, row(cb), row(ln_g), row(ln_b))


def _short_body(prev_ref, halo_ref, cv_ref, bg_ref, w_ref, o_ref, xw_ref, *, tt):
    first = pl.program_id(1) == 0
    xw_ref[0:SUBLANES] = jnp.where(first, prev_ref[0], halo_ref[0])
    xw_ref[SUBLANES:SUBLANES + tt] = cv_ref[0]
    base = SUBLANES - (SC_W - 1)
    conv = xw_ref[pl.ds(base, tt), :] * w_ref[0:1, :]
    for k in range(1, SC_W):
        conv = conv + xw_ref[pl.ds(base + k, tt), :] * w_ref[k:k + 1, :]
    o_ref[0] = (bg_ref[0] * conv).astype(o_ref.dtype)


def short_conv(cv, bg, prev, w, tt, out_dtype):
    b, t, c = cv.shape
    hb = tt // SUBLANES
    wp = jnp.pad(w, ((0, SUBLANES - w.shape[0]), (0, 0)))
    return pl.pallas_call(
        functools.partial(_short_body, tt=tt),
        grid=(b, t // tt),
        in_specs=[pl.BlockSpec((1, SUBLANES, c), lambda bi, i: (bi, 0, 0)),
                  pl.BlockSpec((1, SUBLANES, c), lambda bi, i: (bi, jnp.maximum(i * hb - 1, 0), 0)),
                  pl.BlockSpec((1, tt, c), lambda bi, i: (bi, i, 0)),
                  pl.BlockSpec((1, tt, c), lambda bi, i: (bi, i, 0)),
                  pl.BlockSpec((SUBLANES, c), lambda bi, i: (0, 0))],
        out_specs=pl.BlockSpec((1, tt, c), lambda bi, i: (bi, i, 0)),
        out_shape=jax.ShapeDtypeStruct((b, t, c), out_dtype),
        scratch_shapes=[pltpu.VMEM((tt + SUBLANES, c), F32)],
        compiler_params=_params(("arbitrary", "arbitrary"), 10 * (tt + SUBLANES) * c * 4),
    )(prev, cv, cv, bg, wp)


CMP_PAGES = 8
SUB_PITCH = 24


def _cmp_proj_body(pt_ref, *refs, row_major):
    pages = refs[:CMP_PAGES]
    w1_ref, p_ref, kbuf_ref, x_ref = refs[CMP_PAGES:]
    rows = CMP_PAGES * SUB_PER_PAGE
    for kv in range(2):
        for g in range(N_KV):
            kvg = kv * N_KV + g
            for c in range(CMP_PAGES):
                if row_major:
                    tok = pages[c][0, pl.ds(kvg, PAGE_SIZE, stride=2 * N_KV), :]
                else:
                    tok = pages[c][0, :, kvg * HEAD_DIM:(kvg + 1) * HEAD_DIM]
                for m in range(SUB_PER_PAGE):
                    r0 = (c * SUB_PER_PAGE + m) * SUB_PITCH
                    kbuf_ref[g, r0:r0 + CMP_STRIDE, :] = tok[m * CMP_STRIDE:(m + 1) * CMP_STRIDE]
            for s in range(CMP_STRIDE):
                x_ref[g * rows:(g + 1) * rows, s * HEAD_DIM:(s + 1) * HEAD_DIM] = (
                    kbuf_ref[g, pl.ds(s, rows, stride=SUB_PITCH), :])
        p = _dot(x_ref[...].astype(BF16), w1_ref[kv])
        for g in range(N_KV):
            p_ref[0, kv, g] = p[g * rows:(g + 1) * rows]


def cmp_w1_cat(w1):
    w = w1.reshape(2, 2, SUB_FLAT, HEAD_DIM).transpose(0, 2, 1, 3)
    return w.reshape(2, SUB_FLAT, 2 * HEAD_DIM).astype(BF16)


def cmp_project(pages_arr, row_major, col_blk, page_table, w1cat):
    b, n_pages = page_table.shape
    n_sub = n_pages * SUB_PER_PAGE
    steps = n_pages // CMP_PAGES
    assert steps * CMP_PAGES == n_pages
    blk = (1, PAGE_SIZE * 2 * N_KV, HEAD_DIM) if row_major else (1, PAGE_SIZE, PAGE_ROW_W)

    def page_spec(c):
        return pl.BlockSpec(blk, lambda bi, i, pt: (pt[bi, i * CMP_PAGES + c], 0, col_blk))

    rows = CMP_PAGES * SUB_PER_PAGE
    grid_spec = pltpu.PrefetchScalarGridSpec(
        num_scalar_prefetch=1, grid=(b, steps),
        in_specs=[page_spec(c) for c in range(CMP_PAGES)] + [
            pl.BlockSpec((2, SUB_FLAT, 2 * HEAD_DIM), lambda bi, i, pt: (0, 0, 0))],
        out_specs=pl.BlockSpec((1, 2, N_KV, rows, 2 * HEAD_DIM), lambda bi, i, pt: (bi, 0, 0, i, 0)),
        scratch_shapes=[pltpu.VMEM((N_KV, rows * SUB_PITCH, HEAD_DIM), F32),
                        pltpu.VMEM((N_KV * rows, SUB_FLAT), F32)])
    return pl.pallas_call(
        functools.partial(_cmp_proj_body, row_major=row_major), grid_spec=grid_spec,
        out_shape=jax.ShapeDtypeStruct((b, 2, N_KV, n_sub, 2 * HEAD_DIM), F32),
        compiler_params=_params(("arbitrary", "arbitrary"), 32 << 20),
    )(page_table, *([pages_arr] * CMP_PAGES), w1cat)


def _cmp_finish_body(p_ref, pe_ref, w1_ref, b1_ref, w2_ref, o_ref):
    n_sub = p_ref.shape[-2]
    w1 = w1_ref[0]
    pe_term = (_dot(pe_ref[0, 0].astype(BF16), w1)[0:1, 0:HEAD_DIM]
               + _dot(pe_ref[0, 1].astype(BF16), w1)[0:1, HEAD_DIM:])
    p = p_ref[0, 0, 0]
    p1_next = pltpu.roll(p[:, HEAD_DIM:], n_sub - 1, 0)
    pre = p[:, 0:HEAD_DIM] + p1_next + (b1_ref[0] + pe_term)
    o_ref[0, 0, 0] = _dot(jax.nn.gelu(pre).astype(BF16), w2_ref[0].astype(BF16)).astype(o_ref.dtype)


def cmp_finish(p, pe, w1cat, b1, w2):
    b, _, _, n_sub, _ = p.shape
    pe_rows = jnp.broadcast_to(pe.reshape(2, 2, 1, SUB_FLAT), (2, 2, SUBLANES, SUB_FLAT))
    return pl.pallas_call(
        _cmp_finish_body, grid=(b, 2, N_KV),
        in_specs=[pl.BlockSpec((1, 1, 1, n_sub, 2 * HEAD_DIM), lambda bi, kv, g: (bi, kv, g, 0, 0)),
                  pl.BlockSpec((1, 2, SUBLANES, SUB_FLAT), lambda bi, kv, g: (kv, 0, 0, 0)),
                  pl.BlockSpec((1, SUB_FLAT, 2 * HEAD_DIM), lambda bi, kv, g: (kv, 0, 0)),
                  pl.BlockSpec((1, 1, HEAD_DIM), lambda bi, kv, g: (kv, 0, 0)),
                  pl.BlockSpec((1, HEAD_DIM, HEAD_DIM), lambda bi, kv, g: (kv, 0, 0))],
        out_specs=pl.BlockSpec((1, 1, 1, n_sub, HEAD_DIM), lambda bi, kv, g: (bi, kv, g, 0, 0)),
        out_shape=jax.ShapeDtypeStruct((b, 2, N_KV, n_sub, HEAD_DIM), BF16),
        compiler_params=_params(("arbitrary",) * 3, 16 << 20),
    )(p, pe_rows, w1cat, b1.reshape(2, 1, HEAD_DIM), w2)


def _masked_softmax(s, mask):
    s = jnp.where(mask, s, NEG)
    e = jnp.exp(s - jnp.max(s, axis=-1, keepdims=True))
    return jnp.where(mask, e / jnp.sum(e, axis=-1, keepdims=True), 0.0)


def _split3(x):
    x1 = x.astype(BF16)
    r1 = x - x1.astype(F32)
    x2 = r1.astype(BF16)
    x3 = (r1 - x2.astype(F32)).astype(BF16)
    return x1, x2, x3


def _overlap_matrix(n_cmp, n_sel):
    n = lax.broadcasted_iota(jnp.int32, (n_cmp, n_sel), 0) * CMP_STRIDE
    j = lax.broadcasted_iota(jnp.int32, (n_cmp, n_sel), 1) * L_SEL
    return jnp.where(jnp.logical_and(n < j + L_SEL, n + L_CMP > j), 1.0, 0.0).astype(BF16)


def _importance(p_sum, n_sel):
    ov = _overlap_matrix(p_sum.shape[1], n_sel)
    a, b, c = _split3(p_sum)
    return _dot(a, ov) + _dot(b, ov) + _dot(c, ov)


def _importance_t(p_sum_t, n_sel):
    n_cmp = p_sum_t.shape[0]
    j = lax.broadcasted_iota(jnp.int32, (n_sel, n_cmp), 0) * L_SEL
    n = lax.broadcasted_iota(jnp.int32, (n_sel, n_cmp), 1) * CMP_STRIDE
    ov_t = jnp.where(jnp.logical_and(n < j + L_SEL, n + L_CMP > j), 1.0, 0.0).astype(BF16)
    a, b, c = _split3(p_sum_t)
    return _dot(ov_t, a) + _dot(ov_t, b) + _dot(ov_t, c)


def _select_blocks_t(imp_t, t_pos):
    n, r = imp_t.shape
    j = lax.broadcasted_iota(jnp.int32, (n, r), 0)
    cur = t_pos // L_SEL
    forced = (j == 0) | (j == cur) | (j == cur - 1)
    valid = j * L_SEL <= t_pos
    score = jnp.where(forced, FORCE, jnp.where(valid, imp_t, NEG))
    rank = jnp.zeros((n, r), F32)
    for k in range(n):
        row = score[k:k + 1, :]
        ahead = (row > score) | ((row == score) & (j > k))
        rank = rank + jnp.where(ahead, 1.0, 0.0)
    return (rank < N_SEL) & (score > 0.5 * NEG)


def _select_blocks(imp, t_pos):
    r, n = imp.shape
    j = lax.broadcasted_iota(jnp.int32, (r, n), 1)
    cur = t_pos // L_SEL
    forced = (j == 0) | (j == cur) | (j == cur - 1)
    valid = j * L_SEL <= t_pos
    score = jnp.where(forced, FORCE, jnp.where(valid, imp, NEG))
    ok = score > 0.5 * NEG
    picked = jnp.zeros((r, n), jnp.bool_)
    left = score
    jf = j.astype(F32)
    for _ in range(N_SEL):
        top = jnp.max(left, axis=-1, keepdims=True)
        first = jnp.min(jnp.where(left == top, jf, float(n)), axis=-1, keepdims=True)
        hit = jf == first
        picked = picked | hit
        left = jnp.where(hit, -jnp.inf, left)
    return picked & ok


def _flash_update(s, v, m_ref, l_ref, acc_ref):
    m_prev = m_ref[...]
    m_new = jnp.maximum(m_prev, jnp.max(s, axis=-1, keepdims=True))
    alpha = jnp.exp(m_prev - m_new)
    p = jnp.exp(s - m_new)
    l_ref[...] = alpha * l_ref[...] + jnp.sum(p, axis=-1, keepdims=True)
    acc_ref[...] = alpha * acc_ref[...] + _dot(p.astype(BF16), v)
    m_ref[...] = m_new


PQ = 128
PK = 512
P_SEL_PAD = 128
WIN_SPAN = WINDOW + PQ


def _nsa_prompt_body(q_ref, ks_ref, vs_ref, kw_ref, vw_ref, kc_ref, vc_ref, gl_ref, o_ref,
                     kaug_ref, vsb_ref, kwb_ref, vwb_ref, qa_ref, m_ref, l_ref, acc_ref, *, seq):
    i = pl.program_id(2)
    rows = GROUP * PQ
    half = rows // 2
    n_sel = seq // L_SEL
    scale = HEAD_DIM ** -0.5

    @pl.when(i == 0)
    def _():
        kaug_ref[:, 0:HEAD_DIM] = ks_ref[0].astype(BF16)
        key = lax.broadcasted_iota(jnp.int32, (seq, P_SEL_PAD), 0)
        blk = lax.broadcasted_iota(jnp.int32, (seq, P_SEL_PAD), 1)
        kaug_ref[:, HEAD_DIM:] = jnp.where(key // L_SEL == blk, 1.0, 0.0).astype(BF16)
        vsb_ref[...] = vs_ref[0].astype(BF16)
        kwb_ref[...] = kw_ref[0].astype(BF16)
        vwb_ref[...] = vw_ref[0].astype(BF16)

    q = q_ref[0]
    qs = jnp.concatenate([q[:, r * HEAD_DIM:(r + 1) * HEAD_DIM] for r in range(GROUP)], axis=0)
    qb = (qs * scale).astype(BF16)
    qa_ref[:, 0:HEAD_DIM] = qb
    t_q = i * PQ + lax.broadcasted_iota(jnp.int32, (PQ, 1), 0)
    t_rows = jnp.concatenate([t_q] * GROUP, axis=0)

    kc = kc_ref[0, 0, 0]
    n_cmp = kc.shape[0]
    s = _dot_nt(qb, kc)
    cmp_end = lax.broadcasted_iota(jnp.int32, (rows, n_cmp), 1) * CMP_STRIDE + (L_CMP - 1)
    p_cmp = _masked_softmax(s, cmp_end <= t_rows)
    o_cmp = _dot(p_cmp.astype(BF16), vc_ref[0, 0, 0])

    w0 = pl.multiple_of(jnp.clip(i * PQ - WINDOW, 0, seq - WIN_SPAN), PQ)
    s = _dot_nt(qb, kwb_ref[pl.ds(w0, WIN_SPAN), :])
    wpos = w0 + lax.broadcasted_iota(jnp.int32, (rows, WIN_SPAN), 1)
    p_win = _masked_softmax(s, (wpos <= t_rows) & (wpos > t_rows - WINDOW))
    o_win = _dot(p_win.astype(BF16), vwb_ref[pl.ds(w0, WIN_SPAN), :])

    p_sum = p_cmp[0:PQ]
    for r in range(1, GROUP):
        p_sum = p_sum + p_cmp[r * PQ:(r + 1) * PQ]
    t_lane = i * PQ + lax.broadcasted_iota(jnp.int32, (1, PQ), 1)
    sel_t = _select_blocks_t(_importance_t(p_sum.T, n_sel), t_lane)
    pen_t = jnp.concatenate([jnp.where(sel_t, 0.0, NEG), jnp.zeros((P_SEL_PAD - n_sel, PQ), F32)], axis=0)
    pen = pen_t.T.astype(BF16)
    for r in range(GROUP):
        qa_ref[r * PQ:(r + 1) * PQ, HEAD_DIM:] = pen

    m_ref[...] = jnp.full(m_ref.shape, NEG, F32)
    l_ref[...] = jnp.zeros(l_ref.shape, F32)
    acc_ref[...] = jnp.zeros(acc_ref.shape, F32)

    def slc_tile(kt, causal):
        k0 = pl.multiple_of(kt * PK, PK)
        kaug = kaug_ref[pl.ds(k0, PK), :]
        v = vsb_ref[pl.ds(k0, PK), :]
        for h in range(2):
            hs = pl.ds(h * half, half)
            s = _dot_nt(qa_ref[hs, :], kaug)
            if causal:
                kpos = k0 + lax.broadcasted_iota(jnp.int32, (half, PK), 1)
                s = jnp.where(kpos <= t_rows[h * half:(h + 1) * half], s, NEG)
            _flash_update(s, v, m_ref.at[hs], l_ref.at[hs], acc_ref.at[hs])

    n_full = (i * PQ) // PK

    def full_tile(kt, carry):
        slc_tile(kt, False)
        return carry

    lax.fori_loop(0, n_full, full_tile, 0)
    slc_tile(n_full, True)
    o_slc = acc_ref[...] / l_ref[...]

    per_group = GROUP * N_BRANCH
    gate = pltpu.roll(_sigmoid(gl_ref[0]), (LANES - pl.program_id(1) * per_group) % LANES, 1)
    for r in range(GROUP):
        rs = slice(r * PQ, (r + 1) * PQ)
        c0 = r * N_BRANCH
        o = (gate[:, c0:c0 + 1] * o_cmp[rs] + gate[:, c0 + 1:c0 + 2] * o_slc[rs]
             + gate[:, c0 + 2:c0 + 3] * o_win[rs])
        o_ref[0, :, r * HEAD_DIM:(r + 1) * HEAD_DIM] = o.astype(o_ref.dtype)


def nsa_prompt(qkv, cblk):
    b, t, _ = qkv.shape
    qcol = ATTN_W // HEAD_DIM

    def kv_spec(slot):
        return pl.BlockSpec((1, t, HEAD_DIM), lambda bi, g, i: (bi, 0, qcol + slot * N_KV + g))

    n_cmp = cblk.shape[3]
    rows = GROUP * PQ
    return pl.pallas_call(
        functools.partial(_nsa_prompt_body, seq=t),
        grid=(b, N_KV, t // PQ),
        in_specs=[pl.BlockSpec((1, PQ, GROUP * HEAD_DIM), lambda bi, g, i: (bi, i, g)),
                  kv_spec(2), kv_spec(3), kv_spec(4), kv_spec(5),
                  pl.BlockSpec((1, 1, 1, n_cmp, HEAD_DIM), lambda bi, g, i: (bi, 0, g, 0, 0)),
                  pl.BlockSpec((1, 1, 1, n_cmp, HEAD_DIM), lambda bi, g, i: (bi, 1, g, 0, 0)),
                  pl.BlockSpec((1, PQ, LANES), lambda bi, g, i: (bi, i, QKV_W // LANES))],
        out_specs=pl.BlockSpec((1, PQ, GROUP * HEAD_DIM), lambda bi, g, i: (bi, i, g)),
        out_shape=jax.ShapeDtypeStruct((b, t, ATTN_W), BF16),
        scratch_shapes=[pltpu.VMEM((t, 2 * HEAD_DIM), BF16), pltpu.VMEM((t, HEAD_DIM), BF16),
                        pltpu.VMEM((t, HEAD_DIM), BF16), pltpu.VMEM((t, HEAD_DIM), BF16),
                        pltpu.VMEM((rows, 2 * HEAD_DIM), BF16), pltpu.VMEM((rows, 1), F32), pltpu.VMEM((rows, 1), F32),
                        pltpu.VMEM((rows, HEAD_DIM), F32)],
        compiler_params=_params(("arbitrary",) * 3, 40 << 20),
    )(qkv, qkv, qkv, qkv, qkv, cblk, cblk, qkv)


SP = 8
S_KEYS = SP * PAGE_SIZE
S_BLKS = S_KEYS // L_SEL
S_CHUNK = LANES // S_BLKS
S_ROWS = GROUP * S_PAD


def _nsa_sample_body(pt_ref, *refs, t_new, n_steps, n_sel_pad):
    pages = refs[:SP]
    (q_ref, cb_ref, win_ref, gl_ref, o_ref,
     qb_ref, pen_ref, new_ref, ocw_ref, m_ref, l_ref, acc_ref) = refs[SP:]
    i = pl.program_id(1)
    scale = HEAD_DIM ** -0.5
    n_chunks = n_sel_pad // LANES
    n_win = win_ref.shape[1] // (2 * N_KV)
    tok = lax.broadcasted_iota(jnp.int32, (S_PAD, 1), 0)
    t_q = PAST_LEN + tok
    t_rows = jnp.concatenate([t_q] * GROUP, axis=0)
    qcol = ATTN_W

    @pl.when(i == 0)
    def _():
        pad = jnp.zeros((LANES - S_PAD, HEAD_DIM), F32)
        for g in range(N_KV):
            heads = [q_ref[0, :, (g * GROUP + r) * HEAD_DIM:(g * GROUP + r + 1) * HEAD_DIM]
                     for r in range(GROUP)]
            qb_ref[g] = (jnp.concatenate(heads, axis=0) * scale).astype(BF16)
            for slot in range(2, 6):
                c0 = qcol + (slot * N_KV + g) * HEAD_DIM
                new_ref[slot - 2, g] = jnp.concatenate(
                    [q_ref[0, :, c0:c0 + HEAD_DIM], pad], axis=0).astype(BF16)
        m_ref[...] = jnp.full(m_ref.shape, NEG, F32)
        l_ref[...] = jnp.zeros(l_ref.shape, F32)
        acc_ref[...] = jnp.zeros(acc_ref.shape, F32)
        p_sums = []
        for g in range(N_KV):
            qb = qb_ref[g]
            kc = cb_ref[0, 0, g]
            n_cmp = kc.shape[0]
            s = _dot_nt(qb, kc)
            cmp_end = lax.broadcasted_iota(jnp.int32, (S_ROWS, n_cmp), 1) * CMP_STRIDE + (L_CMP - 1)
            p_cmp = _masked_softmax(s, cmp_end <= t_rows)
            ocw_ref[0, g] = _dot(p_cmp.astype(BF16), cb_ref[0, 1, g])
            p_sum = p_cmp[0:S_PAD]
            for r in range(1, GROUP):
                p_sum = p_sum + p_cmp[r * S_PAD:(r + 1) * S_PAD]
            p_sums.append(p_sum)
            kw = win_ref[0, pl.ds(g, n_win, stride=2 * N_KV), :].astype(BF16)
            vw = win_ref[0, pl.ds(N_KV + g, n_win, stride=2 * N_KV), :].astype(BF16)
            s = jnp.concatenate([_dot_nt(qb, kw), _dot_nt(qb, new_ref[2, g])], axis=1)
            lane = lax.broadcasted_iota(jnp.int32, (S_ROWS, n_win + LANES), 1)
            wpos = PAST_LEN - n_win + lane
            mask = (wpos <= t_rows) & (wpos > t_rows - WINDOW) & (lane < n_win + t_new)
            p_win = _masked_softmax(s, mask).astype(BF16)
            ocw_ref[1, g] = _dot(p_win[:, 0:n_win], vw) + _dot(p_win[:, n_win:], new_ref[3, g])
        imp = _importance(jnp.concatenate(p_sums, axis=0), n_sel_pad)
        sel = _select_blocks(imp, jnp.concatenate([t_q] * N_KV, axis=0))
        pen = jnp.where(sel, 0.0, NEG).astype(BF16)
        for g in range(N_KV):
            for ch in range(n_chunks):
                blk = pen[g * S_PAD:(g + 1) * S_PAD, ch * LANES:(ch + 1) * LANES]
                pen_ref[ch, g] = jnp.concatenate([blk] * GROUP, axis=0)

    key = lax.broadcasted_iota(jnp.int32, (S_KEYS, LANES), 0)
    blk = lax.broadcasted_iota(jnp.int32, (S_KEYS, LANES), 1)
    ind = jnp.where((i % S_CHUNK) * S_BLKS + key // L_SEL == blk, 1.0, 0.0).astype(BF16)
    for g in range(N_KV):
        k = jnp.concatenate(
            [p[0, pl.ds(g, PAGE_SIZE, stride=2 * N_KV), :] for p in pages], axis=0)
        v = jnp.concatenate(
            [p[0, pl.ds(N_KV + g, PAGE_SIZE, stride=2 * N_KV), :] for p in pages], axis=0)
        kaug = jnp.concatenate([k.astype(BF16), ind], axis=1)
        qa = jnp.concatenate([qb_ref[g], pen_ref[i // S_CHUNK, g]], axis=1)
        _flash_update(_dot_nt(qa, kaug), v.astype(BF16), m_ref.at[g], l_ref.at[g], acc_ref.at[g])

    @pl.when(i == n_steps - 1)
    def _():
        gate = _sigmoid(gl_ref[0])
        lane = lax.broadcasted_iota(jnp.int32, (S_ROWS, LANES), 1)
        cur_chunk, cur_lane = (PAST_LEN // L_SEL) // LANES, (PAST_LEN // L_SEL) % LANES
        for g in range(N_KV):
            pen_cur = pen_ref[cur_chunk, g][:, cur_lane:cur_lane + 1].astype(F32)
            s = _dot_nt(qb_ref[g], new_ref[0, g]) + pen_cur
            s = jnp.where((PAST_LEN + lane <= t_rows) & (lane < t_new), s, NEG)
            _flash_update(s, new_ref[1, g], m_ref.at[g], l_ref.at[g], acc_ref.at[g])
            o_slc = acc_ref[g] / l_ref[g]
            for r in range(GROUP):
                rs = slice(r * S_PAD, (r + 1) * S_PAD)
                c0 = (g * GROUP + r) * N_BRANCH
                o = (gate[:, c0:c0 + 1] * ocw_ref[0, g][rs] + gate[:, c0 + 1:c0 + 2] * o_slc[rs]
                     + gate[:, c0 + 2:c0 + 3] * ocw_ref[1, g][rs])
                h = g * GROUP + r
                o_ref[0, :, h * HEAD_DIM:(h + 1) * HEAD_DIM] = o.astype(o_ref.dtype)


def nsa_sample(qkv, t_new, cblk, cache_slc, cache_win, page_table, gl):
    b = qkv.shape[0]
    n_pages = page_table.shape[1]
    n_steps = n_pages // SP
    n_cmp = cblk.shape[3]
    n_win = cache_win.shape[1] // (2 * N_KV)
    n_sel = -(-(PAST_LEN + t_new) // L_SEL)
    n_sel_pad = -(-n_sel // LANES) * LANES
    assert n_steps * S_BLKS <= n_sel_pad and n_steps * SP == n_pages

    def page_spec(c):
        return pl.BlockSpec((1, PAGE_SIZE * 2 * N_KV, HEAD_DIM),
                            lambda bi, i, pt: (pt[bi, i * SP + c], 0, 0))

    grid_spec = pltpu.PrefetchScalarGridSpec(
        num_scalar_prefetch=1, grid=(b, n_steps),
        in_specs=[page_spec(c) for c in range(SP)] + [
            pl.BlockSpec((1, S_PAD, QKV_W), lambda bi, i, pt: (bi, 0, 0)),
            pl.BlockSpec((1, 2, N_KV, n_cmp, HEAD_DIM), lambda bi, i, pt: (bi, 0, 0, 0, 0)),
            pl.BlockSpec((1, n_win * 2 * N_KV, HEAD_DIM), lambda bi, i, pt: (bi, 0, 0)),
            pl.BlockSpec((1, S_PAD, LANES), lambda bi, i, pt: (bi, 0, 0))],
        out_specs=pl.BlockSpec((1, S_PAD, ATTN_W), lambda bi, i, pt: (bi, 0, 0)),
        scratch_shapes=[pltpu.VMEM((N_KV, S_ROWS, HEAD_DIM), BF16),
                        pltpu.VMEM((n_sel_pad // LANES, N_KV, S_ROWS, LANES), BF16),
                        pltpu.VMEM((4, N_KV, LANES, HEAD_DIM), BF16),
                        pltpu.VMEM((2, N_KV, S_ROWS, HEAD_DIM), F32),
                        pltpu.VMEM((N_KV, S_ROWS, 1), F32), pltpu.VMEM((N_KV, S_ROWS, 1), F32),
                        pltpu.VMEM((N_KV, S_ROWS, HEAD_DIM), F32)])
    return pl.pallas_call(
        functools.partial(_nsa_sample_body, t_new=t_new, n_steps=n_steps, n_sel_pad=n_sel_pad),
        grid_spec=grid_spec,
        out_shape=jax.ShapeDtypeStruct((b, S_PAD, ATTN_W), F32),
        compiler_params=_params(("arbitrary", "arbitrary"), 40 << 20),
    )(page_table, *([cache_slc] * SP), qkv, cblk, cache_win, gl)


def _kv_rows_body(x_ref, o_ref):
    tt = x_ref.shape[1]
    for kvg in range(2 * N_KV):
        o_ref[0, pl.ds(kvg, tt, stride=2 * N_KV), :] = x_ref[0, :, kvg * HEAD_DIM:(kvg + 1) * HEAD_DIM]


def kv_rows(qkv, slot, t0, tt):
    b, t, _ = qkv.shape
    n = (t - t0) // tt
    assert n * tt == t - t0 and t0 % tt == 0
    col_blk = (ATTN_W + slot * KV_W) // PAGE_ROW_W
    out = pl.pallas_call(
        _kv_rows_body, grid=(b, n),
        in_specs=[pl.BlockSpec((1, tt, PAGE_ROW_W), lambda bi, i: (bi, t0 // tt + i, col_blk))],
        out_specs=pl.BlockSpec((1, tt * 2 * N_KV, HEAD_DIM), lambda bi, i: (bi, i, 0)),
        out_shape=jax.ShapeDtypeStruct((b, (t - t0) * 2 * N_KV, HEAD_DIM), F32),
        compiler_params=_params(("arbitrary", "arbitrary"), 6 * tt * PAGE_ROW_W * 4),
    )(qkv)
    return out.reshape(b, t - t0, 2, N_KV, HEAD_DIM)


def _pad_rows(x, front, total):
    return jnp.pad(x, ((0, 0), (front, total - front - x.shape[1]), (0, 0)))


def kernel(x_prompt, x_sample, cache_cmp_kv, cache_slc_kv, cache_win_kv, state_conf_conv,
           state_short_conv, page_table, p_prompt, p_sample, g_mix0, w_in0, conv_w0, conv_b0,
           ln_g0, ln_b0, cmp_pe, cmp_w1, cmp_b1, cmp_w2, w_out0, g_mix1, w_in1, sconv_w1, w_out1,
           g_ffn, w_ffn_gate, w_ffn_up, w_ffn_down, g_ple, w_ple_gate, w_ple_proj, g_final):
    bp, seq, d = x_prompt.shape
    bs, t_new, _ = x_sample.shape
    mp, ms = bp * seq, bs * t_new
    d_ff = w_ffn_gate.shape[-1]
    n_pool = cache_cmp_kv.shape[1]
    tm = 1024

    xp = x_prompt.reshape(mp, d)
    xs = x_sample.reshape(ms, d)
    rope_p = rope_tables(jnp.arange(seq, dtype=jnp.int32))
    rope_s = tuple(jnp.tile(tb, (bs, 1)) for tb in rope_tables(PAST_LEN + jnp.arange(t_new, dtype=jnp.int32)))
    outs = {}
    w_down_bf16 = w_ffn_down.astype(BF16)

    def residual(groups, x, tm_, tn_, m_outer):
        res_p, res_s = fused_mm(groups, _ep_residual, [F32], d, tm_, tn_, tilex=[x], m_outer=m_outer,
                                emit_bf16=True, emit_ssq=True)
        return tuple(zip(res_p, res_s))

    def normed(stream, gain, ws):
        (_, _), (b_p, b_s), ssq = stream
        return Lhs(b_p, b_s, ws, gain=gain, ssq=ssq)

    def ffn_ple(stream, i, last):
        (gp,), (gs,) = fused_mm([normed(stream, g_ffn[i], [W(w_ffn_gate, i), W(w_ffn_up, i)])],
                                _ep_swiglu, [BF16], d_ff, tm, 256)
        stream = residual([Lhs(gp, gs, [W(w_down_bf16, i)])], stream[0], 512, 256, m_outer=True)
        pp = p_prompt[i].reshape(mp, -1).astype(BF16)
        ps = p_sample[i].reshape(ms, -1).astype(BF16)
        res_p, res_s = fused_mm([normed(stream, g_ple[i], [W(w_ple_gate, i)]), Lhs(pp, ps, [W(w_ple_proj, i)])],
                                _ep_ple, [F32], d, tm, 512, tilex=[stream[0]], m_outer=True,
                                emit_bf16=not last, emit_ssq=not last)
        return tuple(zip(res_p, res_s))

    e = 0
    hp, hs = rmsnorm(xp, g_mix0[e], BF16), rmsnorm(xs, g_mix0[e], BF16)
    w_in0_t = jnp.swapaxes(w_in0, 1, 2)
    win = lambda col_off: W(w_in0_t, e, col_off=col_off, transposed=True)
    (up,), (us,) = fused_mm([Lhs(hp, hs, [win(0), win(CONF_CH // 256)])], _ep_glu,
                            [F32], CONF_CH, tm, 256)
    (qkv_p,), (qkv_s,) = fused_mm([Lhs(hp, hs, [win(2 * CONF_CH // KV_W)])], _ep_rope,
                                  [F32], QKVG_W, tm, KV_W, rowx=list(zip(rope_p, rope_s)))

    up3, us3 = up.reshape(bp, seq, CONF_CH), us.reshape(bs, t_new, CONF_CH)
    conf_w = (conv_w0[e], conv_b0[e], ln_g0[e], ln_b0[e])
    a_p = conf_conv(up3, jnp.zeros((bp, HALO, CONF_CH), F32), *conf_w, tt=128, out_dtype=BF16)
    st = state_conf_conv[e]
    a_s = conf_conv(_pad_rows(us3, 0, S_PAD), _pad_rows(st, HALO - st.shape[1], HALO), *conf_w,
                    tt=S_PAD, out_dtype=F32)
    outs["conf_p"] = up3[:, seq - (CONF_W - 1):]
    outs["conf_s"] = jnp.concatenate([st, us3], axis=1)[:, t_new:]

    qkv_p3, qkv_s3 = qkv_p.reshape(bp, seq, QKVG_W), qkv_s.reshape(bs, t_new, QKVG_W)
    kv_shape = lambda x: x.reshape(x.shape[0], x.shape[1], 2, N_KV, HEAD_DIM)
    for name, slot in (("cmp", 0), ("slc", 2), ("win", 4)):
        c0 = ATTN_W + slot * KV_W
        t0 = seq - min(WINDOW, seq) if name == "win" else 0
        outs[name + "_p"] = kv_rows(qkv_p3, slot, t0, 512)
        outs[name + "_s"] = kv_shape(qkv_s3[:, :, c0:c0 + 2 * KV_W])
    outs["win_s"] = jnp.concatenate([cache_win_kv[e], outs["win_s"]], axis=1)[:, t_new:]

    pages_p = seq // PAGE_SIZE
    ident = jnp.arange(bp * pages_p, dtype=jnp.int32).reshape(bp, pages_p)
    w1cat = cmp_w1_cat(cmp_w1[e])
    proj_p = cmp_project(qkv_p.reshape(bp * pages_p, PAGE_SIZE, QKVG_W), False, ATTN_W // PAGE_ROW_W,
                         ident, w1cat)
    row_view = lambda c: c.reshape(-1, c.shape[-4] * 2 * N_KV, HEAD_DIM)
    pool_pages = page_table + e * n_pool
    proj_s = cmp_project(row_view(cache_cmp_kv), True, 0, pool_pages, w1cat)
    cblk_p = cmp_finish(proj_p, cmp_pe[e], w1cat, cmp_b1[e], cmp_w2[e])
    cblk_s = cmp_finish(proj_s, cmp_pe[e], w1cat, cmp_b1[e], cmp_w2[e])

    attn_p = nsa_prompt(qkv_p3, cblk_p)
    gl_s3 = jnp.pad(qkv_s3[:, :, QKV_W:QKV_W + LANES], ((0, 0), (0, S_PAD - t_new), (0, 0)))
    attn_s = nsa_sample(_pad_rows(qkv_s3[:, :, :QKV_W], 0, S_PAD), t_new, cblk_s, row_view(cache_slc_kv),
                        row_view(cache_win_kv[e]), pool_pages, gl_s3)
    a_p2, a_s2 = a_p.reshape(mp, CONF_CH), a_s[:, :t_new].reshape(ms, CONF_CH).astype(BF16)
    at_p2, at_s2 = attn_p.reshape(mp, ATTN_W), attn_s[:, :t_new].reshape(ms, ATTN_W).astype(BF16)
    stream = residual(
        [Lhs(a_p2, a_s2, [W(w_out0, e, k_rows=CONF_CH, row_blk=0)]),
         Lhs(at_p2, at_s2, [W(w_out0, e, k_rows=ATTN_W, row_blk=CONF_CH // ATTN_W)])],
        (xp, xs), tm, 512, m_outer=True)
    stream = ffn_ple(stream, 0, last=False)

    o = 0
    (bg_p, cv_p), (bg_s, cv_s) = fused_mm(
        [normed(stream, g_mix1[o], [W(w_in1, o), W(w_in1, o, col_off=SC_CH // 256),
                                    W(w_in1, o, col_off=2 * SC_CH // 256)])],
        _ep_shortconv, [F32, F32], SC_CH, 512, 256)
    cv_p3, cv_s3 = cv_p.reshape(bp, seq, SC_CH), cv_s.reshape(bs, t_new, SC_CH)
    y_p = short_conv(cv_p3, bg_p.reshape(bp, seq, SC_CH), jnp.zeros((bp, SUBLANES, SC_CH), F32),
                     sconv_w1[o], tt=256, out_dtype=BF16)
    st = state_short_conv[o]
    y_s = short_conv(_pad_rows(cv_s3, 0, S_PAD), _pad_rows(bg_s.reshape(bs, t_new, SC_CH), 0, S_PAD),
                     _pad_rows(st, SUBLANES - st.shape[1], SUBLANES), sconv_w1[o], tt=S_PAD,
                     out_dtype=F32)
    outs["sc_p"] = cv_p3[:, seq - (SC_W - 1):]
    outs["sc_s"] = jnp.concatenate([st, cv_s3], axis=1)[:, t_new:]
    stream = residual(
        [Lhs(y_p.reshape(mp, SC_CH), y_s[:, :t_new].reshape(ms, SC_CH).astype(BF16), [W(w_out1, o)])],
        stream[0], tm, 512, m_outer=True)
    (xp, xs), = ffn_ple(stream, 1, last=True)

    y_p = rmsnorm(xp, g_final, F32).reshape(bp, seq, d)
    y_s = rmsnorm(xs, g_final, F32).reshape(bs, t_new, d)
    st1 = lambda x: x[None]
    return (y_p, y_s, st1(outs["cmp_p"]), st1(outs["cmp_s"]), st1(outs["slc_p"]), st1(outs["slc_s"]),
            st1(outs["win_p"]), st1(outs["win_s"]), st1(outs["conf_p"]), st1(outs["conf_s"]),
            st1(outs["sc_p"]), st1(outs["sc_s"]))
```

```python
import functools

import jax
import jax.numpy as jnp
from jax import lax
from jax.experimental import pallas as pl
from jax.experimental.pallas import tpu as pltpu

F32 = jnp.float32
BF16 = jnp.bfloat16

V7X_VMEM_BYTES = 64 * 1024 * 1024
LANES = 128
SUBLANES = 8

D_MODEL = 4096
PAST_LEN = 16384
PAGE_SIZE = 128
N_HEADS = 16
HEAD_DIM = 128
N_KV = 4
GROUP = N_HEADS // N_KV
ATTN_W = N_HEADS * HEAD_DIM
KV_W = N_KV * HEAD_DIM
ROPE_DIM = HEAD_DIM // 4
ROPE_THETA = 500000.0
L_CMP = 32
CMP_STRIDE = 16
L_SEL = 64
N_SEL = 16
WINDOW = 512
N_BRANCH = 3
CONF_CH = D_MODEL // 2
CONF_W = 31
SC_CH = D_MODEL
SC_W = 3
RMS_EPS = 1e-6
LN_EPS = 1e-5
NEG = -1e30
FORCE = 1e9
QKV_W = ATTN_W + 6 * KV_W
QKVG_W = QKV_W + KV_W
SUB_PER_PAGE = PAGE_SIZE // CMP_STRIDE
SUB_FLAT = CMP_STRIDE * HEAD_DIM
PAGE_ROW_W = 2 * KV_W
HALO = 32
S_PAD = 8


def _vmem_limit(n_bytes):
    return int(min(V7X_VMEM_BYTES - (6 << 20), max(n_bytes, 16 << 20)))


def _params(sem, vmem):
    return pltpu.CompilerParams(dimension_semantics=sem, vmem_limit_bytes=_vmem_limit(vmem))


def _dot(a, b):
    return jnp.dot(a, b, preferred_element_type=F32)


def _dot_nt(a, b):
    return lax.dot_general(a, b, (((1,), (1,)), ((), ())), preferred_element_type=F32)


def _sigmoid(x):
    return jax.nn.sigmoid(x)


def _rmsnorm_body(x_ref, g_ref, o_ref):
    x = x_ref[...]
    y = x * lax.rsqrt(jnp.mean(x * x, axis=-1, keepdims=True) + RMS_EPS)
    o_ref[...] = (y * g_ref[...]).astype(o_ref.dtype)


def rmsnorm(x, g, out_dtype, rows=256):
    m, d = x.shape
    tr = min(rows, m)
    return pl.pallas_call(
        _rmsnorm_body,
        grid=(m // tr,),
        in_specs=[pl.BlockSpec((tr, d), lambda i: (i, 0)),
                  pl.BlockSpec((1, d), lambda i: (0, 0))],
        out_specs=pl.BlockSpec((tr, d), lambda i: (i, 0)),
        out_shape=jax.ShapeDtypeStruct((m, d), out_dtype),
        compiler_params=_params(("arbitrary",), 6 * tr * d * 4),
    )(x, g.reshape(1, d))


class W:
    def __init__(self, arr, layer=0, k_rows=None, row_blk=0, col_off=0, transposed=False):
        self.arr = arr
        self.layer = layer
        self.k_rows = arr.shape[2 if transposed else 1] if k_rows is None else k_rows
        self.row_blk = row_blk
        self.col_off = col_off
        self.transposed = transposed


class Lhs:
    def __init__(self, a_p, a_s, ws, gain=None, ssq=None):
        self.a_p, self.a_s, self.ws, self.gain, self.ssq = a_p, a_s, ws, gain, ssq


def _fold_lanes(x):
    out = x[:, 0:LANES]
    for c in range(1, x.shape[1] // LANES):
        out = out + x[:, c * LANES:(c + 1) * LANES]
    return out


def _mm_body(*refs, meta, n_rx, n_tx, n_main, emit_bf16, emit_ssq, epilogue, m_axis):
    it = iter(refs)
    groups = []
    for w_transposed, has_gain, has_ssq in meta:
        a = (next(it), next(it))
        gain = next(it) if has_gain else None
        ssq = (next(it), next(it)) if has_ssq else None
        groups.append((a, gain, ssq, [(next(it), t) for t in w_transposed]))
    rx = [[next(it) for _ in range(n_rx)] for _ in range(2)]
    tx = [[next(it) for _ in range(n_tx)] for _ in range(2)]
    n_out = n_main + emit_bf16 + emit_ssq
    outs = [[next(it) for _ in range(n_out)] for _ in range(2)]
    cache = list(it)
    i = pl.program_id(m_axis)
    j = pl.program_id(1 - m_axis)

    def prepared(w, gain):
        return (w[...] if gain is None else w[...] * gain[...]).astype(BF16)

    if cache:
        slots = iter(cache)
        wb = [[(next(slots), t) for _, t in ws] for _, _, _, ws in groups]

        @pl.when(i == 0)
        def _():
            for (_, gain, _, ws), cached in zip(groups, wb):
                for (w, _), (c, _) in zip(ws, cached):
                    c[...] = prepared(w, gain)
    else:
        wb = [[(prepared(w, gain), t) for w, t in ws] for _, gain, _, ws in groups]

    def run(which):
        dots = []
        for (a, _, ssq, _), wbs in zip(groups, wb):
            lhs = a[which][...]
            scale = None
            if ssq is not None:
                total = jnp.sum(ssq[which][...], axis=-1, keepdims=True)
                scale = lax.rsqrt(total / lhs.shape[1] + RMS_EPS)
            for w, transposed in wbs:
                w = w[...]
                d = _dot_nt(lhs, w) if transposed else _dot(lhs, w)
                dots.append(d if scale is None else d * scale)
        res = list(epilogue(dots, [r[...] for r in rx[which]], [t[...] for t in tx[which]], j))
        if emit_bf16:
            res.append(res[0])
        for o, r in zip(outs[which], res):
            o[...] = r.astype(o.dtype)
        if emit_ssq:
            part = _fold_lanes(res[0] * res[0])
            acc = outs[which][-1]
            if m_axis == 1:
                acc[...] = part
            else:
                @pl.when(j == 0)
                def _():
                    acc[...] = part

                @pl.when(j > 0)
                def _():
                    acc[...] += part

    run(0)

    @pl.when(i == 0)
    def _():
        run(1)

    if m_axis == 0:
        @pl.when(i == 1)
        def _():
            for o in outs[1]:
                o[...] = jnp.zeros(o.shape, o.dtype)


def fused_mm(groups, epilogue, out_dtypes, n_cols, tm, tn, rowx=(), tilex=(), m_outer=False,
             emit_bf16=False, emit_ssq=False, single_buffer_rows=False):
    mp = groups[0].a_p.shape[0]
    ms = groups[0].a_s.shape[0]
    nj, ni = n_cols // tn, mp // tm
    assert nj * tn == n_cols and ni * tm == mp
    m_axis = 0 if m_outer else 1
    spare = 1 if m_outer else 0

    def spec(shape, fn, **kw):
        return pl.BlockSpec(shape, lambda *g: fn(g[m_axis], g[1 - m_axis]), **kw)

    assert m_outer or not single_buffer_rows
    rows_kw = dict(pipeline_mode=pl.Buffered(1)) if single_buffer_rows else {}
    args, in_specs, meta = [], [], []
    vmem = 0
    for g in groups:
        k = g.a_p.shape[1]
        args += [g.a_p, g.a_s]
        in_specs += [spec((tm, k), lambda i, j: (i, 0), **rows_kw), spec((ms, k), lambda i, j: (0, 0))]
        vmem += ((1 if single_buffer_rows else 2) * tm + 2 * ms) * k * g.a_p.dtype.itemsize
        if g.gain is not None:
            args.append(g.gain.reshape(k, 1))
            in_specs.append(spec((k, 1), lambda i, j: (0, 0)))
            vmem += 2 * k * LANES * 4
        if g.ssq is not None:
            args += list(g.ssq)
            in_specs += [spec((tm, g.ssq[0].shape[1]), lambda i, j: (i, 0)),
                         spec(g.ssq[1].shape, lambda i, j: (0, 0))]
            vmem += 2 * (tm + ms) * g.ssq[0].shape[1] * 4
        for w in g.ws:
            assert w.k_rows == k and not (w.transposed and g.gain is not None)
            args.append(w.arr)
            if w.transposed:
                in_specs.append(spec((None, tn, k), functools.partial(
                    lambda i, j, la, rb, co: (la, co + j, rb), la=w.layer, rb=w.row_blk, co=w.col_off)))
            else:
                in_specs.append(spec((None, k, tn), functools.partial(
                    lambda i, j, la, rb, co: (la, rb, co + j), la=w.layer, rb=w.row_blk, co=w.col_off)))
            vmem += k * tn * (2 * w.arr.dtype.itemsize + 2)
        meta.append((tuple(w.transposed for w in g.ws), g.gain is not None, g.ssq is not None))
    for which in (0, 1):
        for tab_p, tab_s in rowx:
            if which == 0:
                per = tab_p.shape[0] // tm
                args.append(tab_p)
                in_specs.append(spec((tm, tab_p.shape[1]),
                                     functools.partial(lambda i, j, per: (i % per, 0), per=per)))
                vmem += 2 * tm * tab_p.shape[1] * 4
            else:
                args.append(tab_s)
                in_specs.append(spec(tab_s.shape, lambda i, j: (0, 0)))
    for which in (0, 1):
        for t_p, t_s in tilex:
            if which == 0:
                args.append(t_p)
                in_specs.append(spec((tm, tn), lambda i, j: (i, j)))
            else:
                args.append(t_s)
                in_specs.append(spec((ms, tn), lambda i, j: (0, j)))
    kinds = [(dt, tn, nj) for dt in out_dtypes]
    if emit_bf16:
        kinds.append((BF16, tn, nj))
    if emit_ssq:
        kinds.append((F32, LANES, 1 if m_outer else nj))
    out_shape, out_specs = [], []
    for which in (0, 1):
        for dt, width, nblk in kinds:
            col = (lambda i, j: j) if nblk > 1 else (lambda i, j: 0)
            if which == 0:
                out_shape.append(jax.ShapeDtypeStruct((mp, nblk * width), dt))
                out_specs.append(spec((tm, width), functools.partial(lambda i, j, col: (i, col(i, j)), col=col)))
            else:
                out_shape.append(jax.ShapeDtypeStruct((ms, (nblk + spare) * width), dt))
                out_specs.append(spec((ms, width), functools.partial(
                    lambda i, j, col, nblk: (0, jnp.where(i == 0, col(i, j), nblk) if m_outer else col(i, j)),
                    col=col, nblk=nblk)))
    n_dots = sum(len(g.ws) for g in groups)
    vmem += (2 * (len(tilex) + len(kinds)) + n_dots + 2) * tm * tn * 4
    body = functools.partial(
        _mm_body, meta=tuple(meta), n_rx=len(rowx), n_tx=len(tilex), n_main=len(out_dtypes),
        emit_bf16=emit_bf16, emit_ssq=emit_ssq, epilogue=epilogue, m_axis=m_axis)
    scratch = []
    if not m_outer:
        scratch = [pltpu.VMEM((tn, w.k_rows) if w.transposed else (w.k_rows, tn), BF16)
                   for g in groups for w in g.ws]
    res = pl.pallas_call(
        body, grid=(ni, nj) if m_outer else (nj, ni), in_specs=in_specs, out_specs=out_specs,
        out_shape=out_shape, scratch_shapes=scratch,
        compiler_params=_params(("arbitrary", "arbitrary"), vmem + (6 << 20)),
    )(*args)
    n = len(kinds)
    sample = [r[:, :nblk * width] for r, (_, width, nblk) in zip(res[n:], kinds)]
    return res[:n], sample


def _ep_glu(dots, rx, tx, j):
    return [dots[0] * _sigmoid(dots[1])]


def _ep_swiglu(dots, rx, tx, j):
    return [jax.nn.silu(dots[0]) * dots[1]]


def _ep_residual(dots, rx, tx, j):
    return [tx[0] + sum(dots[1:], dots[0])]


def _ep_ple(dots, rx, tx, j):
    return [tx[0] + _sigmoid(dots[0]) * dots[1]]


def _ep_shortconv(dots, rx, tx, j):
    return [dots[0], dots[1] * dots[2]]


def _ep_rope(dots, rx, tx, j):
    z = dots[0]
    cos, sin_lo, sin_hi = rx
    half = ROPE_DIM // 2
    heads = []
    for h in range(z.shape[1] // HEAD_DIM):
        x = z[:, h * HEAD_DIM:(h + 1) * HEAD_DIM]
        heads.append(x * cos + pltpu.roll(x, half, 1) * sin_hi
                     + pltpu.roll(x, HEAD_DIM - half, 1) * sin_lo)
    roped = jnp.concatenate(heads, axis=1)
    slot = j - ATTN_W // z.shape[1]
    is_v = jnp.logical_and(slot >= 0, slot % 2 == 1)
    lane = lax.broadcasted_iota(jnp.int32, z.shape, 1)
    gates = jnp.where(lane < N_HEADS * N_BRANCH, z, 0.0)
    return [jnp.where(slot == 6, gates, jnp.where(is_v, z, roped))]


def rope_tables(pos):
    half = ROPE_DIM // 2
    inv = ROPE_THETA ** (-2.0 * jnp.arange(half, dtype=F32) / ROPE_DIM)
    ang = pos.astype(F32)[:, None] * inv[None, :]
    cos, sin = jnp.cos(ang), jnp.sin(ang)
    n = pos.shape[0]
    rest = HEAD_DIM - ROPE_DIM
    c = jnp.concatenate([cos, cos, jnp.ones((n, rest), F32)], axis=1)
    s_lo = jnp.concatenate([-sin, jnp.zeros((n, half + rest), F32)], axis=1)
    s_hi = jnp.concatenate([jnp.zeros((n, half), F32), sin, jnp.zeros((n, rest), F32)], axis=1)
    return c, s_lo, s_hi


def _conf_core(xw_ref, cw_ref, cb_ref, g_ref, b_ref, cbuf_ref, o_ref, tt):
    last = CONF_W - 1
    for c in range(CONF_CH // LANES):
        cs = slice(c * LANES, (c + 1) * LANES)
        acc = None
        for r in range(SUBLANES):
            y = None
            for a in range((last - r) // SUBLANES + 1):
                lo = HALO - SUBLANES * (a + 1)
                term = xw_ref[lo:lo + tt + SUBLANES, cs] * cw_ref[last - SUBLANES * a - r:last - SUBLANES * a - r + 1, cs]
                y = term if y is None else y + term
            part = y[SUBLANES - r:SUBLANES - r + tt]
            acc = part if acc is None else acc + part
        cbuf_ref[:, cs] = acc
    c = cbuf_ref[...] + cb_ref[...]
    mu = jnp.mean(c, axis=-1, keepdims=True)
    var = jnp.mean(jnp.square(c - mu), axis=-1, keepdims=True)
    y = (c - mu) * lax.rsqrt(var + LN_EPS) * g_ref[...] + b_ref[...]
    o_ref[0] = jax.nn.silu(y).astype(o_ref.dtype)


def _conf_body(prev_ref, halo_ref, x_ref, cw_ref, cb_ref, g_ref, b_ref, o_ref, xw_ref, cbuf_ref, *, tt):
    first = pl.program_id(1) == 0
    xw_ref[0:HALO] = jnp.where(first, prev_ref[0], halo_ref[0])
    xw_ref[HALO:HALO + tt] = x_ref[0]
    _conf_core(xw_ref, cw_ref, cb_ref, g_ref, b_ref, cbuf_ref, o_ref, tt)


def conf_conv(u, prev, cw, cb, ln_g, ln_b, tt, out_dtype):
    b, t, c = u.shape
    if tt < HALO:
        assert t == tt
        halo_spec = pl.BlockSpec((1, HALO, c), lambda bi, i: (bi, 0, 0))
        halo_arr = prev
    else:
        hb = tt // HALO
        halo_spec = pl.BlockSpec((1, HALO, c), lambda bi, i: (bi, jnp.maximum(i * hb - 1, 0), 0))
        halo_arr = u
    cwp = jnp.pad(cw, ((0, HALO - cw.shape[0]), (0, 0)))
    row = lambda v: v.reshape(1, c)
    const = lambda bi, i: (0, 0)
    return pl.pallas_call(
        functools.partial(_conf_body, tt=tt),
        grid=(b, t // tt),
        in_specs=[pl.BlockSpec((1, HALO, c), lambda bi, i: (bi, 0, 0)), halo_spec,
                  pl.BlockSpec((1, tt, c), lambda bi, i: (bi, i, 0)),
                  pl.BlockSpec((HALO, c), const), pl.BlockSpec((1, c), const),
                  pl.BlockSpec((1, c), const), pl.BlockSpec((1, c), const)],
        out_specs=pl.BlockSpec((1, tt, c), lambda bi, i: (bi, i, 0)),
        out_shape=jax.ShapeDtypeStruct((b, t, c), out_dtype),
        scratch_shapes=[pltpu.VMEM((tt + HALO, c), F32), pltpu.VMEM((tt, c), F32)],
        compiler_params=_params(("arbitrary", "arbitrary"), 10 * (tt + HALO) * c * 4),
    )(prev, halo_arr, u, cwp, row(cb), row(ln_g), row(ln_b))


def _short_body(prev_ref, halo_ref, cv_ref, bg_ref, w_ref, o_ref, xw_ref, *, tt):
    first = pl.program_id(1) == 0
    xw_ref[0:SUBLANES] = jnp.where(first, prev_ref[0], halo_ref[0])
    xw_ref[SUBLANES:SUBLANES + tt] = cv_ref[0]
    base = SUBLANES - (SC_W - 1)
    conv = xw_ref[pl.ds(base, tt), :] * w_ref[0:1, :]
    for k in range(1, SC_W):
        conv = conv + xw_ref[pl.ds(base + k, tt), :] * w_ref[k:k + 1, :]
    o_ref[0] = (bg_ref[0] * conv).astype(o_ref.dtype)


def short_conv(cv, bg, prev, w, tt, out_dtype):
    b, t, c = cv.shape
    hb = tt // SUBLANES
    wp = jnp.pad(w, ((0, SUBLANES - w.shape[0]), (0, 0)))
    return pl.pallas_call(
        functools.partial(_short_body, tt=tt),
        grid=(b, t // tt),
        in_specs=[pl.BlockSpec((1, SUBLANES, c), lambda bi, i: (bi, 0, 0)),
                  pl.BlockSpec((1, SUBLANES, c), lambda bi, i: (bi, jnp.maximum(i * hb - 1, 0), 0)),
                  pl.BlockSpec((1, tt, c), lambda bi, i: (bi, i, 0)),
                  pl.BlockSpec((1, tt, c), lambda bi, i: (bi, i, 0)),
                  pl.BlockSpec((SUBLANES, c), lambda bi, i: (0, 0))],
        out_specs=pl.BlockSpec((1, tt, c), lambda bi, i: (bi, i, 0)),
        out_shape=jax.ShapeDtypeStruct((b, t, c), out_dtype),
        scratch_shapes=[pltpu.VMEM((tt + SUBLANES, c), F32)],
        compiler_params=_params(("arbitrary", "arbitrary"), 10 * (tt + SUBLANES) * c * 4),
    )(prev, cv, cv, bg, wp)


CMP_PAGES = 8
SUB_PITCH = 24


def _cmp_proj_body(pt_ref, *refs, row_major):
    pages = refs[:CMP_PAGES]
    w1_ref, p_ref, kbuf_ref, x_ref = refs[CMP_PAGES:]
    rows = CMP_PAGES * SUB_PER_PAGE
    for kv in range(2):
        for g in range(N_KV):
            kvg = kv * N_KV + g
            for c in range(CMP_PAGES):
                if row_major:
                    tok = pages[c][0, pl.ds(kvg, PAGE_SIZE, stride=2 * N_KV), :]
                else:
                    tok = pages[c][0, :, kvg * HEAD_DIM:(kvg + 1) * HEAD_DIM]
                for m in range(SUB_PER_PAGE):
                    r0 = (c * SUB_PER_PAGE + m) * SUB_PITCH
                    kbuf_ref[g, r0:r0 + CMP_STRIDE, :] = tok[m * CMP_STRIDE:(m + 1) * CMP_STRIDE]
            for s in range(CMP_STRIDE):
                x_ref[g * rows:(g + 1) * rows, s * HEAD_DIM:(s + 1) * HEAD_DIM] = (
                    kbuf_ref[g, pl.ds(s, rows, stride=SUB_PITCH), :])
        p = _dot(x_ref[...].astype(BF16), w1_ref[kv])
        for g in range(N_KV):
            p_ref[0, kv, g] = p[g * rows:(g + 1) * rows]


def cmp_w1_cat(w1):
    w = w1.reshape(2, 2, SUB_FLAT, HEAD_DIM).transpose(0, 2, 1, 3)
    return w.reshape(2, SUB_FLAT, 2 * HEAD_DIM).astype(BF16)


def cmp_project(pages_arr, row_major, col_blk, page_table, w1cat):
    b, n_pages = page_table.shape
    n_sub = n_pages * SUB_PER_PAGE
    steps = n_pages // CMP_PAGES
    assert steps * CMP_PAGES == n_pages
    blk = (1, PAGE_SIZE * 2 * N_KV, HEAD_DIM) if row_major else (1, PAGE_SIZE, PAGE_ROW_W)

    def page_spec(c):
        return pl.BlockSpec(blk, lambda bi, i, pt: (pt[bi, i * CMP_PAGES + c], 0, col_blk))

    rows = CMP_PAGES * SUB_PER_PAGE
    grid_spec = pltpu.PrefetchScalarGridSpec(
        num_scalar_prefetch=1, grid=(b, steps),
        in_specs=[page_spec(c) for c in range(CMP_PAGES)] + [
            pl.BlockSpec((2, SUB_FLAT, 2 * HEAD_DIM), lambda bi, i, pt: (0, 0, 0))],
        out_specs=pl.BlockSpec((1, 2, N_KV, rows, 2 * HEAD_DIM), lambda bi, i, pt: (bi, 0, 0, i, 0)),
        scratch_shapes=[pltpu.VMEM((N_KV, rows * SUB_PITCH, HEAD_DIM), F32),
                        pltpu.VMEM((N_KV * rows, SUB_FLAT), F32)])
    return pl.pallas_call(
        functools.partial(_cmp_proj_body, row_major=row_major), grid_spec=grid_spec,
        out_shape=jax.ShapeDtypeStruct((b, 2, N_KV, n_sub, 2 * HEAD_DIM), F32),
        compiler_params=_params(("arbitrary", "arbitrary"), 32 << 20),
    )(page_table, *([pages_arr] * CMP_PAGES), w1cat)


def _cmp_finish_body(p_ref, pe_ref, w1_ref, b1_ref, w2_ref, o_ref):
    n_sub = p_ref.shape[-2]
    w1 = w1_ref[0]
    pe_term = (_dot(pe_ref[0, 0].astype(BF16), w1)[0:1, 0:HEAD_DIM]
               + _dot(pe_ref[0, 1].astype(BF16), w1)[0:1, HEAD_DIM:])
    p = p_ref[0, 0, 0]
    p1_next = pltpu.roll(p[:, HEAD_DIM:], n_sub - 1, 0)
    pre = p[:, 0:HEAD_DIM] + p1_next + (b1_ref[0] + pe_term)
    o_ref[0, 0, 0] = _dot(jax.nn.gelu(pre).astype(BF16), w2_ref[0].astype(BF16)).astype(o_ref.dtype)


def cmp_finish(p, pe, w1cat, b1, w2):
    b, _, _, n_sub, _ = p.shape
    pe_rows = jnp.broadcast_to(pe.reshape(2, 2, 1, SUB_FLAT), (2, 2, SUBLANES, SUB_FLAT))
    return pl.pallas_call(
        _cmp_finish_body, grid=(b, 2, N_KV),
        in_specs=[pl.BlockSpec((1, 1, 1, n_sub, 2 * HEAD_DIM), lambda bi, kv, g: (bi, kv, g, 0, 0)),
                  pl.BlockSpec((1, 2, SUBLANES, SUB_FLAT), lambda bi, kv, g: (kv, 0, 0, 0)),
                  pl.BlockSpec((1, SUB_FLAT, 2 * HEAD_DIM), lambda bi, kv, g: (kv, 0, 0)),
                  pl.BlockSpec((1, 1, HEAD_DIM), lambda bi, kv, g: (kv, 0, 0)),
                  pl.BlockSpec((1, HEAD_DIM, HEAD_DIM), lambda bi, kv, g: (kv, 0, 0))],
        out_specs=pl.BlockSpec((1, 1, 1, n_sub, HEAD_DIM), lambda bi, kv, g: (bi, kv, g, 0, 0)),
        out_shape=jax.ShapeDtypeStruct((b, 2, N_KV, n_sub, HEAD_DIM), BF16),
        compiler_params=_params(("arbitrary",) * 3, 16 << 20),
    )(p, pe_rows, w1cat, b1.reshape(2, 1, HEAD_DIM), w2)


def _masked_softmax(s, mask):
    s = jnp.where(mask, s, NEG)
    e = jnp.exp(s - jnp.max(s, axis=-1, keepdims=True))
    return jnp.where(mask, e / jnp.sum(e, axis=-1, keepdims=True), 0.0)


def _split3(x):
    x1 = x.astype(BF16)
    r1 = x - x1.astype(F32)
    x2 = r1.astype(BF16)
    x3 = (r1 - x2.astype(F32)).astype(BF16)
    return x1, x2, x3


def _overlap_matrix(n_cmp, n_sel):
    n = lax.broadcasted_iota(jnp.int32, (n_cmp, n_sel), 0) * CMP_STRIDE
    j = lax.broadcasted_iota(jnp.int32, (n_cmp, n_sel), 1) * L_SEL
    return jnp.where(jnp.logical_and(n < j + L_SEL, n + L_CMP > j), 1.0, 0.0).astype(BF16)


def _importance(p_sum, n_sel):
    ov = _overlap_matrix(p_sum.shape[1], n_sel)
    a, b, c = _split3(p_sum)
    return _dot(a, ov) + _dot(b, ov) + _dot(c, ov)


def _importance_t(p_sum_t, n_sel):
    n_cmp = p_sum_t.shape[0]
    j = lax.broadcasted_iota(jnp.int32, (n_sel, n_cmp), 0) * L_SEL
    n = lax.broadcasted_iota(jnp.int32, (n_sel, n_cmp), 1) * CMP_STRIDE
    ov_t = jnp.where(jnp.logical_and(n < j + L_SEL, n + L_CMP > j), 1.0, 0.0).astype(BF16)
    a, b, c = _split3(p_sum_t)
    return _dot(ov_t, a) + _dot(ov_t, b) + _dot(ov_t, c)


def _select_blocks_t(imp_t, t_pos):
    n, r = imp_t.shape
    j = lax.broadcasted_iota(jnp.int32, (n, r), 0)
    cur = t_pos // L_SEL
    forced = (j == 0) | (j == cur) | (j == cur - 1)
    valid = j * L_SEL <= t_pos
    score = jnp.where(forced, FORCE, jnp.where(valid, imp_t, NEG))
    rank = jnp.zeros((n, r), F32)
    for k in range(n):
        row = score[k:k + 1, :]
        ahead = (row > score) | ((row == score) & (j > k))
        rank = rank + jnp.where(ahead, 1.0, 0.0)
    return (rank < N_SEL) & (score > 0.5 * NEG)


def _select_blocks(imp, t_pos):
    r, n = imp.shape
    j = lax.broadcasted_iota(jnp.int32, (r, n), 1)
    cur = t_pos // L_SEL
    forced = (j == 0) | (j == cur) | (j == cur - 1)
    valid = j * L_SEL <= t_pos
    score = jnp.where(forced, FORCE, jnp.where(valid, imp, NEG))
    ok = score > 0.5 * NEG
    picked = jnp.zeros((r, n), jnp.bool_)
    left = score
    jf = j.astype(F32)
    for _ in range(N_SEL):
        top = jnp.max(left, axis=-1, keepdims=True)
        first = jnp.min(jnp.where(left == top, jf, float(n)), axis=-1, keepdims=True)
        hit = jf == first
        picked = picked | hit
        left = jnp.where(hit, -jnp.inf, left)
    return picked & ok


def _flash_update(s, v, m_ref, l_ref, acc_ref):
    m_prev = m_ref[...]
    m_new = jnp.maximum(m_prev, jnp.max(s, axis=-1, keepdims=True))
    alpha = jnp.exp(m_prev - m_new)
    p = jnp.exp(s - m_new)
    l_ref[...] = alpha * l_ref[...] + jnp.sum(p, axis=-1, keepdims=True)
    acc_ref[...] = alpha * acc_ref[...] + _dot(p.astype(BF16), v)
    m_ref[...] = m_new


PQ = 128
PK = 512
P_SEL_PAD = 128
WIN_SPAN = WINDOW + PQ


def _nsa_prompt_body(q_ref, ks_ref, vs_ref, kw_ref, vw_ref, kc_ref, vc_ref, gl_ref, o_ref,
                     kaug_ref, vsb_ref, kwb_ref, vwb_ref, qa_ref, m_ref, l_ref, acc_ref, *, seq):
    i = pl.program_id(2)
    rows = GROUP * PQ
    half = rows // 2
    n_sel = seq // L_SEL
    scale = HEAD_DIM ** -0.5

    @pl.when(i == 0)
    def _():
        kaug_ref[:, 0:HEAD_DIM] = ks_ref[0].astype(BF16)
        key = lax.broadcasted_iota(jnp.int32, (seq, P_SEL_PAD), 0)
        blk = lax.broadcasted_iota(jnp.int32, (seq, P_SEL_PAD), 1)
        kaug_ref[:, HEAD_DIM:] = jnp.where(key // L_SEL == blk, 1.0, 0.0).astype(BF16)
        vsb_ref[...] = vs_ref[0].astype(BF16)
        kwb_ref[...] = kw_ref[0].astype(BF16)
        vwb_ref[...] = vw_ref[0].astype(BF16)

    q = q_ref[0]
    qs = jnp.concatenate([q[:, r * HEAD_DIM:(r + 1) * HEAD_DIM] for r in range(GROUP)], axis=0)
    qb = (qs * scale).astype(BF16)
    qa_ref[:, 0:HEAD_DIM] = qb
    t_q = i * PQ + lax.broadcasted_iota(jnp.int32, (PQ, 1), 0)
    t_rows = jnp.concatenate([t_q] * GROUP, axis=0)

    kc = kc_ref[0, 0, 0]
    n_cmp = kc.shape[0]
    s = _dot_nt(qb, kc)
    cmp_end = lax.broadcasted_iota(jnp.int32, (rows, n_cmp), 1) * CMP_STRIDE + (L_CMP - 1)
    p_cmp = _masked_softmax(s, cmp_end <= t_rows)
    o_cmp = _dot(p_cmp.astype(BF16), vc_ref[0, 0, 0])

    w0 = pl.multiple_of(jnp.clip(i * PQ - WINDOW, 0, seq - WIN_SPAN), PQ)
    s = _dot_nt(qb, kwb_ref[pl.ds(w0, WIN_SPAN), :])
    wpos = w0 + lax.broadcasted_iota(jnp.int32, (rows, WIN_SPAN), 1)
    p_win = _masked_softmax(s, (wpos <= t_rows) & (wpos > t_rows - WINDOW))
    o_win = _dot(p_win.astype(BF16), vwb_ref[pl.ds(w0, WIN_SPAN), :])

    p_sum = p_cmp[0:PQ]
    for r in range(1, GROUP):
        p_sum = p_sum + p_cmp[r * PQ:(r + 1) * PQ]
    t_lane = i * PQ + lax.broadcasted_iota(jnp.int32, (1, PQ), 1)
    sel_t = _select_blocks_t(_importance_t(p_sum.T, n_sel), t_lane)
    pen_t = jnp.concatenate([jnp.where(sel_t, 0.0, NEG), jnp.zeros((P_SEL_PAD - n_sel, PQ), F32)], axis=0)
    pen = pen_t.T.astype(BF16)
    for r in range(GROUP):
        qa_ref[r * PQ:(r + 1) * PQ, HEAD_DIM:] = pen

    m_ref[...] = jnp.full(m_ref.shape, NEG, F32)
    l_ref[...] = jnp.zeros(l_ref.shape, F32)
    acc_ref[...] = jnp.zeros(acc_ref.shape, F32)

    def slc_tile(kt, causal):
        k0 = pl.multiple_of(kt * PK, PK)
        kaug = kaug_ref[pl.ds(k0, PK), :]
        v = vsb_ref[pl.ds(k0, PK), :]
        for h in range(2):
            hs = pl.ds(h * half, half)
            s = _dot_nt(qa_ref[hs, :], kaug)
            if causal:
                kpos = k0 + lax.broadcasted_iota(jnp.int32, (half, PK), 1)
                s = jnp.where(kpos <= t_rows[h * half:(h + 1) * half], s, NEG)
            _flash_update(s, v, m_ref.at[hs], l_ref.at[hs], acc_ref.at[hs])

    n_full = (i * PQ) // PK

    def full_tile(kt, carry):
        slc_tile(kt, False)
        return carry

    lax.fori_loop(0, n_full, full_tile, 0)
    slc_tile(n_full, True)
    o_slc = acc_ref[...] / l_ref[...]

    per_group = GROUP * N_BRANCH
    gate = pltpu.roll(_sigmoid(gl_ref[0]), (LANES - pl.program_id(1) * per_group) % LANES, 1)
    for r in range(GROUP):
        rs = slice(r * PQ, (r + 1) * PQ)
        c0 = r * N_BRANCH
        o = (gate[:, c0:c0 + 1] * o_cmp[rs] + gate[:, c0 + 1:c0 + 2] * o_slc[rs]
             + gate[:, c0 + 2:c0 + 3] * o_win[rs])
        o_ref[0, :, r * HEAD_DIM:(r + 1) * HEAD_DIM] = o.astype(o_ref.dtype)


def nsa_prompt(qkv, cblk):
    b, t, _ = qkv.shape
    qcol = ATTN_W // HEAD_DIM

    def kv_spec(slot):
        return pl.BlockSpec((1, t, HEAD_DIM), lambda bi, g, i: (bi, 0, qcol + slot * N_KV + g))

    n_cmp = cblk.shape[3]
    rows = GROUP * PQ
    return pl.pallas_call(
        functools.partial(_nsa_prompt_body, seq=t),
        grid=(b, N_KV, t // PQ),
        in_specs=[pl.BlockSpec((1, PQ, GROUP * HEAD_DIM), lambda bi, g, i: (bi, i, g)),
                  kv_spec(2), kv_spec(3), kv_spec(4), kv_spec(5),
                  pl.BlockSpec((1, 1, 1, n_cmp, HEAD_DIM), lambda bi, g, i: (bi, 0, g, 0, 0)),
                  pl.BlockSpec((1, 1, 1, n_cmp, HEAD_DIM), lambda bi, g, i: (bi, 1, g, 0, 0)),
                  pl.BlockSpec((1, PQ, LANES), lambda bi, g, i: (bi, i, QKV_W // LANES))],
        out_specs=pl.BlockSpec((1, PQ, GROUP * HEAD_DIM), lambda bi, g, i: (bi, i, g)),
        out_shape=jax.ShapeDtypeStruct((b, t, ATTN_W), BF16),
        scratch_shapes=[pltpu.VMEM((t, 2 * HEAD_DIM), BF16), pltpu.VMEM((t, HEAD_DIM), BF16),
                        pltpu.VMEM((t, HEAD_DIM), BF16), pltpu.VMEM((t, HEAD_DIM), BF16),
                        pltpu.VMEM((rows, 2 * HEAD_DIM), BF16), pltpu.VMEM((rows, 1), F32), pltpu.VMEM((rows, 1), F32),
                        pltpu.VMEM((rows, HEAD_DIM), F32)],
        compiler_params=_params(("arbitrary",) * 3, 40 << 20),
    )(qkv, qkv, qkv, qkv, qkv, cblk, cblk, qkv)


SP = 8
S_KEYS = SP * PAGE_SIZE
S_BLKS = S_KEYS // L_SEL
S_CHUNK = LANES // S_BLKS
S_ROWS = GROUP * S_PAD


def _nsa_sample_body(pt_ref, *refs, t_new, n_steps, n_sel_pad):
    pages = refs[:SP]
    (q_ref, cb_ref, win_ref, gl_ref, o_ref,
     qb_ref, pen_ref, new_ref, ocw_ref, m_ref, l_ref, acc_ref) = refs[SP:]
    i = pl.program_id(1)
    scale = HEAD_DIM ** -0.5
    n_chunks = n_sel_pad // LANES
    n_win = win_ref.shape[1] // (2 * N_KV)
    tok = lax.broadcasted_iota(jnp.int32, (S_PAD, 1), 0)
    t_q = PAST_LEN + tok
    t_rows = jnp.concatenate([t_q] * GROUP, axis=0)
    qcol = ATTN_W

    @pl.when(i == 0)
    def _():
        pad = jnp.zeros((LANES - S_PAD, HEAD_DIM), F32)
        for g in range(N_KV):
            heads = [q_ref[0, :, (g * GROUP + r) * HEAD_DIM:(g * GROUP + r + 1) * HEAD_DIM]
                     for r in range(GROUP)]
            qb_ref[g] = (jnp.concatenate(heads, axis=0) * scale).astype(BF16)
            for slot in range(2, 6):
                c0 = qcol + (slot * N_KV + g) * HEAD_DIM
                new_ref[slot - 2, g] = jnp.concatenate(
                    [q_ref[0, :, c0:c0 + HEAD_DIM], pad], axis=0).astype(BF16)
        m_ref[...] = jnp.full(m_ref.shape, NEG, F32)
        l_ref[...] = jnp.zeros(l_ref.shape, F32)
        acc_ref[...] = jnp.zeros(acc_ref.shape, F32)
        p_sums = []
        for g in range(N_KV):
            qb = qb_ref[g]
            kc = cb_ref[0, 0, g]
            n_cmp = kc.shape[0]
            s = _dot_nt(qb, kc)
            cmp_end = lax.broadcasted_iota(jnp.int32, (S_ROWS, n_cmp), 1) * CMP_STRIDE + (L_CMP - 1)
            p_cmp = _masked_softmax(s, cmp_end <= t_rows)
            ocw_ref[0, g] = _dot(p_cmp.astype(BF16), cb_ref[0, 1, g])
            p_sum = p_cmp[0:S_PAD]
            for r in range(1, GROUP):
                p_sum = p_sum + p_cmp[r * S_PAD:(r + 1) * S_PAD]
            p_sums.append(p_sum)
            kw = win_ref[0, pl.ds(g, n_win, stride=2 * N_KV), :].astype(BF16)
            vw = win_ref[0, pl.ds(N_KV + g, n_win, stride=2 * N_KV), :].astype(BF16)
            s = jnp.concatenate([_dot_nt(qb, kw), _dot_nt(qb, new_ref[2, g])], axis=1)
            lane = lax.broadcasted_iota(jnp.int32, (S_ROWS, n_win + LANES), 1)
            wpos = PAST_LEN - n_win + lane
            mask = (wpos <= t_rows) & (wpos > t_rows - WINDOW) & (lane < n_win + t_new)
            p_win = _masked_softmax(s, mask).astype(BF16)
            ocw_ref[1, g] = _dot(p_win[:, 0:n_win], vw) + _dot(p_win[:, n_win:], new_ref[3, g])
        imp = _importance(jnp.concatenate(p_sums, axis=0), n_sel_pad)
        sel = _select_blocks(imp, jnp.concatenate([t_q] * N_KV, axis=0))
        pen = jnp.where(sel, 0.0, NEG).astype(BF16)
        for g in range(N_KV):
            for ch in range(n_chunks):
                blk = pen[g * S_PAD:(g + 1) * S_PAD, ch * LANES:(ch + 1) * LANES]
                pen_ref[ch, g] = jnp.concatenate([blk] * GROUP, axis=0)

    key = lax.broadcasted_iota(jnp.int32, (S_KEYS, LANES), 0)
    blk = lax.broadcasted_iota(jnp.int32, (S_KEYS, LANES), 1)
    ind = jnp.where((i % S_CHUNK) * S_BLKS + key // L_SEL == blk, 1.0, 0.0).astype(BF16)
    for g in range(N_KV):
        k = jnp.concatenate(
            [p[0, pl.ds(g, PAGE_SIZE, stride=2 * N_KV), :] for p in pages], axis=0)
        v = jnp.concatenate(
            [p[0, pl.ds(N_KV + g, PAGE_SIZE, stride=2 * N_KV), :] for p in pages], axis=0)
        kaug = jnp.concatenate([k.astype(BF16), ind], axis=1)
        qa = jnp.concatenate([qb_ref[g], pen_ref[i // S_CHUNK, g]], axis=1)
        _flash_update(_dot_nt(qa, kaug), v.astype(BF16), m_ref.at[g], l_ref.at[g], acc_ref.at[g])

    @pl.when(i == n_steps - 1)
    def _():
        gate = _sigmoid(gl_ref[0])
        lane = lax.broadcasted_iota(jnp.int32, (S_ROWS, LANES), 1)
        cur_chunk, cur_lane = (PAST_LEN // L_SEL) // LANES, (PAST_LEN // L_SEL) % LANES
        for g in range(N_KV):
            pen_cur = pen_ref[cur_chunk, g][:, cur_lane:cur_lane + 1].astype(F32)
            s = _dot_nt(qb_ref[g], new_ref[0, g]) + pen_cur
            s = jnp.where((PAST_LEN + lane <= t_rows) & (lane < t_new), s, NEG)
            _flash_update(s, new_ref[1, g], m_ref.at[g], l_ref.at[g], acc_ref.at[g])
            o_slc = acc_ref[g] / l_ref[g]
            for r in range(GROUP):
                rs = slice(r * S_PAD, (r + 1) * S_PAD)
                c0 = (g * GROUP + r) * N_BRANCH
                o = (gate[:, c0:c0 + 1] * ocw_ref[0, g][rs] + gate[:, c0 + 1:c0 + 2] * o_slc[rs]
                     + gate[:, c0 + 2:c0 + 3] * ocw_ref[1, g][rs])
                h = g * GROUP + r
                o_ref[0, :, h * HEAD_DIM:(h + 1) * HEAD_DIM] = o.astype(o_ref.dtype)


def nsa_sample(qkv, t_new, cblk, cache_slc, cache_win, page_table, gl):
    b = qkv.shape[0]
    n_pages = page_table.shape[1]
    n_steps = n_pages // SP
    n_cmp = cblk.shape[3]
    n_win = cache_win.shape[1] // (2 * N_KV)
    n_sel = -(-(PAST_LEN + t_new) // L_SEL)
    n_sel_pad = -(-n_sel // LANES) * LANES
    assert n_steps * S_BLKS <= n_sel_pad and n_steps * SP == n_pages

    def page_spec(c):
        return pl.BlockSpec((1, PAGE_SIZE * 2 * N_KV, HEAD_DIM),
                            lambda bi, i, pt: (pt[bi, i * SP + c], 0, 0))

    grid_spec = pltpu.PrefetchScalarGridSpec(
        num_scalar_prefetch=1, grid=(b, n_steps),
        in_specs=[page_spec(c) for c in range(SP)] + [
            pl.BlockSpec((1, S_PAD, QKV_W), lambda bi, i, pt: (bi, 0, 0)),
            pl.BlockSpec((1, 2, N_KV, n_cmp, HEAD_DIM), lambda bi, i, pt: (bi, 0, 0, 0, 0)),
            pl.BlockSpec((1, n_win * 2 * N_KV, HEAD_DIM), lambda bi, i, pt: (bi, 0, 0)),
            pl.BlockSpec((1, S_PAD, LANES), lambda bi, i, pt: (bi, 0, 0))],
        out_specs=pl.BlockSpec((1, S_PAD, ATTN_W), lambda bi, i, pt: (bi, 0, 0)),
        scratch_shapes=[pltpu.VMEM((N_KV, S_ROWS, HEAD_DIM), BF16),
                        pltpu.VMEM((n_sel_pad // LANES, N_KV, S_ROWS, LANES), BF16),
                        pltpu.VMEM((4, N_KV, LANES, HEAD_DIM), BF16),
                        pltpu.VMEM((2, N_KV, S_ROWS, HEAD_DIM), F32),
                        pltpu.VMEM((N_KV, S_ROWS, 1), F32), pltpu.VMEM((N_KV, S_ROWS, 1), F32),
                        pltpu.VMEM((N_KV, S_ROWS, HEAD_DIM), F32)])
    return pl.pallas_call(
        functools.partial(_nsa_sample_body, t_new=t_new, n_steps=n_steps, n_sel_pad=n_sel_pad),
        grid_spec=grid_spec,
        out_shape=jax.ShapeDtypeStruct((b, S_PAD, ATTN_W), F32),
        compiler_params=_params(("arbitrary", "arbitrary"), 40 << 20),
    )(page_table, *([cache_slc] * SP), qkv, cblk, cache_win, gl)


def _kv_rows_body(x_ref, o_ref):
    tt = x_ref.shape[1]
    for kvg in range(2 * N_KV):
        o_ref[0, pl.ds(kvg, tt, stride=2 * N_KV), :] = x_ref[0, :, kvg * HEAD_DIM:(kvg + 1) * HEAD_DIM]


def kv_rows(qkv, slot, t0, tt):
    b, t, _ = qkv.shape
    n = (t - t0) // tt
    assert n * tt == t - t0 and t0 % tt == 0
    col_blk = (ATTN_W + slot * KV_W) // PAGE_ROW_W
    out = pl.pallas_call(
        _kv_rows_body, grid=(b, n),
        in_specs=[pl.BlockSpec((1, tt, PAGE_ROW_W), lambda bi, i: (bi, t0 // tt + i, col_blk))],
        out_specs=pl.BlockSpec((1, tt * 2 * N_KV, HEAD_DIM), lambda bi, i: (bi, i, 0)),
        out_shape=jax.ShapeDtypeStruct((b, (t - t0) * 2 * N_KV, HEAD_DIM), F32),
        compiler_params=_params(("arbitrary", "arbitrary"), 6 * tt * PAGE_ROW_W * 4),
    )(qkv)
    return out.reshape(b, t - t0, 2, N_KV, HEAD_DIM)


def _pad_rows(x, front, total):
    return jnp.pad(x, ((0, 0), (front, total - front - x.shape[1]), (0, 0)))


def kernel(x_prompt, x_sample, cache_cmp_kv, cache_slc_kv, cache_win_kv, state_conf_conv,
           state_short_conv, page_table, p_prompt, p_sample, g_mix0, w_in0, conv_w0, conv_b0,
           ln_g0, ln_b0, cmp_pe, cmp_w1, cmp_b1, cmp_w2, w_out0, g_mix1, w_in1, sconv_w1, w_out1,
           g_ffn, w_ffn_gate, w_ffn_up, w_ffn_down, g_ple, w_ple_gate, w_ple_proj, g_final):
    bp, seq, d = x_prompt.shape
    bs, t_new, _ = x_sample.shape
    mp, ms = bp * seq, bs * t_new
    d_ff = w_ffn_gate.shape[-1]
    n_pool = cache_cmp_kv.shape[1]
    tm = 1024

    xp = x_prompt.reshape(mp, d)
    xs = x_sample.reshape(ms, d)
    rope_p = rope_tables(jnp.arange(seq, dtype=jnp.int32))
    rope_s = tuple(jnp.tile(tb, (bs, 1)) for tb in rope_tables(PAST_LEN + jnp.arange(t_new, dtype=jnp.int32)))
    outs = {}
    w_down_bf16 = w_ffn_down.astype(BF16)

    def residual(groups, x, tm_, tn_, single_buffer_rows=False):
        res_p, res_s = fused_mm(groups, _ep_residual, [F32], d, tm_, tn_, tilex=[x], m_outer=True,
                                emit_bf16=True, emit_ssq=True, single_buffer_rows=single_buffer_rows)
        return tuple(zip(res_p, res_s))

    def normed(stream, gain, ws):
        (_, _), (b_p, b_s), ssq = stream
        return Lhs(b_p, b_s, ws, gain=gain, ssq=ssq)

    def ffn_ple(stream, i, last):
        (gp,), (gs,) = fused_mm([normed(stream, g_ffn[i], [W(w_ffn_gate, i), W(w_ffn_up, i)])],
                                _ep_swiglu, [BF16], d_ff, tm, 256)
        stream = residual([Lhs(gp, gs, [W(w_down_bf16, i)])], stream[0], tm, 256, single_buffer_rows=True)
        pp = p_prompt[i].reshape(mp, -1).astype(BF16)
        ps = p_sample[i].reshape(ms, -1).astype(BF16)
        res_p, res_s = fused_mm([normed(stream, g_ple[i], [W(w_ple_gate, i)]), Lhs(pp, ps, [W(w_ple_proj, i)])],
                                _ep_ple, [F32], d, tm, 512, tilex=[stream[0]], m_outer=True,
                                emit_bf16=not last, emit_ssq=not last)
        return tuple(zip(res_p, res_s))

    e = 0
    hp, hs = rmsnorm(xp, g_mix0[e], BF16), rmsnorm(xs, g_mix0[e], BF16)
    w_in0_t = jnp.swapaxes(w_in0, 1, 2)
    win = lambda col_off: W(w_in0_t, e, col_off=col_off, transposed=True)
    (up,), (us,) = fused_mm([Lhs(hp, hs, [win(0), win(CONF_CH // 256)])], _ep_glu,
                            [F32], CONF_CH, tm, 256)
    (qkv_p,), (qkv_s,) = fused_mm([Lhs(hp, hs, [win(2 * CONF_CH // KV_W)])], _ep_rope,
                                  [F32], QKVG_W, tm, KV_W, rowx=list(zip(rope_p, rope_s)))

    up3, us3 = up.reshape(bp, seq, CONF_CH), us.reshape(bs, t_new, CONF_CH)
    conf_w = (conv_w0[e], conv_b0[e], ln_g0[e], ln_b0[e])
    a_p = conf_conv(up3, jnp.zeros((bp, HALO, CONF_CH), F32), *conf_w, tt=128, out_dtype=BF16)
    st = state_conf_conv[e]
    a_s = conf_conv(_pad_rows(us3, 0, S_PAD), _pad_rows(st, HALO - st.shape[1], HALO), *conf_w,
                    tt=S_PAD, out_dtype=F32)
    outs["conf_p"] = up3[:, seq - (CONF_W - 1):]
    outs["conf_s"] = jnp.concatenate([st, us3], axis=1)[:, t_new:]

    qkv_p3, qkv_s3 = qkv_p.reshape(bp, seq, QKVG_W), qkv_s.reshape(bs, t_new, QKVG_W)
    kv_shape = lambda x: x.reshape(x.shape[0], x.shape[1], 2, N_KV, HEAD_DIM)
    for name, slot in (("cmp", 0), ("slc", 2), ("win", 4)):
        c0 = ATTN_W + slot * KV_W
        t0 = seq - min(WINDOW, seq) if name == "win" else 0
        outs[name + "_p"] = kv_rows(qkv_p3, slot, t0, 512)
        outs[name + "_s"] = kv_shape(qkv_s3[:, :, c0:c0 + 2 * KV_W])
    outs["win_s"] = jnp.concatenate([cache_win_kv[e], outs["win_s"]], axis=1)[:, t_new:]

    pages_p = seq // PAGE_SIZE
    ident = jnp.arange(bp * pages_p, dtype=jnp.int32).reshape(bp, pages_p)
    w1cat = cmp_w1_cat(cmp_w1[e])
    proj_p = cmp_project(qkv_p.reshape(bp * pages_p, PAGE_SIZE, QKVG_W), False, ATTN_W // PAGE_ROW_W,
                         ident, w1cat)
    row_view = lambda c: c.reshape(-1, c.shape[-4] * 2 * N_KV, HEAD_DIM)
    pool_pages = page_table + e * n_pool
    proj_s = cmp_project(row_view(cache_cmp_kv), True, 0, pool_pages, w1cat)
    cblk_p = cmp_finish(proj_p, cmp_pe[e], w1cat, cmp_b1[e], cmp_w2[e])
    cblk_s = cmp_finish(proj_s, cmp_pe[e], w1cat, cmp_b1[e], cmp_w2[e])

    attn_p = nsa_prompt(qkv_p3, cblk_p)
    gl_s3 = jnp.pad(qkv_s3[:, :, QKV_W:QKV_W + LANES], ((0, 0), (0, S_PAD - t_new), (0, 0)))
    attn_s = nsa_sample(_pad_rows(qkv_s3[:, :, :QKV_W], 0, S_PAD), t_new, cblk_s, row_view(cache_slc_kv),
                        row_view(cache_win_kv[e]), pool_pages, gl_s3)
    a_p2, a_s2 = a_p.reshape(mp, CONF_CH), a_s[:, :t_new].reshape(ms, CONF_CH).astype(BF16)
    at_p2, at_s2 = attn_p.reshape(mp, ATTN_W), attn_s[:, :t_new].reshape(ms, ATTN_W).astype(BF16)
    stream = residual(
        [Lhs(a_p2, a_s2, [W(w_out0, e, k_rows=CONF_CH, row_blk=0)]),
         Lhs(at_p2, at_s2, [W(w_out0, e, k_rows=ATTN_W, row_blk=CONF_CH // ATTN_W)])],
        (xp, xs), tm, 512)
    stream = ffn_ple(stream, 0, last=False)

    o = 0
    (bg_p, cv_p), (bg_s, cv_s) = fused_mm(
        [normed(stream, g_mix1[o], [W(w_in1, o), W(w_in1, o, col_off=SC_CH // 256),
                                    W(w_in1, o, col_off=2 * SC_CH // 256)])],
        _ep_shortconv, [F32, F32], SC_CH, 512, 256)
    cv_p3, cv_s3 = cv_p.reshape(bp, seq, SC_CH), cv_s.reshape(bs, t_new, SC_CH)
    y_p = short_conv(cv_p3, bg_p.reshape(bp, seq, SC_CH), jnp.zeros((bp, SUBLANES, SC_CH), F32),
                     sconv_w1[o], tt=256, out_dtype=BF16)
    st = state_short_conv[o]
    y_s = short_conv(_pad_rows(cv_s3, 0, S_PAD), _pad_rows(bg_s.reshape(bs, t_new, SC_CH), 0, S_PAD),
                     _pad_rows(st, SUBLANES - st.shape[1], SUBLANES), sconv_w1[o], tt=S_PAD,
                     out_dtype=F32)
    outs["sc_p"] = cv_p3[:, seq - (SC_W - 1):]
    outs["sc_s"] = jnp.concatenate([st, cv_s3], axis=1)[:, t_new:]
    stream = residual(
        [Lhs(y_p.reshape(mp, SC_CH), y_s[:, :t_new].reshape(ms, SC_CH).astype(BF16), [W(w_out1, o)])],
        stream[0], tm, 512)
    (xp, xs), = ffn_ple(stream, 1, last=True)

    y_p = rmsnorm(xp, g_final, F32).reshape(bp, seq, d)
    y_s = rmsnorm(xs, g_final, F32).reshape(bs, t_new, d)
    st1 = lambda x: x[None]
    return (y_p, y_s, st1(outs["cmp_p"]), st1(outs["cmp_s"]), st1(outs["slc_p"]), st1(outs["slc_s"]),
            st1(outs["win_p"]), st1(outs["win_s"]), st1(outs["conf_p"]), st1(outs["conf_s"]),
            st1(outs["sc_p"]), st1(outs["sc_s"]))
```

```python
import functools

import jax
import jax.numpy as jnp
from jax import lax
from jax.experimental import pallas as pl
from jax.experimental.pallas import tpu as pltpu

F32 = jnp.float32
BF16 = jnp.bfloat16

V7X_VMEM_BYTES = 64 * 1024 * 1024
LANES = 128
SUBLANES = 8

D_MODEL = 4096
PAST_LEN = 16384
PAGE_SIZE = 128
N_HEADS = 16
HEAD_DIM = 128
N_KV = 4
GROUP = N_HEADS // N_KV
ATTN_W = N_HEADS * HEAD_DIM
KV_W = N_KV * HEAD_DIM
ROPE_DIM = HEAD_DIM // 4
ROPE_THETA = 500000.0
L_CMP = 32
CMP_STRIDE = 16
L_SEL = 64
N_SEL = 16
WINDOW = 512
N_BRANCH = 3
CONF_CH = D_MODEL // 2
CONF_W = 31
SC_CH = D_MODEL
SC_W = 3
RMS_EPS = 1e-6
LN_EPS = 1e-5
NEG = -1e30
FORCE = 1e9
QKV_W = ATTN_W + 6 * KV_W
QKVG_W = QKV_W + KV_W
SUB_PER_PAGE = PAGE_SIZE // CMP_STRIDE
SUB_FLAT = CMP_STRIDE * HEAD_DIM
PAGE_ROW_W = 2 * KV_W
HALO = 32
S_PAD = 8


def _vmem_limit(n_bytes):
    return int(min(V7X_VMEM_BYTES - (6 << 20), max(n_bytes, 16 << 20)))


def _params(sem, vmem):
    return pltpu.CompilerParams(dimension_semantics=sem, vmem_limit_bytes=_vmem_limit(vmem))


def _dot(a, b):
    return jnp.dot(a, b, preferred_element_type=F32)


def _dot_nt(a, b):
    return lax.dot_general(a, b, (((1,), (1,)), ((), ())), preferred_element_type=F32)


def _sigmoid(x):
    return jax.nn.sigmoid(x)


def _rmsnorm_body(x_ref, g_ref, o_ref):
    x = x_ref[...]
    y = x * lax.rsqrt(jnp.mean(x * x, axis=-1, keepdims=True) + RMS_EPS)
    o_ref[...] = (y * g_ref[...]).astype(o_ref.dtype)


def rmsnorm(x, g, out_dtype, rows=256):
    m, d = x.shape
    tr = min(rows, m)
    return pl.pallas_call(
        _rmsnorm_body,
        grid=(m // tr,),
        in_specs=[pl.BlockSpec((tr, d), lambda i: (i, 0)),
                  pl.BlockSpec((1, d), lambda i: (0, 0))],
        out_specs=pl.BlockSpec((tr, d), lambda i: (i, 0)),
        out_shape=jax.ShapeDtypeStruct((m, d), out_dtype),
        compiler_params=_params(("arbitrary",), 6 * tr * d * 4),
    )(x, g.reshape(1, d))


class W:
    def __init__(self, arr, layer=0, k_rows=None, row_blk=0, col_off=0, transposed=False):
        self.arr = arr
        self.layer = layer
        self.k_rows = arr.shape[2 if transposed else 1] if k_rows is None else k_rows
        self.row_blk = row_blk
        self.col_off = col_off
        self.transposed = transposed


class Lhs:
    def __init__(self, a_p, a_s, ws, gain=None, ssq=None):
        self.a_p, self.a_s, self.ws, self.gain, self.ssq = a_p, a_s, ws, gain, ssq


def _fold_lanes(x):
    out = x[:, 0:LANES]
    for c in range(1, x.shape[1] // LANES):
        out = out + x[:, c * LANES:(c + 1) * LANES]
    return out


def _mm_body(*refs, meta, n_rx, n_tx, n_main, emit_bf16, emit_ssq, epilogue, m_axis, n_j):
    it = iter(refs)
    groups = []
    for w_transposed, has_gain, has_ssq in meta:
        a = (next(it), next(it))
        gain = next(it) if has_gain else None
        ssq = (next(it), next(it)) if has_ssq else None
        groups.append((a, gain, ssq, [(next(it), t) for t in w_transposed]))
    rx = [[next(it) for _ in range(n_rx)] for _ in range(2)]
    tx = [[next(it) for _ in range(n_tx)] for _ in range(2)]
    n_out = n_main + emit_bf16 + emit_ssq
    outs = [[next(it) for _ in range(n_out)] for _ in range(2)]
    cache = list(it)
    i = pl.program_id(m_axis)
    j = pl.program_id(1 - m_axis)

    def prepared(w, gain):
        return (w[...] if gain is None else w[...] * gain[...]).astype(BF16)

    if cache:
        slots = iter(cache)
        wb = [[(next(slots), t) for _, t in ws] for _, _, _, ws in groups]

        @pl.when(i == 0)
        def _():
            for (_, gain, _, ws), cached in zip(groups, wb):
                for (w, _), (c, _) in zip(ws, cached):
                    c[...] = prepared(w, gain)
    else:
        wb = [[(prepared(w, gain), t) for w, t in ws] for _, gain, _, ws in groups]

    def run(which):
        dots = []
        for (a, _, ssq, _), wbs in zip(groups, wb):
            lhs = a[which][...]
            scale = None if ssq is None else ssq[which][...]
            for w, transposed in wbs:
                w = w[...]
                d = _dot_nt(lhs, w) if transposed else _dot(lhs, w)
                if scale is not None:
                    d = d * jnp.concatenate([scale] * (d.shape[1] // LANES), axis=1)
                dots.append(d)
        res = list(epilogue(dots, [r[...] for r in rx[which]], [t[...] for t in tx[which]], j))
        if emit_bf16:
            res.append(res[0])
        for o, r in zip(outs[which], res):
            o[...] = r.astype(o.dtype)
        if emit_ssq:
            part = _fold_lanes(res[0] * res[0])
            acc = outs[which][-1]

            @pl.when(j == 0)
            def _():
                acc[...] = part

            @pl.when(j > 0)
            def _():
                acc[...] += part

            @pl.when(j == n_j - 1)
            def _():
                total = jnp.sum(acc[...], axis=-1, keepdims=True)
                acc[...] = jnp.broadcast_to(lax.rsqrt(total / (n_j * res[0].shape[1]) + RMS_EPS), acc.shape)

    run(0)

    @pl.when(i == 0)
    def _():
        run(1)

    if m_axis == 0:
        @pl.when(i == 1)
        def _():
            for o in outs[1]:
                o[...] = jnp.zeros(o.shape, o.dtype)


def fused_mm(groups, epilogue, out_dtypes, n_cols, tm, tn, rowx=(), tilex=(), m_outer=False,
             emit_bf16=False, emit_ssq=False, single_buffer_rows=False):
    mp = groups[0].a_p.shape[0]
    ms = groups[0].a_s.shape[0]
    nj, ni = n_cols // tn, mp // tm
    assert nj * tn == n_cols and ni * tm == mp
    m_axis = 0 if m_outer else 1
    assert m_outer or not emit_ssq
    spare = 1 if m_outer else 0

    def spec(shape, fn, **kw):
        return pl.BlockSpec(shape, lambda *g: fn(g[m_axis], g[1 - m_axis]), **kw)

    assert m_outer or not single_buffer_rows
    rows_kw = dict(pipeline_mode=pl.Buffered(1)) if single_buffer_rows else {}
    args, in_specs, meta = [], [], []
    vmem = 0
    for g in groups:
        k = g.a_p.shape[1]
        args += [g.a_p, g.a_s]
        in_specs += [spec((tm, k), lambda i, j: (i, 0), **rows_kw), spec((ms, k), lambda i, j: (0, 0))]
        vmem += ((1 if single_buffer_rows else 2) * tm + 2 * ms) * k * g.a_p.dtype.itemsize
        if g.gain is not None:
            args.append(g.gain.reshape(k, 1))
            in_specs.append(spec((k, 1), lambda i, j: (0, 0)))
            vmem += 2 * k * LANES * 4
        if g.ssq is not None:
            args += list(g.ssq)
            in_specs += [spec((tm, g.ssq[0].shape[1]), lambda i, j: (i, 0)),
                         spec(g.ssq[1].shape, lambda i, j: (0, 0))]
            vmem += 2 * (tm + ms) * g.ssq[0].shape[1] * 4
        for w in g.ws:
            assert w.k_rows == k and not (w.transposed and g.gain is not None)
            args.append(w.arr)
            if w.transposed:
                in_specs.append(spec((None, tn, k), functools.partial(
                    lambda i, j, la, rb, co: (la, co + j, rb), la=w.layer, rb=w.row_blk, co=w.col_off)))
            else:
                in_specs.append(spec((None, k, tn), functools.partial(
                    lambda i, j, la, rb, co: (la, rb, co + j), la=w.layer, rb=w.row_blk, co=w.col_off)))
            vmem += k * tn * (2 * w.arr.dtype.itemsize + 2)
        meta.append((tuple(w.transposed for w in g.ws), g.gain is not None, g.ssq is not None))
    for which in (0, 1):
        for tab_p, tab_s in rowx:
            if which == 0:
                per = tab_p.shape[0] // tm
                args.append(tab_p)
                in_specs.append(spec((tm, tab_p.shape[1]),
                                     functools.partial(lambda i, j, per: (i % per, 0), per=per)))
                vmem += 2 * tm * tab_p.shape[1] * 4
            else:
                args.append(tab_s)
                in_specs.append(spec(tab_s.shape, lambda i, j: (0, 0)))
    for which in (0, 1):
        for t_p, t_s in tilex:
            if which == 0:
                args.append(t_p)
                in_specs.append(spec((tm, tn), lambda i, j: (i, j)))
            else:
                args.append(t_s)
                in_specs.append(spec((ms, tn), lambda i, j: (0, j)))
    kinds = [(dt, tn, nj) for dt in out_dtypes]
    if emit_bf16:
        kinds.append((BF16, tn, nj))
    if emit_ssq:
        kinds.append((F32, LANES, 1))
    out_shape, out_specs = [], []
    for which in (0, 1):
        for dt, width, nblk in kinds:
            col = (lambda i, j: j) if nblk > 1 else (lambda i, j: 0)
            if which == 0:
                out_shape.append(jax.ShapeDtypeStruct((mp, nblk * width), dt))
                out_specs.append(spec((tm, width), functools.partial(lambda i, j, col: (i, col(i, j)), col=col)))
            else:
                out_shape.append(jax.ShapeDtypeStruct((ms, (nblk + spare) * width), dt))
                out_specs.append(spec((ms, width), functools.partial(
                    lambda i, j, col, nblk: (0, jnp.where(i == 0, col(i, j), nblk) if m_outer else col(i, j)),
                    col=col, nblk=nblk)))
    n_dots = sum(len(g.ws) for g in groups)
    vmem += (2 * (len(tilex) + len(kinds)) + n_dots + 2) * tm * tn * 4
    body = functools.partial(
        _mm_body, meta=tuple(meta), n_rx=len(rowx), n_tx=len(tilex), n_main=len(out_dtypes),
        emit_bf16=emit_bf16, emit_ssq=emit_ssq, epilogue=epilogue, m_axis=m_axis, n_j=nj)
    scratch = []
    if not m_outer and any(g.gain is not None for g in groups):
        scratch = [pltpu.VMEM((tn, w.k_rows) if w.transposed else (w.k_rows, tn), BF16)
                   for g in groups for w in g.ws]
    res = pl.pallas_call(
        body, grid=(ni, nj) if m_outer else (nj, ni), in_specs=in_specs, out_specs=out_specs,
        out_shape=out_shape, scratch_shapes=scratch,
        compiler_params=_params(("arbitrary", "arbitrary"), vmem + (6 << 20)),
    )(*args)
    n = len(kinds)
    sample = [r[:, :nblk * width] for r, (_, width, nblk) in zip(res[n:], kinds)]
    return res[:n], sample


def _ep_glu(dots, rx, tx, j):
    return [dots[0] * _sigmoid(dots[1])]


def _ep_swiglu(dots, rx, tx, j):
    return [jax.nn.silu(dots[0]) * dots[1]]


def _ep_residual(dots, rx, tx, j):
    return [tx[0] + sum(dots[1:], dots[0])]


def _ep_ple(dots, rx, tx, j):
    return [tx[0] + _sigmoid(dots[0]) * dots[1]]


def _ep_shortconv(dots, rx, tx, j):
    return [dots[0], dots[1] * dots[2]]


def _ep_rope(dots, rx, tx, j):
    z = dots[0]
    cos, sin_lo, sin_hi = rx
    half = ROPE_DIM // 2
    heads = []
    for h in range(z.shape[1] // HEAD_DIM):
        x = z[:, h * HEAD_DIM:(h + 1) * HEAD_DIM]
        heads.append(x * cos + pltpu.roll(x, half, 1) * sin_hi
                     + pltpu.roll(x, HEAD_DIM - half, 1) * sin_lo)
    roped = jnp.concatenate(heads, axis=1)
    slot = j - ATTN_W // z.shape[1]
    is_v = jnp.logical_and(slot >= 0, slot % 2 == 1)
    lane = lax.broadcasted_iota(jnp.int32, z.shape, 1)
    gates = jnp.where(lane < N_HEADS * N_BRANCH, z, 0.0)
    return [jnp.where(slot == 6, gates, jnp.where(is_v, z, roped))]


def rope_tables(pos):
    half = ROPE_DIM // 2
    inv = ROPE_THETA ** (-2.0 * jnp.arange(half, dtype=F32) / ROPE_DIM)
    ang = pos.astype(F32)[:, None] * inv[None, :]
    cos, sin = jnp.cos(ang), jnp.sin(ang)
    n = pos.shape[0]
    rest = HEAD_DIM - ROPE_DIM
    c = jnp.concatenate([cos, cos, jnp.ones((n, rest), F32)], axis=1)
    s_lo = jnp.concatenate([-sin, jnp.zeros((n, half + rest), F32)], axis=1)
    s_hi = jnp.concatenate([jnp.zeros((n, half), F32), sin, jnp.zeros((n, rest), F32)], axis=1)
    return c, s_lo, s_hi


def _conf_core(xw_ref, cw_ref, cb_ref, g_ref, b_ref, cbuf_ref, o_ref, tt):
    last = CONF_W - 1
    for c in range(CONF_CH // LANES):
        cs = slice(c * LANES, (c + 1) * LANES)
        acc = None
        for r in range(SUBLANES):
            y = None
            for a in range((last - r) // SUBLANES + 1):
                lo = HALO - SUBLANES * (a + 1)
                term = xw_ref[lo:lo + tt + SUBLANES, cs] * cw_ref[last - SUBLANES * a - r:last - SUBLANES * a - r + 1, cs]
                y = term if y is None else y + term
            part = y[SUBLANES - r:SUBLANES - r + tt]
            acc = part if acc is None else acc + part
        cbuf_ref[:, cs] = acc
    c = cbuf_ref[...] + cb_ref[...]
    mu = jnp.mean(c, axis=-1, keepdims=True)
    var = jnp.mean(jnp.square(c - mu), axis=-1, keepdims=True)
    y = (c - mu) * lax.rsqrt(var + LN_EPS) * g_ref[...] + b_ref[...]
    o_ref[0] = jax.nn.silu(y).astype(o_ref.dtype)


def _conf_body(prev_ref, halo_ref, x_ref, cw_ref, cb_ref, g_ref, b_ref, o_ref, xw_ref, cbuf_ref, *, tt):
    first = pl.program_id(1) == 0
    xw_ref[0:HALO] = jnp.where(first, prev_ref[0], halo_ref[0])
    xw_ref[HALO:HALO + tt] = x_ref[0]
    _conf_core(xw_ref, cw_ref, cb_ref, g_ref, b_ref, cbuf_ref, o_ref, tt)


def conf_conv(u, prev, cw, cb, ln_g, ln_b, tt, out_dtype):
    b, t, c = u.shape
    if tt < HALO:
        assert t == tt
        halo_spec = pl.BlockSpec((1, HALO, c), lambda bi, i: (bi, 0, 0))
        halo_arr = prev
    else:
        hb = tt // HALO
        halo_spec = pl.BlockSpec((1, HALO, c), lambda bi, i: (bi, jnp.maximum(i * hb - 1, 0), 0))
        halo_arr = u
    cwp = jnp.pad(cw, ((0, HALO - cw.shape[0]), (0, 0)))
    row = lambda v: v.reshape(1, c)
    const = lambda bi, i: (0, 0)
    return pl.pallas_call(
        functools.partial(_conf_body, tt=tt),
        grid=(b, t // tt),
        in_specs=[pl.BlockSpec((1, HALO, c), lambda bi, i: (bi, 0, 0)), halo_spec,
                  pl.BlockSpec((1, tt, c), lambda bi, i: (bi, i, 0)),
                  pl.BlockSpec((HALO, c), const), pl.BlockSpec((1, c), const),
                  pl.BlockSpec((1, c), const), pl.BlockSpec((1, c), const)],
        out_specs=pl.BlockSpec((1, tt, c), lambda bi, i: (bi, i, 0)),
        out_shape=jax.ShapeDtypeStruct((b, t, c), out_dtype),
        scratch_shapes=[pltpu.VMEM((tt + HALO, c), F32), pltpu.VMEM((tt, c), F32)],
        compiler_params=_params(("arbitrary", "arbitrary"), 10 * (tt + HALO) * c * 4),
    )(prev, halo_arr, u, cwp, row(cb), row(ln_g), row(ln_b))


def _short_body(prev_ref, halo_ref, cv_ref, bg_ref, w_ref, o_ref, xw_ref, *, tt):
    first = pl.program_id(1) == 0
    xw_ref[0:SUBLANES] = jnp.where(first, prev_ref[0], halo_ref[0])
    xw_ref[SUBLANES:SUBLANES + tt] = cv_ref[0]
    base = SUBLANES - (SC_W - 1)
    conv = xw_ref[pl.ds(base, tt), :] * w_ref[0:1, :]
    for k in range(1, SC_W):
        conv = conv + xw_ref[pl.ds(base + k, tt), :] * w_ref[k:k + 1, :]
    o_ref[0] = (bg_ref[0] * conv).astype(o_ref.dtype)


def short_conv(cv, bg, prev, w, tt, out_dtype):
    b, t, c = cv.shape
    hb = tt // SUBLANES
    wp = jnp.pad(w, ((0, SUBLANES - w.shape[0]), (0, 0)))
    return pl.pallas_call(
        functools.partial(_short_body, tt=tt),
        grid=(b, t // tt),
        in_specs=[pl.BlockSpec((1, SUBLANES, c), lambda bi, i: (bi, 0, 0)),
                  pl.BlockSpec((1, SUBLANES, c), lambda bi, i: (bi, jnp.maximum(i * hb - 1, 0), 0)),
                  pl.BlockSpec((1, tt, c), lambda bi, i: (bi, i, 0)),
                  pl.BlockSpec((1, tt, c), lambda bi, i: (bi, i, 0)),
                  pl.BlockSpec((SUBLANES, c), lambda bi, i: (0, 0))],
        out_specs=pl.BlockSpec((1, tt, c), lambda bi, i: (bi, i, 0)),
        out_shape=jax.ShapeDtypeStruct((b, t, c), out_dtype),
        scratch_shapes=[pltpu.VMEM((tt + SUBLANES, c), F32)],
        compiler_params=_params(("arbitrary", "arbitrary"), 10 * (tt + SUBLANES) * c * 4),
    )(prev, cv, cv, bg, wp)


CMP_PAGES = 8
SUB_PITCH = 24


def _cmp_proj_body(pt_ref, *refs, row_major):
    pages = refs[:CMP_PAGES]
    w1_ref, p_ref, kbuf_ref, x_ref = refs[CMP_PAGES:]
    rows = CMP_PAGES * SUB_PER_PAGE
    for kv in range(2):
        for g in range(N_KV):
            kvg = kv * N_KV + g
            for c in range(CMP_PAGES):
                if row_major:
                    tok = pages[c][0, pl.ds(kvg, PAGE_SIZE, stride=2 * N_KV), :]
                else:
                    tok = pages[c][0, :, kvg * HEAD_DIM:(kvg + 1) * HEAD_DIM]
                for m in range(SUB_PER_PAGE):
                    r0 = (c * SUB_PER_PAGE + m) * SUB_PITCH
                    kbuf_ref[g, r0:r0 + CMP_STRIDE, :] = tok[m * CMP_STRIDE:(m + 1) * CMP_STRIDE]
            for s in range(CMP_STRIDE):
                x_ref[g * rows:(g + 1) * rows, s * HEAD_DIM:(s + 1) * HEAD_DIM] = (
                    kbuf_ref[g, pl.ds(s, rows, stride=SUB_PITCH), :])
        p = _dot(x_ref[...].astype(BF16), w1_ref[kv])
        for g in range(N_KV):
            p_ref[0, kv, g] = p[g * rows:(g + 1) * rows]


def cmp_w1_cat(w1):
    w = w1.reshape(2, 2, SUB_FLAT, HEAD_DIM).transpose(0, 2, 1, 3)
    return w.reshape(2, SUB_FLAT, 2 * HEAD_DIM).astype(BF16)


def cmp_project(pages_arr, row_major, col_blk, page_table, w1cat):
    b, n_pages = page_table.shape
    n_sub = n_pages * SUB_PER_PAGE
    steps = n_pages // CMP_PAGES
    assert steps * CMP_PAGES == n_pages
    blk = (1, PAGE_SIZE * 2 * N_KV, HEAD_DIM) if row_major else (1, PAGE_SIZE, PAGE_ROW_W)

    def page_spec(c):
        return pl.BlockSpec(blk, lambda bi, i, pt: (pt[bi, i * CMP_PAGES + c], 0, col_blk))

    rows = CMP_PAGES * SUB_PER_PAGE
    grid_spec = pltpu.PrefetchScalarGridSpec(
        num_scalar_prefetch=1, grid=(b, steps),
        in_specs=[page_spec(c) for c in range(CMP_PAGES)] + [
            pl.BlockSpec((2, SUB_FLAT, 2 * HEAD_DIM), lambda bi, i, pt: (0, 0, 0))],
        out_specs=pl.BlockSpec((1, 2, N_KV, rows, 2 * HEAD_DIM), lambda bi, i, pt: (bi, 0, 0, i, 0)),
        scratch_shapes=[pltpu.VMEM((N_KV, rows * SUB_PITCH, HEAD_DIM), F32),
                        pltpu.VMEM((N_KV * rows, SUB_FLAT), F32)])
    return pl.pallas_call(
        functools.partial(_cmp_proj_body, row_major=row_major), grid_spec=grid_spec,
        out_shape=jax.ShapeDtypeStruct((b, 2, N_KV, n_sub, 2 * HEAD_DIM), F32),
        compiler_params=_params(("arbitrary", "arbitrary"), 32 << 20),
    )(page_table, *([pages_arr] * CMP_PAGES), w1cat)


def _cmp_finish_body(p_ref, pe_ref, w1_ref, b1_ref, w2_ref, o_ref):
    n_sub = p_ref.shape[-2]
    w1 = w1_ref[0]
    pe_term = (_dot(pe_ref[0, 0].astype(BF16), w1)[0:1, 0:HEAD_DIM]
               + _dot(pe_ref[0, 1].astype(BF16), w1)[0:1, HEAD_DIM:])
    p = p_ref[0, 0, 0]
    p1_next = pltpu.roll(p[:, HEAD_DIM:], n_sub - 1, 0)
    pre = p[:, 0:HEAD_DIM] + p1_next + (b1_ref[0] + pe_term)
    o_ref[0, 0, 0] = _dot(jax.nn.gelu(pre).astype(BF16), w2_ref[0].astype(BF16)).astype(o_ref.dtype)


def cmp_finish(p, pe, w1cat, b1, w2):
    b, _, _, n_sub, _ = p.shape
    pe_rows = jnp.broadcast_to(pe.reshape(2, 2, 1, SUB_FLAT), (2, 2, SUBLANES, SUB_FLAT))
    return pl.pallas_call(
        _cmp_finish_body, grid=(b, 2, N_KV),
        in_specs=[pl.BlockSpec((1, 1, 1, n_sub, 2 * HEAD_DIM), lambda bi, kv, g: (bi, kv, g, 0, 0)),
                  pl.BlockSpec((1, 2, SUBLANES, SUB_FLAT), lambda bi, kv, g: (kv, 0, 0, 0)),
                  pl.BlockSpec((1, SUB_FLAT, 2 * HEAD_DIM), lambda bi, kv, g: (kv, 0, 0)),
                  pl.BlockSpec((1, 1, HEAD_DIM), lambda bi, kv, g: (kv, 0, 0)),
                  pl.BlockSpec((1, HEAD_DIM, HEAD_DIM), lambda bi, kv, g: (kv, 0, 0))],
        out_specs=pl.BlockSpec((1, 1, 1, n_sub, HEAD_DIM), lambda bi, kv, g: (bi, kv, g, 0, 0)),
        out_shape=jax.ShapeDtypeStruct((b, 2, N_KV, n_sub, HEAD_DIM), BF16),
        compiler_params=_params(("arbitrary",) * 3, 16 << 20),
    )(p, pe_rows, w1cat, b1.reshape(2, 1, HEAD_DIM), w2)


def _masked_softmax(s, mask):
    s = jnp.where(mask, s, NEG)
    e = jnp.exp(s - jnp.max(s, axis=-1, keepdims=True))
    return jnp.where(mask, e / jnp.sum(e, axis=-1, keepdims=True), 0.0)


def _split3(x):
    x1 = x.astype(BF16)
    r1 = x - x1.astype(F32)
    x2 = r1.astype(BF16)
    x3 = (r1 - x2.astype(F32)).astype(BF16)
    return x1, x2, x3


def _overlap_matrix(n_cmp, n_sel):
    n = lax.broadcasted_iota(jnp.int32, (n_cmp, n_sel), 0) * CMP_STRIDE
    j = lax.broadcasted_iota(jnp.int32, (n_cmp, n_sel), 1) * L_SEL
    return jnp.where(jnp.logical_and(n < j + L_SEL, n + L_CMP > j), 1.0, 0.0).astype(BF16)


def _importance(p_sum, n_sel):
    ov = _overlap_matrix(p_sum.shape[1], n_sel)
    a, b, c = _split3(p_sum)
    return _dot(a, ov) + _dot(b, ov) + _dot(c, ov)


def _importance_t(p_sum_t, n_sel):
    n_cmp = p_sum_t.shape[0]
    j = lax.broadcasted_iota(jnp.int32, (n_sel, n_cmp), 0) * L_SEL
    n = lax.broadcasted_iota(jnp.int32, (n_sel, n_cmp), 1) * CMP_STRIDE
    ov_t = jnp.where(jnp.logical_and(n < j + L_SEL, n + L_CMP > j), 1.0, 0.0).astype(BF16)
    a, b, c = _split3(p_sum_t)
    return _dot(ov_t, a) + _dot(ov_t, b) + _dot(ov_t, c)


def _select_blocks_t(imp_t, t_pos):
    n, r = imp_t.shape
    j = lax.broadcasted_iota(jnp.int32, (n, r), 0)
    cur = t_pos // L_SEL
    forced = (j == 0) | (j == cur) | (j == cur - 1)
    valid = j * L_SEL <= t_pos
    score = jnp.where(forced, FORCE, jnp.where(valid, imp_t, NEG))
    rank = jnp.zeros((n, r), F32)
    for k in range(n):
        row = score[k:k + 1, :]
        ahead = (row > score) | ((row == score) & (j > k))
        rank = rank + jnp.where(ahead, 1.0, 0.0)
    return (rank < N_SEL) & (score > 0.5 * NEG)


def _select_blocks(imp, t_pos):
    r, n = imp.shape
    j = lax.broadcasted_iota(jnp.int32, (r, n), 1)
    cur = t_pos // L_SEL
    forced = (j == 0) | (j == cur) | (j == cur - 1)
    valid = j * L_SEL <= t_pos
    score = jnp.where(forced, FORCE, jnp.where(valid, imp, NEG))
    ok = score > 0.5 * NEG
    picked = jnp.zeros((r, n), jnp.bool_)
    left = score
    jf = j.astype(F32)
    for _ in range(N_SEL):
        top = jnp.max(left, axis=-1, keepdims=True)
        first = jnp.min(jnp.where(left == top, jf, float(n)), axis=-1, keepdims=True)
        hit = jf == first
        picked = picked | hit
        left = jnp.where(hit, -jnp.inf, left)
    return picked & ok


def _flash_update(s, v, m_ref, l_ref, acc_ref):
    m_prev = m_ref[...]
    m_new = jnp.maximum(m_prev, jnp.max(s, axis=-1, keepdims=True))
    alpha = jnp.exp(m_prev - m_new)
    p = jnp.exp(s - m_new)
    l_ref[...] = alpha * l_ref[...] + jnp.sum(p, axis=-1, keepdims=True)
    acc_ref[...] = alpha * acc_ref[...] + _dot(p.astype(BF16), v)
    m_ref[...] = m_new


PQ = 128
PK = 512
P_SEL_PAD = 128
WIN_SPAN = WINDOW + PQ


def _nsa_prompt_body(q_ref, ks_ref, vs_ref, kw_ref, vw_ref, kc_ref, vc_ref, gl_ref, o_ref,
                     kaug_ref, vsb_ref, kwb_ref, vwb_ref, qa_ref, m_ref, l_ref, acc_ref, *, seq):
    i = pl.program_id(2)
    rows = GROUP * PQ
    half = rows // 2
    n_sel = seq // L_SEL
    scale = HEAD_DIM ** -0.5

    @pl.when(i == 0)
    def _():
        kaug_ref[:, 0:HEAD_DIM] = ks_ref[0].astype(BF16)
        key = lax.broadcasted_iota(jnp.int32, (seq, P_SEL_PAD), 0)
        blk = lax.broadcasted_iota(jnp.int32, (seq, P_SEL_PAD), 1)
        kaug_ref[:, HEAD_DIM:] = jnp.where(key // L_SEL == blk, 1.0, 0.0).astype(BF16)
        vsb_ref[...] = vs_ref[0].astype(BF16)
        kwb_ref[...] = kw_ref[0].astype(BF16)
        vwb_ref[...] = vw_ref[0].astype(BF16)

    q = q_ref[0]
    qs = jnp.concatenate([q[:, r * HEAD_DIM:(r + 1) * HEAD_DIM] for r in range(GROUP)], axis=0)
    qb = (qs * scale).astype(BF16)
    qa_ref[:, 0:HEAD_DIM] = qb
    t_q = i * PQ + lax.broadcasted_iota(jnp.int32, (PQ, 1), 0)
    t_rows = jnp.concatenate([t_q] * GROUP, axis=0)

    kc = kc_ref[0, 0, 0]
    n_cmp = kc.shape[0]
    s = _dot_nt(qb, kc)
    cmp_end = lax.broadcasted_iota(jnp.int32, (rows, n_cmp), 1) * CMP_STRIDE + (L_CMP - 1)
    p_cmp = _masked_softmax(s, cmp_end <= t_rows)
    o_cmp = _dot(p_cmp.astype(BF16), vc_ref[0, 0, 0])

    w0 = pl.multiple_of(jnp.clip(i * PQ - WINDOW, 0, seq - WIN_SPAN), PQ)
    s = _dot_nt(qb, kwb_ref[pl.ds(w0, WIN_SPAN), :])
    wpos = w0 + lax.broadcasted_iota(jnp.int32, (rows, WIN_SPAN), 1)
    p_win = _masked_softmax(s, (wpos <= t_rows) & (wpos > t_rows - WINDOW))
    o_win = _dot(p_win.astype(BF16), vwb_ref[pl.ds(w0, WIN_SPAN), :])

    p_sum = p_cmp[0:PQ]
    for r in range(1, GROUP):
        p_sum = p_sum + p_cmp[r * PQ:(r + 1) * PQ]
    t_lane = i * PQ + lax.broadcasted_iota(jnp.int32, (1, PQ), 1)
    sel_t = _select_blocks_t(_importance_t(p_sum.T, n_sel), t_lane)
    pen_t = jnp.concatenate([jnp.where(sel_t, 0.0, NEG), jnp.zeros((P_SEL_PAD - n_sel, PQ), F32)], axis=0)
    pen = pen_t.T.astype(BF16)
    for r in range(GROUP):
        qa_ref[r * PQ:(r + 1) * PQ, HEAD_DIM:] = pen

    m_ref[...] = jnp.full(m_ref.shape, NEG, F32)
    l_ref[...] = jnp.zeros(l_ref.shape, F32)
    acc_ref[...] = jnp.zeros(acc_ref.shape, F32)

    def slc_tile(kt, causal):
        k0 = pl.multiple_of(kt * PK, PK)
        kaug = kaug_ref[pl.ds(k0, PK), :]
        v = vsb_ref[pl.ds(k0, PK), :]
        for h in range(2):
            hs = pl.ds(h * half, half)
            s = _dot_nt(qa_ref[hs, :], kaug)
            if causal:
                kpos = k0 + lax.broadcasted_iota(jnp.int32, (half, PK), 1)
                s = jnp.where(kpos <= t_rows[h * half:(h + 1) * half], s, NEG)
            _flash_update(s, v, m_ref.at[hs], l_ref.at[hs], acc_ref.at[hs])

    n_full = (i * PQ) // PK

    def full_tile(kt, carry):
        slc_tile(kt, False)
        return carry

    lax.fori_loop(0, n_full, full_tile, 0)
    slc_tile(n_full, True)
    o_slc = acc_ref[...] / l_ref[...]

    per_group = GROUP * N_BRANCH
    gate = pltpu.roll(_sigmoid(gl_ref[0]), (LANES - pl.program_id(1) * per_group) % LANES, 1)
    for r in range(GROUP):
        rs = slice(r * PQ, (r + 1) * PQ)
        c0 = r * N_BRANCH
        o = (gate[:, c0:c0 + 1] * o_cmp[rs] + gate[:, c0 + 1:c0 + 2] * o_slc[rs]
             + gate[:, c0 + 2:c0 + 3] * o_win[rs])
        o_ref[0, :, r * HEAD_DIM:(r + 1) * HEAD_DIM] = o.astype(o_ref.dtype)


def nsa_prompt(qkv, cblk):
    b, t, _ = qkv.shape
    qcol = ATTN_W // HEAD_DIM

    def kv_spec(slot):
        return pl.BlockSpec((1, t, HEAD_DIM), lambda bi, g, i: (bi, 0, qcol + slot * N_KV + g))

    n_cmp = cblk.shape[3]
    rows = GROUP * PQ
    return pl.pallas_call(
        functools.partial(_nsa_prompt_body, seq=t),
        grid=(b, N_KV, t // PQ),
        in_specs=[pl.BlockSpec((1, PQ, GROUP * HEAD_DIM), lambda bi, g, i: (bi, i, g)),
                  kv_spec(2), kv_spec(3), kv_spec(4), kv_spec(5),
                  pl.BlockSpec((1, 1, 1, n_cmp, HEAD_DIM), lambda bi, g, i: (bi, 0, g, 0, 0)),
                  pl.BlockSpec((1, 1, 1, n_cmp, HEAD_DIM), lambda bi, g, i: (bi, 1, g, 0, 0)),
                  pl.BlockSpec((1, PQ, LANES), lambda bi, g, i: (bi, i, QKV_W // LANES))],
        out_specs=pl.BlockSpec((1, PQ, GROUP * HEAD_DIM), lambda bi, g, i: (bi, i, g)),
        out_shape=jax.ShapeDtypeStruct((b, t, ATTN_W), BF16),
        scratch_shapes=[pltpu.VMEM((t, 2 * HEAD_DIM), BF16), pltpu.VMEM((t, HEAD_DIM), BF16),
                        pltpu.VMEM((t, HEAD_DIM), BF16), pltpu.VMEM((t, HEAD_DIM), BF16),
                        pltpu.VMEM((rows, 2 * HEAD_DIM), BF16), pltpu.VMEM((rows, 1), F32), pltpu.VMEM((rows, 1), F32),
                        pltpu.VMEM((rows, HEAD_DIM), F32)],
        compiler_params=_params(("arbitrary",) * 3, 40 << 20),
    )(qkv, qkv, qkv, qkv, qkv, cblk, cblk, qkv)


SP = 8
S_KEYS = SP * PAGE_SIZE
S_BLKS = S_KEYS // L_SEL
S_CHUNK = LANES // S_BLKS
S_ROWS = GROUP * S_PAD


def _nsa_sample_body(pt_ref, *refs, t_new, n_steps, n_sel_pad):
    pages = refs[:SP]
    (q_ref, cb_ref, win_ref, gl_ref, o_ref,
     qb_ref, pen_ref, new_ref, ocw_ref, m_ref, l_ref, acc_ref) = refs[SP:]
    i = pl.program_id(1)
    scale = HEAD_DIM ** -0.5
    n_chunks = n_sel_pad // LANES
    n_win = win_ref.shape[1] // (2 * N_KV)
    tok = lax.broadcasted_iota(jnp.int32, (S_PAD, 1), 0)
    t_q = PAST_LEN + tok
    t_rows = jnp.concatenate([t_q] * GROUP, axis=0)
    qcol = ATTN_W

    @pl.when(i == 0)
    def _():
        pad = jnp.zeros((LANES - S_PAD, HEAD_DIM), F32)
        for g in range(N_KV):
            heads = [q_ref[0, :, (g * GROUP + r) * HEAD_DIM:(g * GROUP + r + 1) * HEAD_DIM]
                     for r in range(GROUP)]
            qb_ref[g] = (jnp.concatenate(heads, axis=0) * scale).astype(BF16)
            for slot in range(2, 6):
                c0 = qcol + (slot * N_KV + g) * HEAD_DIM
                new_ref[slot - 2, g] = jnp.concatenate(
                    [q_ref[0, :, c0:c0 + HEAD_DIM], pad], axis=0).astype(BF16)
        m_ref[...] = jnp.full(m_ref.shape, NEG, F32)
        l_ref[...] = jnp.zeros(l_ref.shape, F32)
        acc_ref[...] = jnp.zeros(acc_ref.shape, F32)
        p_sums = []
        for g in range(N_KV):
            qb = qb_ref[g]
            kc = cb_ref[0, 0, g]
            n_cmp = kc.shape[0]
            s = _dot_nt(qb, kc)
            cmp_end = lax.broadcasted_iota(jnp.int32, (S_ROWS, n_cmp), 1) * CMP_STRIDE + (L_CMP - 1)
            p_cmp = _masked_softmax(s, cmp_end <= t_rows)
            ocw_ref[0, g] = _dot(p_cmp.astype(BF16), cb_ref[0, 1, g])
            p_sum = p_cmp[0:S_PAD]
            for r in range(1, GROUP):
                p_sum = p_sum + p_cmp[r * S_PAD:(r + 1) * S_PAD]
            p_sums.append(p_sum)
            kw = win_ref[0, pl.ds(g, n_win, stride=2 * N_KV), :].astype(BF16)
            vw = win_ref[0, pl.ds(N_KV + g, n_win, stride=2 * N_KV), :].astype(BF16)
            s = jnp.concatenate([_dot_nt(qb, kw), _dot_nt(qb, new_ref[2, g])], axis=1)
            lane = lax.broadcasted_iota(jnp.int32, (S_ROWS, n_win + LANES), 1)
            wpos = PAST_LEN - n_win + lane
            mask = (wpos <= t_rows) & (wpos > t_rows - WINDOW) & (lane < n_win + t_new)
            p_win = _masked_softmax(s, mask).astype(BF16)
            ocw_ref[1, g] = _dot(p_win[:, 0:n_win], vw) + _dot(p_win[:, n_win:], new_ref[3, g])
        imp = _importance(jnp.concatenate(p_sums, axis=0), n_sel_pad)
        sel = _select_blocks(imp, jnp.concatenate([t_q] * N_KV, axis=0))
        pen = jnp.where(sel, 0.0, NEG).astype(BF16)
        for g in range(N_KV):
            for ch in range(n_chunks):
                blk = pen[g * S_PAD:(g + 1) * S_PAD, ch * LANES:(ch + 1) * LANES]
                pen_ref[ch, g] = jnp.concatenate([blk] * GROUP, axis=0)

    key = lax.broadcasted_iota(jnp.int32, (S_KEYS, LANES), 0)
    blk = lax.broadcasted_iota(jnp.int32, (S_KEYS, LANES), 1)
    ind = jnp.where((i % S_CHUNK) * S_BLKS + key // L_SEL == blk, 1.0, 0.0).astype(BF16)
    for g in range(N_KV):
        k = jnp.concatenate(
            [p[0, pl.ds(g, PAGE_SIZE, stride=2 * N_KV), :] for p in pages], axis=0)
        v = jnp.concatenate(
            [p[0, pl.ds(N_KV + g, PAGE_SIZE, stride=2 * N_KV), :] for p in pages], axis=0)
        kaug = jnp.concatenate([k.astype(BF16), ind], axis=1)
        qa = jnp.concatenate([qb_ref[g], pen_ref[i // S_CHUNK, g]], axis=1)
        _flash_update(_dot_nt(qa, kaug), v.astype(BF16), m_ref.at[g], l_ref.at[g], acc_ref.at[g])

    @pl.when(i == n_steps - 1)
    def _():
        gate = _sigmoid(gl_ref[0])
        lane = lax.broadcasted_iota(jnp.int32, (S_ROWS, LANES), 1)
        cur_chunk, cur_lane = (PAST_LEN // L_SEL) // LANES, (PAST_LEN // L_SEL) % LANES
        for g in range(N_KV):
            pen_cur = pen_ref[cur_chunk, g][:, cur_lane:cur_lane + 1].astype(F32)
            s = _dot_nt(qb_ref[g], new_ref[0, g]) + pen_cur
            s = jnp.where((PAST_LEN + lane <= t_rows) & (lane < t_new), s, NEG)
            _flash_update(s, new_ref[1, g], m_ref.at[g], l_ref.at[g], acc_ref.at[g])
            o_slc = acc_ref[g] / l_ref[g]
            for r in range(GROUP):
                rs = slice(r * S_PAD, (r + 1) * S_PAD)
                c0 = (g * GROUP + r) * N_BRANCH
                o = (gate[:, c0:c0 + 1] * ocw_ref[0, g][rs] + gate[:, c0 + 1:c0 + 2] * o_slc[rs]
                     + gate[:, c0 + 2:c0 + 3] * ocw_ref[1, g][rs])
                h = g * GROUP + r
                o_ref[0, :, h * HEAD_DIM:(h + 1) * HEAD_DIM] = o.astype(o_ref.dtype)


def nsa_sample(qkv, t_new, cblk, cache_slc, cache_win, page_table, gl):
    b = qkv.shape[0]
    n_pages = page_table.shape[1]
    n_steps = n_pages // SP
    n_cmp = cblk.shape[3]
    n_win = cache_win.shape[1] // (2 * N_KV)
    n_sel = -(-(PAST_LEN + t_new) // L_SEL)
    n_sel_pad = -(-n_sel // LANES) * LANES
    assert n_steps * S_BLKS <= n_sel_pad and n_steps * SP == n_pages

    def page_spec(c):
        return pl.BlockSpec((1, PAGE_SIZE * 2 * N_KV, HEAD_DIM),
                            lambda bi, i, pt: (pt[bi, i * SP + c], 0, 0))

    grid_spec = pltpu.PrefetchScalarGridSpec(
        num_scalar_prefetch=1, grid=(b, n_steps),
        in_specs=[page_spec(c) for c in range(SP)] + [
            pl.BlockSpec((1, S_PAD, QKV_W), lambda bi, i, pt: (bi, 0, 0)),
            pl.BlockSpec((1, 2, N_KV, n_cmp, HEAD_DIM), lambda bi, i, pt: (bi, 0, 0, 0, 0)),
            pl.BlockSpec((1, n_win * 2 * N_KV, HEAD_DIM), lambda bi, i, pt: (bi, 0, 0)),
            pl.BlockSpec((1, S_PAD, LANES), lambda bi, i, pt: (bi, 0, 0))],
        out_specs=pl.BlockSpec((1, S_PAD, ATTN_W), lambda bi, i, pt: (bi, 0, 0)),
        scratch_shapes=[pltpu.VMEM((N_KV, S_ROWS, HEAD_DIM), BF16),
                        pltpu.VMEM((n_sel_pad // LANES, N_KV, S_ROWS, LANES), BF16),
                        pltpu.VMEM((4, N_KV, LANES, HEAD_DIM), BF16),
                        pltpu.VMEM((2, N_KV, S_ROWS, HEAD_DIM), F32),
                        pltpu.VMEM((N_KV, S_ROWS, 1), F32), pltpu.VMEM((N_KV, S_ROWS, 1), F32),
                        pltpu.VMEM((N_KV, S_ROWS, HEAD_DIM), F32)])
    return pl.pallas_call(
        functools.partial(_nsa_sample_body, t_new=t_new, n_steps=n_steps, n_sel_pad=n_sel_pad),
        grid_spec=grid_spec,
        out_shape=jax.ShapeDtypeStruct((b, S_PAD, ATTN_W), F32),
        compiler_params=_params(("arbitrary", "arbitrary"), 40 << 20),
    )(page_table, *([cache_slc] * SP), qkv, cblk, cache_win, gl)


def _kv_rows_body(x_ref, o_ref):
    tt = x_ref.shape[1]
    for kvg in range(2 * N_KV):
        o_ref[0, pl.ds(kvg, tt, stride=2 * N_KV), :] = x_ref[0, :, kvg * HEAD_DIM:(kvg + 1) * HEAD_DIM]


def kv_rows(qkv, slot, t0, tt):
    b, t, _ = qkv.shape
    n = (t - t0) // tt
    assert n * tt == t - t0 and t0 % tt == 0
    col_blk = (ATTN_W + slot * KV_W) // PAGE_ROW_W
    out = pl.pallas_call(
        _kv_rows_body, grid=(b, n),
        in_specs=[pl.BlockSpec((1, tt, PAGE_ROW_W), lambda bi, i: (bi, t0 // tt + i, col_blk))],
        out_specs=pl.BlockSpec((1, tt * 2 * N_KV, HEAD_DIM), lambda bi, i: (bi, i, 0)),
        out_shape=jax.ShapeDtypeStruct((b, (t - t0) * 2 * N_KV, HEAD_DIM), F32),
        compiler_params=_params(("arbitrary", "arbitrary"), 6 * tt * PAGE_ROW_W * 4),
    )(qkv)
    return out.reshape(b, t - t0, 2, N_KV, HEAD_DIM)


def _pad_rows(x, front, total):
    return jnp.pad(x, ((0, 0), (front, total - front - x.shape[1]), (0, 0)))


def kernel(x_prompt, x_sample, cache_cmp_kv, cache_slc_kv, cache_win_kv, state_conf_conv,
           state_short_conv, page_table, p_prompt, p_sample, g_mix0, w_in0, conv_w0, conv_b0,
           ln_g0, ln_b0, cmp_pe, cmp_w1, cmp_b1, cmp_w2, w_out0, g_mix1, w_in1, sconv_w1, w_out1,
           g_ffn, w_ffn_gate, w_ffn_up, w_ffn_down, g_ple, w_ple_gate, w_ple_proj, g_final):
    bp, seq, d = x_prompt.shape
    bs, t_new, _ = x_sample.shape
    mp, ms = bp * seq, bs * t_new
    d_ff = w_ffn_gate.shape[-1]
    n_pool = cache_cmp_kv.shape[1]
    tm = 1024

    xp = x_prompt.reshape(mp, d)
    xs = x_sample.reshape(ms, d)
    rope_p = rope_tables(jnp.arange(seq, dtype=jnp.int32))
    rope_s = tuple(jnp.tile(tb, (bs, 1)) for tb in rope_tables(PAST_LEN + jnp.arange(t_new, dtype=jnp.int32)))
    outs = {}
    w_down_bf16 = w_ffn_down.astype(BF16)

    def residual(groups, x, tm_, tn_, single_buffer_rows=False):
        res_p, res_s = fused_mm(groups, _ep_residual, [F32], d, tm_, tn_, tilex=[x], m_outer=True,
                                emit_bf16=True, emit_ssq=True, single_buffer_rows=single_buffer_rows)
        return tuple(zip(res_p, res_s))

    def normed(stream, gain, ws):
        (_, _), (b_p, b_s), ssq = stream
        return Lhs(b_p, b_s, ws, gain=gain, ssq=ssq)

    def ffn_ple(stream, i, last):
        (gp,), (gs,) = fused_mm([normed(stream, g_ffn[i], [W(w_ffn_gate, i), W(w_ffn_up, i)])],
                                _ep_swiglu, [BF16], d_ff, tm, 256)
        stream = residual([Lhs(gp, gs, [W(w_down_bf16, i)])], stream[0], tm, 256, single_buffer_rows=True)
        pp = p_prompt[i].reshape(mp, -1).astype(BF16)
        ps = p_sample[i].reshape(ms, -1).astype(BF16)
        res_p, res_s = fused_mm([normed(stream, g_ple[i], [W(w_ple_gate, i)]), Lhs(pp, ps, [W(w_ple_proj, i)])],
                                _ep_ple, [F32], d, tm, 512, tilex=[stream[0]], m_outer=True,
                                emit_bf16=not last, emit_ssq=not last)
        return tuple(zip(res_p, res_s))

    e = 0
    hp, hs = rmsnorm(xp, g_mix0[e], BF16), rmsnorm(xs, g_mix0[e], BF16)
    w_in0_t = jnp.swapaxes(w_in0, 1, 2)
    win = lambda col_off: W(w_in0_t, e, col_off=col_off, transposed=True)
    (up,), (us,) = fused_mm([Lhs(hp, hs, [win(0), win(CONF_CH // 256)])], _ep_glu,
                            [F32], CONF_CH, tm, 256)
    (qkv_p,), (qkv_s,) = fused_mm([Lhs(hp, hs, [win(2 * CONF_CH // KV_W)])], _ep_rope,
                                  [F32], QKVG_W, tm, KV_W, rowx=list(zip(rope_p, rope_s)))

    up3, us3 = up.reshape(bp, seq, CONF_CH), us.reshape(bs, t_new, CONF_CH)
    conf_w = (conv_w0[e], conv_b0[e], ln_g0[e], ln_b0[e])
    a_p = conf_conv(up3, jnp.zeros((bp, HALO, CONF_CH), F32), *conf_w, tt=128, out_dtype=BF16)
    st = state_conf_conv[e]
    a_s = conf_conv(_pad_rows(us3, 0, S_PAD), _pad_rows(st, HALO - st.shape[1], HALO), *conf_w,
                    tt=S_PAD, out_dtype=F32)
    outs["conf_p"] = up3[:, seq - (CONF_W - 1):]
    outs["conf_s"] = jnp.concatenate([st, us3], axis=1)[:, t_new:]

    qkv_p3, qkv_s3 = qkv_p.reshape(bp, seq, QKVG_W), qkv_s.reshape(bs, t_new, QKVG_W)
    kv_shape = lambda x: x.reshape(x.shape[0], x.shape[1], 2, N_KV, HEAD_DIM)
    for name, slot in (("cmp", 0), ("slc", 2), ("win", 4)):
        c0 = ATTN_W + slot * KV_W
        t0 = seq - min(WINDOW, seq) if name == "win" else 0
        outs[name + "_p"] = kv_rows(qkv_p3, slot, t0, 512)
        outs[name + "_s"] = kv_shape(qkv_s3[:, :, c0:c0 + 2 * KV_W])
    outs["win_s"] = jnp.concatenate([cache_win_kv[e], outs["win_s"]], axis=1)[:, t_new:]

    pages_p = seq // PAGE_SIZE
    ident = jnp.arange(bp * pages_p, dtype=jnp.int32).reshape(bp, pages_p)
    w1cat = cmp_w1_cat(cmp_w1[e])
    proj_p = cmp_project(qkv_p.reshape(bp * pages_p, PAGE_SIZE, QKVG_W), False, ATTN_W // PAGE_ROW_W,
                         ident, w1cat)
    row_view = lambda c: c.reshape(-1, c.shape[-4] * 2 * N_KV, HEAD_DIM)
    pool_pages = page_table + e * n_pool
    proj_s = cmp_project(row_view(cache_cmp_kv), True, 0, pool_pages, w1cat)
    cblk_p = cmp_finish(proj_p, cmp_pe[e], w1cat, cmp_b1[e], cmp_w2[e])
    cblk_s = cmp_finish(proj_s, cmp_pe[e], w1cat, cmp_b1[e], cmp_w2[e])

    attn_p = nsa_prompt(qkv_p3, cblk_p)
    gl_s3 = jnp.pad(qkv_s3[:, :, QKV_W:QKV_W + LANES], ((0, 0), (0, S_PAD - t_new), (0, 0)))
    attn_s = nsa_sample(_pad_rows(qkv_s3[:, :, :QKV_W], 0, S_PAD), t_new, cblk_s, row_view(cache_slc_kv),
                        row_view(cache_win_kv[e]), pool_pages, gl_s3)
    a_p2, a_s2 = a_p.reshape(mp, CONF_CH), a_s[:, :t_new].reshape(ms, CONF_CH).astype(BF16)
    at_p2, at_s2 = attn_p.reshape(mp, ATTN_W), attn_s[:, :t_new].reshape(ms, ATTN_W).astype(BF16)
    stream = residual(
        [Lhs(a_p2, a_s2, [W(w_out0, e, k_rows=CONF_CH, row_blk=0)]),
         Lhs(at_p2, at_s2, [W(w_out0, e, k_rows=ATTN_W, row_blk=CONF_CH // ATTN_W)])],
        (xp, xs), tm, 512)
    stream = ffn_ple(stream, 0, last=False)

    o = 0
    (bg_p, cv_p), (bg_s, cv_s) = fused_mm(
        [normed(stream, g_mix1[o], [W(w_in1, o), W(w_in1, o, col_off=SC_CH // 256),
                                    W(w_in1, o, col_off=2 * SC_CH // 256)])],
        _ep_shortconv, [F32, F32], SC_CH, 512, 256)
    cv_p3, cv_s3 = cv_p.reshape(bp, seq, SC_CH), cv_s.reshape(bs, t_new, SC_CH)
    y_p = short_conv(cv_p3, bg_p.reshape(bp, seq, SC_CH), jnp.zeros((bp, SUBLANES, SC_CH), F32),
                     sconv_w1[o], tt=256, out_dtype=BF16)
    st = state_short_conv[o]
    y_s = short_conv(_pad_rows(cv_s3, 0, S_PAD), _pad_rows(bg_s.reshape(bs, t_new, SC_CH), 0, S_PAD),
                     _pad_rows(st, SUBLANES - st.shape[1], SUBLANES), sconv_w1[o], tt=S_PAD,
                     out_dtype=F32)
    outs["sc_p"] = cv_p3[:, seq - (SC_W - 1):]
    outs["sc_s"] = jnp.concatenate([st, cv_s3], axis=1)[:, t_new:]
    stream = residual(
        [Lhs(y_p.reshape(mp, SC_CH), y_s[:, :t_new].reshape(ms, SC_CH).astype(BF16), [W(w_out1, o)])],
        stream[0], tm, 512)
    (xp, xs), = ffn_ple(stream, 1, last=True)

    y_p = rmsnorm(xp, g_final, F32).reshape(bp, seq, d)
    y_s = rmsnorm(xs, g_final, F32).reshape(bs, t_new, d)
    st1 = lambda x: x[None]
    return (y_p, y_s, st1(outs["cmp_p"]), st1(outs["cmp_s"]), st1(outs["slc_p"]), st1(outs["slc_s"]),
            st1(outs["win_p"]), st1(outs["win_s"]), st1(outs["conf_p"]), st1(outs["conf_s"]),
            st1(outs["sc_p"]), st1(outs["sc_s"]))
```

```python
import functools

import jax
import jax.numpy as jnp
from jax import lax
from jax.experimental import pallas as pl
from jax.experimental.pallas import tpu as pltpu

F32 = jnp.float32
BF16 = jnp.bfloat16

V7X_VMEM_BYTES = 64 * 1024 * 1024
LANES = 128
SUBLANES = 8

D_MODEL = 4096
PAST_LEN = 16384
PAGE_SIZE = 128
N_HEADS = 16
HEAD_DIM = 128
N_KV = 4
GROUP = N_HEADS // N_KV
ATTN_W = N_HEADS * HEAD_DIM
KV_W = N_KV * HEAD_DIM
ROPE_DIM = HEAD_DIM // 4
ROPE_THETA = 500000.0
L_CMP = 32
CMP_STRIDE = 16
L_SEL = 64
N_SEL = 16
WINDOW = 512
N_BRANCH = 3
CONF_CH = D_MODEL // 2
CONF_W = 31
SC_CH = D_MODEL
SC_W = 3
RMS_EPS = 1e-6
LN_EPS = 1e-5
NEG = -1e30
FORCE = 1e9
QKV_W = ATTN_W + 6 * KV_W
QKVG_W = QKV_W + KV_W
SUB_PER_PAGE = PAGE_SIZE // CMP_STRIDE
SUB_FLAT = CMP_STRIDE * HEAD_DIM
PAGE_ROW_W = 2 * KV_W
HALO = 32
S_PAD = 8


def _vmem_limit(n_bytes):
    return int(min(V7X_VMEM_BYTES - (6 << 20), max(n_bytes, 16 << 20)))


def _params(sem, vmem):
    return pltpu.CompilerParams(dimension_semantics=sem, vmem_limit_bytes=_vmem_limit(vmem))


def _dot(a, b):
    return jnp.dot(a, b, preferred_element_type=F32)


def _dot_nt(a, b):
    return lax.dot_general(a, b, (((1,), (1,)), ((), ())), preferred_element_type=F32)


def _sigmoid(x):
    return jax.nn.sigmoid(x)


def _rmsnorm_body(x_ref, g_ref, o_ref):
    x = x_ref[...]
    y = x * lax.rsqrt(jnp.mean(x * x, axis=-1, keepdims=True) + RMS_EPS)
    o_ref[...] = (y * g_ref[...]).astype(o_ref.dtype)


def rmsnorm(x, g, out_dtype, rows=256):
    m, d = x.shape
    tr = min(rows, m)
    return pl.pallas_call(
        _rmsnorm_body,
        grid=(m // tr,),
        in_specs=[pl.BlockSpec((tr, d), lambda i: (i, 0)),
                  pl.BlockSpec((1, d), lambda i: (0, 0))],
        out_specs=pl.BlockSpec((tr, d), lambda i: (i, 0)),
        out_shape=jax.ShapeDtypeStruct((m, d), out_dtype),
        compiler_params=_params(("arbitrary",), 6 * tr * d * 4),
    )(x, g.reshape(1, d))


class W:
    def __init__(self, arr, layer=0, k_rows=None, row_blk=0, col_off=0, transposed=False):
        self.arr = arr
        self.layer = layer
        self.k_rows = arr.shape[2 if transposed else 1] if k_rows is None else k_rows
        self.row_blk = row_blk
        self.col_off = col_off
        self.transposed = transposed


class Lhs:
    def __init__(self, a_p, a_s, ws, gain=None, ssq=None):
        self.a_p, self.a_s, self.ws, self.gain, self.ssq = a_p, a_s, ws, gain, ssq


def _fold_lanes(x):
    out = x[:, 0:LANES]
    for c in range(1, x.shape[1] // LANES):
        out = out + x[:, c * LANES:(c + 1) * LANES]
    return out


def _mm_body(*refs, meta, n_rx, n_tx, n_main, emit_bf16, emit_ssq, epilogue, m_axis, n_j):
    it = iter(refs)
    groups = []
    for w_transposed, has_gain, has_ssq in meta:
        a = (next(it), next(it))
        gain = next(it) if has_gain else None
        ssq = (next(it), next(it)) if has_ssq else None
        groups.append((a, gain, ssq, [(next(it), t) for t in w_transposed]))
    rx = [[next(it) for _ in range(n_rx)] for _ in range(2)]
    tx = [[next(it) for _ in range(n_tx)] for _ in range(2)]
    n_out = n_main + emit_bf16 + emit_ssq
    outs = [[next(it) for _ in range(n_out)] for _ in range(2)]
    cache = list(it)
    i = pl.program_id(m_axis)
    j = pl.program_id(1 - m_axis)

    def prepared(w, gain):
        return (w[...] if gain is None else w[...] * gain[...]).astype(BF16)

    if cache:
        slots = iter(cache)
        wb = [[(next(slots), t) for _, t in ws] for _, _, _, ws in groups]

        @pl.when(i == 0)
        def _():
            for (_, gain, _, ws), cached in zip(groups, wb):
                for (w, _), (c, _) in zip(ws, cached):
                    c[...] = prepared(w, gain)
    else:
        wb = [[(prepared(w, gain), t) for w, t in ws] for _, gain, _, ws in groups]

    def run(which):
        dots = []
        for (a, _, ssq, _), wbs in zip(groups, wb):
            lhs = a[which][...]
            scale = None if ssq is None else ssq[which][...]
            for w, transposed in wbs:
                w = w[...]
                d = _dot_nt(lhs, w) if transposed else _dot(lhs, w)
                if scale is not None:
                    d = d * jnp.concatenate([scale] * (d.shape[1] // LANES), axis=1)
                dots.append(d)
        res = list(epilogue(dots, [r[...] for r in rx[which]], [t[...] for t in tx[which]], j))
        if emit_bf16:
            res.append(res[0])
        for o, r in zip(outs[which], res):
            o[...] = r.astype(o.dtype)
        if emit_ssq:
            part = _fold_lanes(res[0] * res[0])
            acc = outs[which][-1]

            @pl.when(j == 0)
            def _():
                acc[...] = part

            @pl.when(j > 0)
            def _():
                acc[...] += part

            @pl.when(j == n_j - 1)
            def _():
                total = jnp.sum(acc[...], axis=-1, keepdims=True)
                acc[...] = jnp.broadcast_to(lax.rsqrt(total / (n_j * res[0].shape[1]) + RMS_EPS), acc.shape)

    run(0)

    @pl.when(i == 0)
    def _():
        run(1)

    if m_axis == 0:
        @pl.when(i == 1)
        def _():
            for o in outs[1]:
                o[...] = jnp.zeros(o.shape, o.dtype)


def fused_mm(groups, epilogue, out_dtypes, n_cols, tm, tn, rowx=(), tilex=(), m_outer=False,
             emit_bf16=False, emit_ssq=False, single_buffer_rows=False):
    mp = groups[0].a_p.shape[0]
    ms = groups[0].a_s.shape[0]
    nj, ni = n_cols // tn, mp // tm
    assert nj * tn == n_cols and ni * tm == mp
    m_axis = 0 if m_outer else 1
    assert m_outer or not emit_ssq
    spare = 1 if m_outer else 0

    def spec(shape, fn, **kw):
        return pl.BlockSpec(shape, lambda *g: fn(g[m_axis], g[1 - m_axis]), **kw)

    assert m_outer or not single_buffer_rows
    rows_kw = dict(pipeline_mode=pl.Buffered(1)) if single_buffer_rows else {}
    args, in_specs, meta = [], [], []
    vmem = 0
    for g in groups:
        k = g.a_p.shape[1]
        args += [g.a_p, g.a_s]
        in_specs += [spec((tm, k), lambda i, j: (i, 0), **rows_kw), spec((ms, k), lambda i, j: (0, 0))]
        vmem += ((1 if single_buffer_rows else 2) * tm + 2 * ms) * k * g.a_p.dtype.itemsize
        if g.gain is not None:
            args.append(g.gain.reshape(k, 1))
            in_specs.append(spec((k, 1), lambda i, j: (0, 0)))
            vmem += 2 * k * LANES * 4
        if g.ssq is not None:
            args += list(g.ssq)
            in_specs += [spec((tm, g.ssq[0].shape[1]), lambda i, j: (i, 0)),
                         spec(g.ssq[1].shape, lambda i, j: (0, 0))]
            vmem += 2 * (tm + ms) * g.ssq[0].shape[1] * 4
        for w in g.ws:
            assert w.k_rows == k and not (w.transposed and g.gain is not None)
            args.append(w.arr)
            if w.transposed:
                in_specs.append(spec((None, tn, k), functools.partial(
                    lambda i, j, la, rb, co: (la, co + j, rb), la=w.layer, rb=w.row_blk, co=w.col_off)))
            else:
                in_specs.append(spec((None, k, tn), functools.partial(
                    lambda i, j, la, rb, co: (la, rb, co + j), la=w.layer, rb=w.row_blk, co=w.col_off)))
            vmem += k * tn * (2 * w.arr.dtype.itemsize + 2)
        meta.append((tuple(w.transposed for w in g.ws), g.gain is not None, g.ssq is not None))
    for which in (0, 1):
        for tab_p, tab_s in rowx:
            if which == 0:
                per = tab_p.shape[0] // tm
                args.append(tab_p)
                in_specs.append(spec((tm, tab_p.shape[1]),
                                     functools.partial(lambda i, j, per: (i % per, 0), per=per)))
                vmem += 2 * tm * tab_p.shape[1] * 4
            else:
                args.append(tab_s)
                in_specs.append(spec(tab_s.shape, lambda i, j: (0, 0)))
    for which in (0, 1):
        for t_p, t_s in tilex:
            if which == 0:
                args.append(t_p)
                in_specs.append(spec((tm, tn), lambda i, j: (i, j)))
            else:
                args.append(t_s)
                in_specs.append(spec((ms, tn), lambda i, j: (0, j)))
    kinds = [(dt, tn, nj) for dt in out_dtypes]
    if emit_bf16:
        kinds.append((BF16, tn, nj))
    if emit_ssq:
        kinds.append((F32, LANES, 1))
    out_shape, out_specs = [], []
    for which in (0, 1):
        for dt, width, nblk in kinds:
            col = (lambda i, j: j) if nblk > 1 else (lambda i, j: 0)
            if which == 0:
                out_shape.append(jax.ShapeDtypeStruct((mp, nblk * width), dt))
                out_specs.append(spec((tm, width), functools.partial(lambda i, j, col: (i, col(i, j)), col=col)))
            else:
                out_shape.append(jax.ShapeDtypeStruct((ms, (nblk + spare) * width), dt))
                out_specs.append(spec((ms, width), functools.partial(
                    lambda i, j, col, nblk: (0, jnp.where(i == 0, col(i, j), nblk) if m_outer else col(i, j)),
                    col=col, nblk=nblk)))
    n_dots = sum(len(g.ws) for g in groups)
    vmem += (2 * (len(tilex) + len(kinds)) + n_dots + 2) * tm * tn * 4
    body = functools.partial(
        _mm_body, meta=tuple(meta), n_rx=len(rowx), n_tx=len(tilex), n_main=len(out_dtypes),
        emit_bf16=emit_bf16, emit_ssq=emit_ssq, epilogue=epilogue, m_axis=m_axis, n_j=nj)
    scratch = []
    if not m_outer and any(g.gain is not None for g in groups):
        scratch = [pltpu.VMEM((tn, w.k_rows) if w.transposed else (w.k_rows, tn), BF16)
                   for g in groups for w in g.ws]
    res = pl.pallas_call(
        body, grid=(ni, nj) if m_outer else (nj, ni), in_specs=in_specs, out_specs=out_specs,
        out_shape=out_shape, scratch_shapes=scratch,
        compiler_params=_params(("arbitrary", "arbitrary"), vmem + (6 << 20)),
    )(*args)
    n = len(kinds)
    sample = [r[:, :nblk * width] for r, (_, width, nblk) in zip(res[n:], kinds)]
    return res[:n], sample


def _ep_glu(dots, rx, tx, j):
    return [dots[0] * _sigmoid(dots[1])]


def _ep_swiglu(dots, rx, tx, j):
    return [jax.nn.silu(dots[0]) * dots[1]]


def _ep_residual(dots, rx, tx, j):
    return [tx[0] + sum(dots[1:], dots[0])]


def _ep_ple(dots, rx, tx, j):
    return [tx[0] + _sigmoid(dots[0]) * dots[1]]


def _ep_shortconv(dots, rx, tx, j):
    return [dots[0], dots[1] * dots[2]]


def _ep_rope(dots, rx, tx, j):
    z = dots[0]
    cos, sin_lo, sin_hi = rx
    half = ROPE_DIM // 2
    heads = []
    for h in range(z.shape[1] // HEAD_DIM):
        x = z[:, h * HEAD_DIM:(h + 1) * HEAD_DIM]
        heads.append(x * cos + pltpu.roll(x, half, 1) * sin_hi
                     + pltpu.roll(x, HEAD_DIM - half, 1) * sin_lo)
    roped = jnp.concatenate(heads, axis=1)
    slot = j - ATTN_W // z.shape[1]
    is_v = jnp.logical_and(slot >= 0, slot % 2 == 1)
    lane = lax.broadcasted_iota(jnp.int32, z.shape, 1)
    gates = jnp.where(lane < N_HEADS * N_BRANCH, z, 0.0)
    return [jnp.where(slot == 6, gates, jnp.where(is_v, z, roped))]


def rope_tables(pos):
    half = ROPE_DIM // 2
    inv = ROPE_THETA ** (-2.0 * jnp.arange(half, dtype=F32) / ROPE_DIM)
    ang = pos.astype(F32)[:, None] * inv[None, :]
    cos, sin = jnp.cos(ang), jnp.sin(ang)
    n = pos.shape[0]
    rest = HEAD_DIM - ROPE_DIM
    c = jnp.concatenate([cos, cos, jnp.ones((n, rest), F32)], axis=1)
    s_lo = jnp.concatenate([-sin, jnp.zeros((n, half + rest), F32)], axis=1)
    s_hi = jnp.concatenate([jnp.zeros((n, half), F32), sin, jnp.zeros((n, rest), F32)], axis=1)
    return c, s_lo, s_hi


def _conf_core(xw_ref, cw_ref, cb_ref, g_ref, b_ref, cbuf_ref, o_ref, tt):
    last = CONF_W - 1
    for c in range(CONF_CH // LANES):
        cs = slice(c * LANES, (c + 1) * LANES)
        acc = None
        for r in range(SUBLANES):
            y = None
            for a in range((last - r) // SUBLANES + 1):
                lo = HALO - SUBLANES * (a + 1)
                term = xw_ref[lo:lo + tt + SUBLANES, cs] * cw_ref[last - SUBLANES * a - r:last - SUBLANES * a - r + 1, cs]
                y = term if y is None else y + term
            part = y[SUBLANES - r:SUBLANES - r + tt]
            acc = part if acc is None else acc + part
        cbuf_ref[:, cs] = acc
    c = cbuf_ref[...] + cb_ref[...]
    mu = jnp.mean(c, axis=-1, keepdims=True)
    var = jnp.mean(jnp.square(c - mu), axis=-1, keepdims=True)
    y = (c - mu) * lax.rsqrt(var + LN_EPS) * g_ref[...] + b_ref[...]
    o_ref[0] = jax.nn.silu(y).astype(o_ref.dtype)


def _conf_body(prev_ref, halo_ref, x_ref, cw_ref, cb_ref, g_ref, b_ref, o_ref, xw_ref, cbuf_ref, *, tt):
    first = pl.program_id(1) == 0
    xw_ref[0:HALO] = jnp.where(first, prev_ref[0], halo_ref[0])
    xw_ref[HALO:HALO + tt] = x_ref[0]
    _conf_core(xw_ref, cw_ref, cb_ref, g_ref, b_ref, cbuf_ref, o_ref, tt)


def conf_conv(u, prev, cw, cb, ln_g, ln_b, tt, out_dtype):
    b, t, c = u.shape
    if tt < HALO:
        assert t == tt
        halo_spec = pl.BlockSpec((1, HALO, c), lambda bi, i: (bi, 0, 0))
        halo_arr = prev
    else:
        hb = tt // HALO
        halo_spec = pl.BlockSpec((1, HALO, c), lambda bi, i: (bi, jnp.maximum(i * hb - 1, 0), 0))
        halo_arr = u
    cwp = jnp.pad(cw, ((0, HALO - cw.shape[0]), (0, 0)))
    row = lambda v: v.reshape(1, c)
    const = lambda bi, i: (0, 0)
    return pl.pallas_call(
        functools.partial(_conf_body, tt=tt),
        grid=(b, t // tt),
        in_specs=[pl.BlockSpec((1, HALO, c), lambda bi, i: (bi, 0, 0)), halo_spec,
                  pl.BlockSpec((1, tt, c), lambda bi, i: (bi, i, 0)),
                  pl.BlockSpec((HALO, c), const), pl.BlockSpec((1, c), const),
                  pl.BlockSpec((1, c), const), pl.BlockSpec((1, c), const)],
        out_specs=pl.BlockSpec((1, tt, c), lambda bi, i: (bi, i, 0)),
        out_shape=jax.ShapeDtypeStruct((b, t, c), out_dtype),
        scratch_shapes=[pltpu.VMEM((tt + HALO, c), F32), pltpu.VMEM((tt, c), F32)],
        compiler_params=_params(("arbitrary", "arbitrary"), 10 * (tt + HALO) * c * 4),
    )(prev, halo_arr, u, cwp, row(cb), row(ln_g), row(ln_b))


def _short_body(prev_ref, halo_ref, cv_ref, bg_ref, w_ref, o_ref, xw_ref, *, tt):
    first = pl.program_id(1) == 0
    xw_ref[0:SUBLANES] = jnp.where(first, prev_ref[0], halo_ref[0])
    xw_ref[SUBLANES:SUBLANES + tt] = cv_ref[0]
    base = SUBLANES - (SC_W - 1)
    conv = xw_ref[pl.ds(base, tt), :] * w_ref[0:1, :]
    for k in range(1, SC_W):
        conv = conv + xw_ref[pl.ds(base + k, tt), :] * w_ref[k:k + 1, :]
    o_ref[0] = (bg_ref[0] * conv).astype(o_ref.dtype)


def short_conv(cv, bg, prev, w, tt, out_dtype):
    b, t, c = cv.shape
    hb = tt // SUBLANES
    wp = jnp.pad(w, ((0, SUBLANES - w.shape[0]), (0, 0)))
    return pl.pallas_call(
        functools.partial(_short_body, tt=tt),
        grid=(b, t // tt),
        in_specs=[pl.BlockSpec((1, SUBLANES, c), lambda bi, i: (bi, 0, 0)),
                  pl.BlockSpec((1, SUBLANES, c), lambda bi, i: (bi, jnp.maximum(i * hb - 1, 0), 0)),
                  pl.BlockSpec((1, tt, c), lambda bi, i: (bi, i, 0)),
                  pl.BlockSpec((1, tt, c), lambda bi, i: (bi, i, 0)),
                  pl.BlockSpec((SUBLANES, c), lambda bi, i: (0, 0))],
        out_specs=pl.BlockSpec((1, tt, c), lambda bi, i: (bi, i, 0)),
        out_shape=jax.ShapeDtypeStruct((b, t, c), out_dtype),
        scratch_shapes=[pltpu.VMEM((tt + SUBLANES, c), F32)],
        compiler_params=_params(("arbitrary", "arbitrary"), 10 * (tt + SUBLANES) * c * 4),
    )(prev, cv, cv, bg, wp)


CMP_PAGES = 8
SUB_PITCH = 24


def _cmp_proj_body(pt_ref, *refs, row_major):
    pages = refs[:CMP_PAGES]
    w1_ref, p_ref, kbuf_ref, x_ref = refs[CMP_PAGES:]
    rows = CMP_PAGES * SUB_PER_PAGE
    for kv in range(2):
        for g in range(N_KV):
            kvg = kv * N_KV + g
            for c in range(CMP_PAGES):
                if row_major:
                    tok = pages[c][0, pl.ds(kvg, PAGE_SIZE, stride=2 * N_KV), :]
                else:
                    tok = pages[c][0, :, kvg * HEAD_DIM:(kvg + 1) * HEAD_DIM]
                for m in range(SUB_PER_PAGE):
                    r0 = (c * SUB_PER_PAGE + m) * SUB_PITCH
                    kbuf_ref[g, r0:r0 + CMP_STRIDE, :] = tok[m * CMP_STRIDE:(m + 1) * CMP_STRIDE]
            for s in range(CMP_STRIDE):
                x_ref[g * rows:(g + 1) * rows, s * HEAD_DIM:(s + 1) * HEAD_DIM] = (
                    kbuf_ref[g, pl.ds(s, rows, stride=SUB_PITCH), :])
        p = _dot(x_ref[...].astype(BF16), w1_ref[kv])
        for g in range(N_KV):
            p_ref[0, kv, g] = p[g * rows:(g + 1) * rows]


def cmp_w1_cat(w1):
    w = w1.reshape(2, 2, SUB_FLAT, HEAD_DIM).transpose(0, 2, 1, 3)
    return w.reshape(2, SUB_FLAT, 2 * HEAD_DIM).astype(BF16)


def cmp_project(pages_arr, row_major, col_blk, page_table, w1cat):
    b, n_pages = page_table.shape
    n_sub = n_pages * SUB_PER_PAGE
    steps = n_pages // CMP_PAGES
    assert steps * CMP_PAGES == n_pages
    blk = (1, PAGE_SIZE * 2 * N_KV, HEAD_DIM) if row_major else (1, PAGE_SIZE, PAGE_ROW_W)

    def page_spec(c):
        return pl.BlockSpec(blk, lambda bi, i, pt: (pt[bi, i * CMP_PAGES + c], 0, col_blk))

    rows = CMP_PAGES * SUB_PER_PAGE
    grid_spec = pltpu.PrefetchScalarGridSpec(
        num_scalar_prefetch=1, grid=(b, steps),
        in_specs=[page_spec(c) for c in range(CMP_PAGES)] + [
            pl.BlockSpec((2, SUB_FLAT, 2 * HEAD_DIM), lambda bi, i, pt: (0, 0, 0))],
        out_specs=pl.BlockSpec((1, 2, N_KV, rows, 2 * HEAD_DIM), lambda bi, i, pt: (bi, 0, 0, i, 0)),
        scratch_shapes=[pltpu.VMEM((N_KV, rows * SUB_PITCH, HEAD_DIM), F32),
                        pltpu.VMEM((N_KV * rows, SUB_FLAT), F32)])
    return pl.pallas_call(
        functools.partial(_cmp_proj_body, row_major=row_major), grid_spec=grid_spec,
        out_shape=jax.ShapeDtypeStruct((b, 2, N_KV, n_sub, 2 * HEAD_DIM), F32),
        compiler_params=_params(("arbitrary", "arbitrary"), 32 << 20),
    )(page_table, *([pages_arr] * CMP_PAGES), w1cat)


def _cmp_finish_body(p_ref, pe_ref, w1_ref, b1_ref, w2_ref, o_ref):
    n_sub = p_ref.shape[-2]
    w1 = w1_ref[0]
    pe_term = (_dot(pe_ref[0, 0].astype(BF16), w1)[0:1, 0:HEAD_DIM]
               + _dot(pe_ref[0, 1].astype(BF16), w1)[0:1, HEAD_DIM:])
    p = p_ref[0, 0, 0]
    p1_next = pltpu.roll(p[:, HEAD_DIM:], n_sub - 1, 0)
    pre = p[:, 0:HEAD_DIM] + p1_next + (b1_ref[0] + pe_term)
    o_ref[0, 0, 0] = _dot(jax.nn.gelu(pre).astype(BF16), w2_ref[0].astype(BF16)).astype(o_ref.dtype)


def cmp_finish(p, pe, w1cat, b1, w2):
    b, _, _, n_sub, _ = p.shape
    pe_rows = jnp.broadcast_to(pe.reshape(2, 2, 1, SUB_FLAT), (2, 2, SUBLANES, SUB_FLAT))
    return pl.pallas_call(
        _cmp_finish_body, grid=(b, 2, N_KV),
        in_specs=[pl.BlockSpec((1, 1, 1, n_sub, 2 * HEAD_DIM), lambda bi, kv, g: (bi, kv, g, 0, 0)),
                  pl.BlockSpec((1, 2, SUBLANES, SUB_FLAT), lambda bi, kv, g: (kv, 0, 0, 0)),
                  pl.BlockSpec((1, SUB_FLAT, 2 * HEAD_DIM), lambda bi, kv, g: (kv, 0, 0)),
                  pl.BlockSpec((1, 1, HEAD_DIM), lambda bi, kv, g: (kv, 0, 0)),
                  pl.BlockSpec((1, HEAD_DIM, HEAD_DIM), lambda bi, kv, g: (kv, 0, 0))],
        out_specs=pl.BlockSpec((1, 1, 1, n_sub, HEAD_DIM), lambda bi, kv, g: (bi, kv, g, 0, 0)),
        out_shape=jax.ShapeDtypeStruct((b, 2, N_KV, n_sub, HEAD_DIM), BF16),
        compiler_params=_params(("arbitrary",) * 3, 16 << 20),
    )(p, pe_rows, w1cat, b1.reshape(2, 1, HEAD_DIM), w2)


def _masked_softmax(s, mask):
    s = jnp.where(mask, s, NEG)
    e = jnp.exp(s - jnp.max(s, axis=-1, keepdims=True))
    return jnp.where(mask, e / jnp.sum(e, axis=-1, keepdims=True), 0.0)


def _split3(x):
    x1 = x.astype(BF16)
    r1 = x - x1.astype(F32)
    x2 = r1.astype(BF16)
    x3 = (r1 - x2.astype(F32)).astype(BF16)
    return x1, x2, x3


def _overlap_matrix(n_cmp, n_sel):
    n = lax.broadcasted_iota(jnp.int32, (n_cmp, n_sel), 0) * CMP_STRIDE
    j = lax.broadcasted_iota(jnp.int32, (n_cmp, n_sel), 1) * L_SEL
    return jnp.where(jnp.logical_and(n < j + L_SEL, n + L_CMP > j), 1.0, 0.0).astype(BF16)


def _importance(p_sum, n_sel):
    ov = _overlap_matrix(p_sum.shape[1], n_sel)
    a, b, c = _split3(p_sum)
    return _dot(a, ov) + _dot(b, ov) + _dot(c, ov)


def _importance_t(p_sum_t, n_sel):
    n_cmp = p_sum_t.shape[0]
    j = lax.broadcasted_iota(jnp.int32, (n_sel, n_cmp), 0) * L_SEL
    n = lax.broadcasted_iota(jnp.int32, (n_sel, n_cmp), 1) * CMP_STRIDE
    ov_t = jnp.where(jnp.logical_and(n < j + L_SEL, n + L_CMP > j), 1.0, 0.0).astype(BF16)
    a, b, c = _split3(p_sum_t)
    return _dot(ov_t, a) + _dot(ov_t, b) + _dot(ov_t, c)


def _select_blocks_t(imp_t, t_pos):
    n, r = imp_t.shape
    j = lax.broadcasted_iota(jnp.int32, (n, r), 0)
    cur = t_pos // L_SEL
    forced = (j == 0) | (j == cur) | (j == cur - 1)
    valid = j * L_SEL <= t_pos
    score = jnp.where(forced, FORCE, jnp.where(valid, imp_t, NEG))
    rank = jnp.zeros((n, r), F32)
    for k in range(n):
        row = score[k:k + 1, :]
        ahead = (row > score) | ((row == score) & (j > k))
        rank = rank + jnp.where(ahead, 1.0, 0.0)
    return (rank < N_SEL) & (score > 0.5 * NEG)


def _select_blocks(imp, t_pos):
    r, n = imp.shape
    j = lax.broadcasted_iota(jnp.int32, (r, n), 1)
    cur = t_pos // L_SEL
    forced = (j == 0) | (j == cur) | (j == cur - 1)
    valid = j * L_SEL <= t_pos
    score = jnp.where(forced, FORCE, jnp.where(valid, imp, NEG))
    ok = score > 0.5 * NEG
    picked = jnp.zeros((r, n), jnp.bool_)
    left = score
    jf = j.astype(F32)
    for _ in range(N_SEL):
        top = jnp.max(left, axis=-1, keepdims=True)
        first = jnp.min(jnp.where(left == top, jf, float(n)), axis=-1, keepdims=True)
        hit = jf == first
        picked = picked | hit
        left = jnp.where(hit, -jnp.inf, left)
    return picked & ok


def _flash_update(s, v, m_ref, l_ref, acc_ref):
    m_prev = m_ref[...]
    m_new = jnp.maximum(m_prev, jnp.max(s, axis=-1, keepdims=True))
    alpha = jnp.exp(m_prev - m_new)
    p = jnp.exp(s - m_new)
    l_ref[...] = alpha * l_ref[...] + jnp.sum(p, axis=-1, keepdims=True)
    acc_ref[...] = alpha * acc_ref[...] + _dot(p.astype(BF16), v)
    m_ref[...] = m_new


PQ = 128
PK = 512
P_SEL_PAD = 128
WIN_SPAN = WINDOW + PQ


def _nsa_prompt_body(q_ref, ks_ref, vs_ref, kw_ref, vw_ref, kc_ref, vc_ref, gl_ref, o_ref,
                     kaug_ref, vsb_ref, kwb_ref, vwb_ref, qa_ref, m_ref, l_ref, acc_ref, *, seq):
    i = pl.program_id(2)
    rows = GROUP * PQ
    half = rows // 2
    n_sel = seq // L_SEL
    scale = HEAD_DIM ** -0.5

    @pl.when(i == 0)
    def _():
        kaug_ref[:, 0:HEAD_DIM] = ks_ref[0].astype(BF16)
        key = lax.broadcasted_iota(jnp.int32, (seq, P_SEL_PAD), 0)
        blk = lax.broadcasted_iota(jnp.int32, (seq, P_SEL_PAD), 1)
        kaug_ref[:, HEAD_DIM:] = jnp.where(key // L_SEL == blk, 1.0, 0.0).astype(BF16)
        vsb_ref[...] = vs_ref[0].astype(BF16)
        kwb_ref[...] = kw_ref[0].astype(BF16)
        vwb_ref[...] = vw_ref[0].astype(BF16)

    q = q_ref[0]
    qs = jnp.concatenate([q[:, r * HEAD_DIM:(r + 1) * HEAD_DIM] for r in range(GROUP)], axis=0)
    qb = (qs * scale).astype(BF16)
    qa_ref[:, 0:HEAD_DIM] = qb
    t_q = i * PQ + lax.broadcasted_iota(jnp.int32, (PQ, 1), 0)
    t_rows = jnp.concatenate([t_q] * GROUP, axis=0)

    kc = kc_ref[0, 0, 0]
    n_cmp = kc.shape[0]
    s = _dot_nt(qb, kc)
    cmp_end = lax.broadcasted_iota(jnp.int32, (rows, n_cmp), 1) * CMP_STRIDE + (L_CMP - 1)
    p_cmp = _masked_softmax(s, cmp_end <= t_rows)
    o_cmp = _dot(p_cmp.astype(BF16), vc_ref[0, 0, 0])

    w0 = pl.multiple_of(jnp.clip(i * PQ - WINDOW, 0, seq - WIN_SPAN), PQ)
    s = _dot_nt(qb, kwb_ref[pl.ds(w0, WIN_SPAN), :])
    wpos = w0 + lax.broadcasted_iota(jnp.int32, (rows, WIN_SPAN), 1)
    p_win = _masked_softmax(s, (wpos <= t_rows) & (wpos > t_rows - WINDOW))
    o_win = _dot(p_win.astype(BF16), vwb_ref[pl.ds(w0, WIN_SPAN), :])

    p_sum = p_cmp[0:PQ]
    for r in range(1, GROUP):
        p_sum = p_sum + p_cmp[r * PQ:(r + 1) * PQ]
    t_lane = i * PQ + lax.broadcasted_iota(jnp.int32, (1, PQ), 1)
    sel_t = _select_blocks_t(_importance_t(p_sum.T, n_sel), t_lane)
    pen_t = jnp.concatenate([jnp.where(sel_t, 0.0, NEG), jnp.zeros((P_SEL_PAD - n_sel, PQ), F32)], axis=0)
    pen = pen_t.T.astype(BF16)
    for r in range(GROUP):
        qa_ref[r * PQ:(r + 1) * PQ, HEAD_DIM:] = pen

    m_ref[...] = jnp.full(m_ref.shape, NEG, F32)
    l_ref[...] = jnp.zeros(l_ref.shape, F32)
    acc_ref[...] = jnp.zeros(acc_ref.shape, F32)

    def slc_tile(kt, causal):
        k0 = pl.multiple_of(kt * PK, PK)
        kaug = kaug_ref[pl.ds(k0, PK), :]
        v = vsb_ref[pl.ds(k0, PK), :]
        for h in range(2):
            hs = pl.ds(h * half, half)
            s = _dot_nt(qa_ref[hs, :], kaug)
            if causal:
                kpos = k0 + lax.broadcasted_iota(jnp.int32, (half, PK), 1)
                s = jnp.where(kpos <= t_rows[h * half:(h + 1) * half], s, NEG)
            _flash_update(s, v, m_ref.at[hs], l_ref.at[hs], acc_ref.at[hs])

    n_full = (i * PQ) // PK

    def full_tile(kt, carry):
        slc_tile(kt, False)
        return carry

    lax.fori_loop(0, n_full, full_tile, 0)
    slc_tile(n_full, True)
    o_slc = acc_ref[...] / l_ref[...]

    per_group = GROUP * N_BRANCH
    gate = pltpu.roll(_sigmoid(gl_ref[0]), (LANES - pl.program_id(1) * per_group) % LANES, 1)
    for r in range(GROUP):
        rs = slice(r * PQ, (r + 1) * PQ)
        c0 = r * N_BRANCH
        o = (gate[:, c0:c0 + 1] * o_cmp[rs] + gate[:, c0 + 1:c0 + 2] * o_slc[rs]
             + gate[:, c0 + 2:c0 + 3] * o_win[rs])
        o_ref[0, :, r * HEAD_DIM:(r + 1) * HEAD_DIM] = o.astype(o_ref.dtype)


def nsa_prompt(qkv, cblk):
    b, t, _ = qkv.shape
    qcol = ATTN_W // HEAD_DIM

    def kv_spec(slot):
        return pl.BlockSpec((1, t, HEAD_DIM), lambda bi, g, i: (bi, 0, qcol + slot * N_KV + g))

    n_cmp = cblk.shape[3]
    rows = GROUP * PQ
    return pl.pallas_call(
        functools.partial(_nsa_prompt_body, seq=t),
        grid=(b, N_KV, t // PQ),
        in_specs=[pl.BlockSpec((1, PQ, GROUP * HEAD_DIM), lambda bi, g, i: (bi, i, g)),
                  kv_spec(2), kv_spec(3), kv_spec(4), kv_spec(5),
                  pl.BlockSpec((1, 1, 1, n_cmp, HEAD_DIM), lambda bi, g, i: (bi, 0, g, 0, 0)),
                  pl.BlockSpec((1, 1, 1, n_cmp, HEAD_DIM), lambda bi, g, i: (bi, 1, g, 0, 0)),
                  pl.BlockSpec((1, PQ, LANES), lambda bi, g, i: (bi, i, QKV_W // LANES))],
        out_specs=pl.BlockSpec((1, PQ, GROUP * HEAD_DIM), lambda bi, g, i: (bi, i, g)),
        out_shape=jax.ShapeDtypeStruct((b, t, ATTN_W), BF16),
        scratch_shapes=[pltpu.VMEM((t, 2 * HEAD_DIM), BF16), pltpu.VMEM((t, HEAD_DIM), BF16),
                        pltpu.VMEM((t, HEAD_DIM), BF16), pltpu.VMEM((t, HEAD_DIM), BF16),
                        pltpu.VMEM((rows, 2 * HEAD_DIM), BF16), pltpu.VMEM((rows, 1), F32), pltpu.VMEM((rows, 1), F32),
                        pltpu.VMEM((rows, HEAD_DIM), F32)],
        compiler_params=_params(("arbitrary",) * 3, 40 << 20),
    )(qkv, qkv, qkv, qkv, qkv, cblk, cblk, qkv)


SP = 8
S_KEYS = SP * PAGE_SIZE
S_BLKS = S_KEYS // L_SEL
S_CHUNK = LANES // S_BLKS
S_ROWS = GROUP * S_PAD


def _nsa_sample_body(pt_ref, *refs, t_new, n_steps, n_sel_pad):
    pages = refs[:SP]
    (q_ref, cb_ref, win_ref, gl_ref, o_ref,
     qb_ref, pen_ref, new_ref, ocw_ref, m_ref, l_ref, acc_ref) = refs[SP:]
    i = pl.program_id(1)
    scale = HEAD_DIM ** -0.5
    n_chunks = n_sel_pad // LANES
    n_win = win_ref.shape[1] // (2 * N_KV)
    tok = lax.broadcasted_iota(jnp.int32, (S_PAD, 1), 0)
    t_q = PAST_LEN + tok
    t_rows = jnp.concatenate([t_q] * GROUP, axis=0)
    qcol = ATTN_W

    @pl.when(i == 0)
    def _():
        pad = jnp.zeros((LANES - S_PAD, HEAD_DIM), F32)
        for g in range(N_KV):
            heads = [q_ref[0, :, (g * GROUP + r) * HEAD_DIM:(g * GROUP + r + 1) * HEAD_DIM]
                     for r in range(GROUP)]
            qb_ref[g] = (jnp.concatenate(heads, axis=0) * scale).astype(BF16)
            for slot in range(2, 6):
                c0 = qcol + (slot * N_KV + g) * HEAD_DIM
                new_ref[slot - 2, g] = jnp.concatenate(
                    [q_ref[0, :, c0:c0 + HEAD_DIM], pad], axis=0).astype(BF16)
        m_ref[...] = jnp.full(m_ref.shape, NEG, F32)
        l_ref[...] = jnp.zeros(l_ref.shape, F32)
        acc_ref[...] = jnp.zeros(acc_ref.shape, F32)
        p_sums = []
        for g in range(N_KV):
            qb = qb_ref[g]
            kc = cb_ref[0, 0, g]
            n_cmp = kc.shape[0]
            s = _dot_nt(qb, kc)
            cmp_end = lax.broadcasted_iota(jnp.int32, (S_ROWS, n_cmp), 1) * CMP_STRIDE + (L_CMP - 1)
            p_cmp = _masked_softmax(s, cmp_end <= t_rows)
            ocw_ref[0, g] = _dot(p_cmp.astype(BF16), cb_ref[0, 1, g])
            p_sum = p_cmp[0:S_PAD]
            for r in range(1, GROUP):
                p_sum = p_sum + p_cmp[r * S_PAD:(r + 1) * S_PAD]
            p_sums.append(p_sum)
            kw = win_ref[0, pl.ds(g, n_win, stride=2 * N_KV), :].astype(BF16)
            vw = win_ref[0, pl.ds(N_KV + g, n_win, stride=2 * N_KV), :].astype(BF16)
            s = jnp.concatenate([_dot_nt(qb, kw), _dot_nt(qb, new_ref[2, g])], axis=1)
            lane = lax.broadcasted_iota(jnp.int32, (S_ROWS, n_win + LANES), 1)
            wpos = PAST_LEN - n_win + lane
            mask = (wpos <= t_rows) & (wpos > t_rows - WINDOW) & (lane < n_win + t_new)
            p_win = _masked_softmax(s, mask).astype(BF16)
            ocw_ref[1, g] = _dot(p_win[:, 0:n_win], vw) + _dot(p_win[:, n_win:], new_ref[3, g])
        imp = _importance(jnp.concatenate(p_sums, axis=0), n_sel_pad)
        sel = _select_blocks(imp, jnp.concatenate([t_q] * N_KV, axis=0))
        pen = jnp.where(sel, 0.0, NEG).astype(BF16)
        for g in range(N_KV):
            for ch in range(n_chunks):
                blk = pen[g * S_PAD:(g + 1) * S_PAD, ch * LANES:(ch + 1) * LANES]
                pen_ref[ch, g] = jnp.concatenate([blk] * GROUP, axis=0)

    key = lax.broadcasted_iota(jnp.int32, (S_KEYS, LANES), 0)
    blk = lax.broadcasted_iota(jnp.int32, (S_KEYS, LANES), 1)
    ind = jnp.where((i % S_CHUNK) * S_BLKS + key // L_SEL == blk, 1.0, 0.0).astype(BF16)
    for g in range(N_KV):
        k = jnp.concatenate(
            [p[0, pl.ds(g, PAGE_SIZE, stride=2 * N_KV), :] for p in pages], axis=0)
        v = jnp.concatenate(
            [p[0, pl.ds(N_KV + g, PAGE_SIZE, stride=2 * N_KV), :] for p in pages], axis=0)
        kaug = jnp.concatenate([k.astype(BF16), ind], axis=1)
        qa = jnp.concatenate([qb_ref[g], pen_ref[i // S_CHUNK, g]], axis=1)
        _flash_update(_dot_nt(qa, kaug), v.astype(BF16), m_ref.at[g], l_ref.at[g], acc_ref.at[g])

    @pl.when(i == n_steps - 1)
    def _():
        gate = _sigmoid(gl_ref[0])
        lane = lax.broadcasted_iota(jnp.int32, (S_ROWS, LANES), 1)
        cur_chunk, cur_lane = (PAST_LEN // L_SEL) // LANES, (PAST_LEN // L_SEL) % LANES
        for g in range(N_KV):
            pen_cur = pen_ref[cur_chunk, g][:, cur_lane:cur_lane + 1].astype(F32)
            s = _dot_nt(qb_ref[g], new_ref[0, g]) + pen_cur
            s = jnp.where((PAST_LEN + lane <= t_rows) & (lane < t_new), s, NEG)
            _flash_update(s, new_ref[1, g], m_ref.at[g], l_ref.at[g], acc_ref.at[g])
            o_slc = acc_ref[g] / l_ref[g]
            for r in range(GROUP):
                rs = slice(r * S_PAD, (r + 1) * S_PAD)
                c0 = (g * GROUP + r) * N_BRANCH
                o = (gate[:, c0:c0 + 1] * ocw_ref[0, g][rs] + gate[:, c0 + 1:c0 + 2] * o_slc[rs]
                     + gate[:, c0 + 2:c0 + 3] * ocw_ref[1, g][rs])
                h = g * GROUP + r
                o_ref[0, :, h * HEAD_DIM:(h + 1) * HEAD_DIM] = o.astype(o_ref.dtype)


def nsa_sample(qkv, t_new, cblk, cache_slc, cache_win, page_table, gl):
    b = qkv.shape[0]
    n_pages = page_table.shape[1]
    n_steps = n_pages // SP
    n_cmp = cblk.shape[3]
    n_win = cache_win.shape[1] // (2 * N_KV)
    n_sel = -(-(PAST_LEN + t_new) // L_SEL)
    n_sel_pad = -(-n_sel // LANES) * LANES
    assert n_steps * S_BLKS <= n_sel_pad and n_steps * SP == n_pages

    def page_spec(c):
        return pl.BlockSpec((1, PAGE_SIZE * 2 * N_KV, HEAD_DIM),
                            lambda bi, i, pt: (pt[bi, i * SP + c], 0, 0))

    grid_spec = pltpu.PrefetchScalarGridSpec(
        num_scalar_prefetch=1, grid=(b, n_steps),
        in_specs=[page_spec(c) for c in range(SP)] + [
            pl.BlockSpec((1, S_PAD, QKV_W), lambda bi, i, pt: (bi, 0, 0)),
            pl.BlockSpec((1, 2, N_KV, n_cmp, HEAD_DIM), lambda bi, i, pt: (bi, 0, 0, 0, 0)),
            pl.BlockSpec((1, n_win * 2 * N_KV, HEAD_DIM), lambda bi, i, pt: (bi, 0, 0)),
            pl.BlockSpec((1, S_PAD, LANES), lambda bi, i, pt: (bi, 0, 0))],
        out_specs=pl.BlockSpec((1, S_PAD, ATTN_W), lambda bi, i, pt: (bi, 0, 0)),
        scratch_shapes=[pltpu.VMEM((N_KV, S_ROWS, HEAD_DIM), BF16),
                        pltpu.VMEM((n_sel_pad // LANES, N_KV, S_ROWS, LANES), BF16),
                        pltpu.VMEM((4, N_KV, LANES, HEAD_DIM), BF16),
                        pltpu.VMEM((2, N_KV, S_ROWS, HEAD_DIM), F32),
                        pltpu.VMEM((N_KV, S_ROWS, 1), F32), pltpu.VMEM((N_KV, S_ROWS, 1), F32),
                        pltpu.VMEM((N_KV, S_ROWS, HEAD_DIM), F32)])
    return pl.pallas_call(
        functools.partial(_nsa_sample_body, t_new=t_new, n_steps=n_steps, n_sel_pad=n_sel_pad),
        grid_spec=grid_spec,
        out_shape=jax.ShapeDtypeStruct((b, S_PAD, ATTN_W), F32),
        compiler_params=_params(("arbitrary", "arbitrary"), 40 << 20),
    )(page_table, *([cache_slc] * SP), qkv, cblk, cache_win, gl)


def _kv_rows_body(x_ref, o_ref):
    tt = x_ref.shape[1]
    for kvg in range(2 * N_KV):
        o_ref[0, pl.ds(kvg, tt, stride=2 * N_KV), :] = x_ref[0, :, kvg * HEAD_DIM:(kvg + 1) * HEAD_DIM]


def kv_rows(qkv, slot, t0, tt):
    b, t, _ = qkv.shape
    n = (t - t0) // tt
    assert n * tt == t - t0 and t0 % tt == 0
    col_blk = (ATTN_W + slot * KV_W) // PAGE_ROW_W
    out = pl.pallas_call(
        _kv_rows_body, grid=(b, n),
        in_specs=[pl.BlockSpec((1, tt, PAGE_ROW_W), lambda bi, i: (bi, t0 // tt + i, col_blk))],
        out_specs=pl.BlockSpec((1, tt * 2 * N_KV, HEAD_DIM), lambda bi, i: (bi, i, 0)),
        out_shape=jax.ShapeDtypeStruct((b, (t - t0) * 2 * N_KV, HEAD_DIM), F32),
        compiler_params=_params(("arbitrary", "arbitrary"), 6 * tt * PAGE_ROW_W * 4),
    )(qkv)
    return out.reshape(b, t - t0, 2, N_KV, HEAD_DIM)


def _pad_rows(x, front, total):
    return jnp.pad(x, ((0, 0), (front, total - front - x.shape[1]), (0, 0)))


def kernel(x_prompt, x_sample, cache_cmp_kv, cache_slc_kv, cache_win_kv, state_conf_conv,
           state_short_conv, page_table, p_prompt, p_sample, g_mix0, w_in0, conv_w0, conv_b0,
           ln_g0, ln_b0, cmp_pe, cmp_w1, cmp_b1, cmp_w2, w_out0, g_mix1, w_in1, sconv_w1, w_out1,
           g_ffn, w_ffn_gate, w_ffn_up, w_ffn_down, g_ple, w_ple_gate, w_ple_proj, g_final):
    bp, seq, d = x_prompt.shape
    bs, t_new, _ = x_sample.shape
    mp, ms = bp * seq, bs * t_new
    d_ff = w_ffn_gate.shape[-1]
    n_pool = cache_cmp_kv.shape[1]
    tm = 1024

    xp = x_prompt.reshape(mp, d)
    xs = x_sample.reshape(ms, d)
    rope_p = rope_tables(jnp.arange(seq, dtype=jnp.int32))
    rope_s = tuple(jnp.tile(tb, (bs, 1)) for tb in rope_tables(PAST_LEN + jnp.arange(t_new, dtype=jnp.int32)))
    outs = {}
    w_down_bf16 = w_ffn_down.astype(BF16)

    def residual(groups, x, tm_, tn_, single_buffer_rows=False):
        res_p, res_s = fused_mm(groups, _ep_residual, [F32], d, tm_, tn_, tilex=[x], m_outer=True,
                                emit_bf16=True, emit_ssq=True, single_buffer_rows=single_buffer_rows)
        return tuple(zip(res_p, res_s))

    def normed(stream, gain, ws):
        (_, _), (b_p, b_s), ssq = stream
        return Lhs(b_p, b_s, ws, gain=gain, ssq=ssq)

    def ffn_ple(stream, i, last):
        (gp,), (gs,) = fused_mm([normed(stream, g_ffn[i], [W(w_ffn_gate, i), W(w_ffn_up, i)])],
                                _ep_swiglu, [BF16], d_ff, 2 * tm, 256, m_outer=True, single_buffer_rows=True)
        stream = residual([Lhs(gp, gs, [W(w_down_bf16, i)])], stream[0], tm, 256, single_buffer_rows=True)
        pp = p_prompt[i].reshape(mp, -1).astype(BF16)
        ps = p_sample[i].reshape(ms, -1).astype(BF16)
        res_p, res_s = fused_mm([normed(stream, g_ple[i], [W(w_ple_gate, i)]), Lhs(pp, ps, [W(w_ple_proj, i)])],
                                _ep_ple, [F32], d, tm, 512, tilex=[stream[0]], m_outer=True,
                                emit_bf16=not last, emit_ssq=not last)
        return tuple(zip(res_p, res_s))

    e = 0
    hp, hs = rmsnorm(xp, g_mix0[e], BF16), rmsnorm(xs, g_mix0[e], BF16)
    w_in0_t = jnp.swapaxes(w_in0, 1, 2)
    win = lambda col_off: W(w_in0_t, e, col_off=col_off, transposed=True)
    (up,), (us,) = fused_mm([Lhs(hp, hs, [win(0), win(CONF_CH // 256)])], _ep_glu,
                            [F32], CONF_CH, tm, 256)
    (qkv_p,), (qkv_s,) = fused_mm([Lhs(hp, hs, [win(2 * CONF_CH // KV_W)])], _ep_rope,
                                  [F32], QKVG_W, tm, KV_W, rowx=list(zip(rope_p, rope_s)))

    up3, us3 = up.reshape(bp, seq, CONF_CH), us.reshape(bs, t_new, CONF_CH)
    conf_w = (conv_w0[e], conv_b0[e], ln_g0[e], ln_b0[e])
    a_p = conf_conv(up3, jnp.zeros((bp, HALO, CONF_CH), F32), *conf_w, tt=128, out_dtype=BF16)
    st = state_conf_conv[e]
    a_s = conf_conv(_pad_rows(us3, 0, S_PAD), _pad_rows(st, HALO - st.shape[1], HALO), *conf_w,
                    tt=S_PAD, out_dtype=F32)
    outs["conf_p"] = up3[:, seq - (CONF_W - 1):]
    outs["conf_s"] = jnp.concatenate([st, us3], axis=1)[:, t_new:]

    qkv_p3, qkv_s3 = qkv_p.reshape(bp, seq, QKVG_W), qkv_s.reshape(bs, t_new, QKVG_W)
    kv_shape = lambda x: x.reshape(x.shape[0], x.shape[1], 2, N_KV, HEAD_DIM)
    for name, slot in (("cmp", 0), ("slc", 2), ("win", 4)):
        c0 = ATTN_W + slot * KV_W
        t0 = seq - min(WINDOW, seq) if name == "win" else 0
        outs[name + "_p"] = kv_rows(qkv_p3, slot, t0, 512)
        outs[name + "_s"] = kv_shape(qkv_s3[:, :, c0:c0 + 2 * KV_W])
    outs["win_s"] = jnp.concatenate([cache_win_kv[e], outs["win_s"]], axis=1)[:, t_new:]

    pages_p = seq // PAGE_SIZE
    ident = jnp.arange(bp * pages_p, dtype=jnp.int32).reshape(bp, pages_p)
    w1cat = cmp_w1_cat(cmp_w1[e])
    proj_p = cmp_project(qkv_p.reshape(bp * pages_p, PAGE_SIZE, QKVG_W), False, ATTN_W // PAGE_ROW_W,
                         ident, w1cat)
    row_view = lambda c: c.reshape(-1, c.shape[-4] * 2 * N_KV, HEAD_DIM)
    pool_pages = page_table + e * n_pool
    proj_s = cmp_project(row_view(cache_cmp_kv), True, 0, pool_pages, w1cat)
    cblk_p = cmp_finish(proj_p, cmp_pe[e], w1cat, cmp_b1[e], cmp_w2[e])
    cblk_s = cmp_finish(proj_s, cmp_pe[e], w1cat, cmp_b1[e], cmp_w2[e])

    attn_p = nsa_prompt(qkv_p3, cblk_p)
    gl_s3 = jnp.pad(qkv_s3[:, :, QKV_W:QKV_W + LANES], ((0, 0), (0, S_PAD - t_new), (0, 0)))
    attn_s = nsa_sample(_pad_rows(qkv_s3[:, :, :QKV_W], 0, S_PAD), t_new, cblk_s, row_view(cache_slc_kv),
                        row_view(cache_win_kv[e]), pool_pages, gl_s3)
    a_p2, a_s2 = a_p.reshape(mp, CONF_CH), a_s[:, :t_new].reshape(ms, CONF_CH).astype(BF16)
    at_p2, at_s2 = attn_p.reshape(mp, ATTN_W), attn_s[:, :t_new].reshape(ms, ATTN_W).astype(BF16)
    stream = residual(
        [Lhs(a_p2, a_s2, [W(w_out0, e, k_rows=CONF_CH, row_blk=0)]),
         Lhs(at_p2, at_s2, [W(w_out0, e, k_rows=ATTN_W, row_blk=CONF_CH // ATTN_W)])],
        (xp, xs), tm, 512)
    stream = ffn_ple(stream, 0, last=False)

    o = 0
    (bg_p, cv_p), (bg_s, cv_s) = fused_mm(
        [normed(stream, g_mix1[o], [W(w_in1, o), W(w_in1, o, col_off=SC_CH // 256),
                                    W(w_in1, o, col_off=2 * SC_CH // 256)])],
        _ep_shortconv, [F32, F32], SC_CH, 512, 256)
    cv_p3, cv_s3 = cv_p.reshape(bp, seq, SC_CH), cv_s.reshape(bs, t_new, SC_CH)
    y_p = short_conv(cv_p3, bg_p.reshape(bp, seq, SC_CH), jnp.zeros((bp, SUBLANES, SC_CH), F32),
                     sconv_w1[o], tt=256, out_dtype=BF16)
    st = state_short_conv[o]
    y_s = short_conv(_pad_rows(cv_s3, 0, S_PAD), _pad_rows(bg_s.reshape(bs, t_new, SC_CH), 0, S_PAD),
                     _pad_rows(st, SUBLANES - st.shape[1], SUBLANES), sconv_w1[o], tt=S_PAD,
                     out_dtype=F32)
    outs["sc_p"] = cv_p3[:, seq - (SC_W - 1):]
    outs["sc_s"] = jnp.concatenate([st, cv_s3], axis=1)[:, t_new:]
    stream = residual(
        [Lhs(y_p.reshape(mp, SC_CH), y_s[:, :t_new].reshape(ms, SC_CH).astype(BF16), [W(w_out1, o)])],
        stream[0], tm, 512)
    (xp, xs), = ffn_ple(stream, 1, last=True)

    y_p = rmsnorm(xp, g_final, F32).reshape(bp, seq, d)
    y_s = rmsnorm(xs, g_final, F32).reshape(bs, t_new, d)
    st1 = lambda x: x[None]
    return (y_p, y_s, st1(outs["cmp_p"]), st1(outs["cmp_s"]), st1(outs["slc_p"]), st1(outs["slc_s"]),
            st1(outs["win_p"]), st1(outs["win_s"]), st1(outs["conf_p"]), st1(outs["conf_s"]),
            st1(outs["sc_p"]), st1(outs["sc_s"]))
```

```python
import functools

import jax
import jax.numpy as jnp
from jax import lax
from jax.experimental import pallas as pl
from jax.experimental.pallas import tpu as pltpu

F32 = jnp.float32
BF16 = jnp.bfloat16

V7X_VMEM_BYTES = 64 * 1024 * 1024
LANES = 128
SUBLANES = 8

D_MODEL = 4096
PAST_LEN = 16384
PAGE_SIZE = 128
N_HEADS = 16
HEAD_DIM = 128
N_KV = 4
GROUP = N_HEADS // N_KV
ATTN_W = N_HEADS * HEAD_DIM
KV_W = N_KV * HEAD_DIM
ROPE_DIM = HEAD_DIM // 4
ROPE_THETA = 500000.0
L_CMP = 32
CMP_STRIDE = 16
L_SEL = 64
N_SEL = 16
WINDOW = 512
N_BRANCH = 3
CONF_CH = D_MODEL // 2
CONF_W = 31
SC_CH = D_MODEL
SC_W = 3
RMS_EPS = 1e-6
LN_EPS = 1e-5
NEG = -1e30
FORCE = 1e9
QKV_W = ATTN_W + 6 * KV_W
QKVG_W = QKV_W + KV_W
SUB_PER_PAGE = PAGE_SIZE // CMP_STRIDE
SUB_FLAT = CMP_STRIDE * HEAD_DIM
PAGE_ROW_W = 2 * KV_W
HALO = 32
S_PAD = 8


def _vmem_limit(n_bytes):
    return int(min(V7X_VMEM_BYTES - (6 << 20), max(n_bytes, 16 << 20)))


def _params(sem, vmem):
    return pltpu.CompilerParams(dimension_semantics=sem, vmem_limit_bytes=_vmem_limit(vmem))


def _dot(a, b):
    return jnp.dot(a, b, preferred_element_type=F32)


def _dot_nt(a, b):
    return lax.dot_general(a, b, (((1,), (1,)), ((), ())), preferred_element_type=F32)


def _sigmoid(x):
    return jax.nn.sigmoid(x)


def _rmsnorm_body(x_ref, g_ref, o_ref):
    x = x_ref[...]
    y = x * lax.rsqrt(jnp.mean(x * x, axis=-1, keepdims=True) + RMS_EPS)
    o_ref[...] = (y * g_ref[...]).astype(o_ref.dtype)


def rmsnorm(x, g, out_dtype, rows=256):
    m, d = x.shape
    tr = min(rows, m)
    return pl.pallas_call(
        _rmsnorm_body,
        grid=(m // tr,),
        in_specs=[pl.BlockSpec((tr, d), lambda i: (i, 0)),
                  pl.BlockSpec((1, d), lambda i: (0, 0))],
        out_specs=pl.BlockSpec((tr, d), lambda i: (i, 0)),
        out_shape=jax.ShapeDtypeStruct((m, d), out_dtype),
        compiler_params=_params(("arbitrary",), 6 * tr * d * 4),
    )(x, g.reshape(1, d))


class W:
    def __init__(self, arr, layer=0, k_rows=None, row_blk=0, col_off=0, transposed=False):
        self.arr = arr
        self.layer = layer
        self.k_rows = arr.shape[2 if transposed else 1] if k_rows is None else k_rows
        self.row_blk = row_blk
        self.col_off = col_off
        self.transposed = transposed


class Lhs:
    def __init__(self, a_p, a_s, ws, gain=None, ssq=None):
        self.a_p, self.a_s, self.ws, self.gain, self.ssq = a_p, a_s, ws, gain, ssq


def _fold_lanes(x):
    out = x[:, 0:LANES]
    for c in range(1, x.shape[1] // LANES):
        out = out + x[:, c * LANES:(c + 1) * LANES]
    return out


def _mm_body(*refs, meta, n_rx, n_tx, n_main, emit_bf16, emit_ssq, epilogue, m_axis, n_j):
    it = iter(refs)
    groups = []
    for w_transposed, has_gain, has_ssq in meta:
        a = (next(it), next(it))
        gain = next(it) if has_gain else None
        ssq = (next(it), next(it)) if has_ssq else None
        groups.append((a, gain, ssq, [(next(it), t) for t in w_transposed]))
    rx = [[next(it) for _ in range(n_rx)] for _ in range(2)]
    tx = [[next(it) for _ in range(n_tx)] for _ in range(2)]
    n_out = n_main + emit_bf16 + emit_ssq
    outs = [[next(it) for _ in range(n_out)] for _ in range(2)]
    cache = list(it)
    i = pl.program_id(m_axis)
    j = pl.program_id(1 - m_axis)

    def prepared(w, gain):
        return (w[...] if gain is None else w[...] * gain[...]).astype(BF16)

    if cache:
        slots = iter(cache)
        wb = [[(next(slots), t) for _, t in ws] for _, _, _, ws in groups]

        @pl.when(i == 0)
        def _():
            for (_, gain, _, ws), cached in zip(groups, wb):
                for (w, _), (c, _) in zip(ws, cached):
                    c[...] = prepared(w, gain)
    else:
        wb = [[(prepared(w, gain), t) for w, t in ws] for _, gain, _, ws in groups]

    def run(which):
        dots = []
        for (a, _, ssq, _), wbs in zip(groups, wb):
            lhs = a[which][...]
            scale = None if ssq is None else ssq[which][...]
            for w, transposed in wbs:
                w = w[...]
                d = _dot_nt(lhs, w) if transposed else _dot(lhs, w)
                if scale is not None:
                    d = d * jnp.concatenate([scale] * (d.shape[1] // LANES), axis=1)
                dots.append(d)
        res = list(epilogue(dots, [r[...] for r in rx[which]], [t[...] for t in tx[which]], j))
        if emit_bf16:
            res.append(res[0])
        for o, r in zip(outs[which], res):
            o[...] = r.astype(o.dtype)
        if emit_ssq:
            part = _fold_lanes(res[0] * res[0])
            acc = outs[which][-1]

            @pl.when(j == 0)
            def _():
                acc[...] = part

            @pl.when(j > 0)
            def _():
                acc[...] += part

            @pl.when(j == n_j - 1)
            def _():
                total = jnp.sum(acc[...], axis=-1, keepdims=True)
                acc[...] = jnp.broadcast_to(lax.rsqrt(total / (n_j * res[0].shape[1]) + RMS_EPS), acc.shape)

    run(0)

    @pl.when(i == 0)
    def _():
        run(1)

    if m_axis == 0:
        @pl.when(i == 1)
        def _():
            for o in outs[1]:
                o[...] = jnp.zeros(o.shape, o.dtype)


def fused_mm(groups, epilogue, out_dtypes, n_cols, tm, tn, rowx=(), tilex=(), m_outer=False,
             emit_bf16=False, emit_ssq=False, single_buffer_rows=False):
    mp = groups[0].a_p.shape[0]
    ms = groups[0].a_s.shape[0]
    nj, ni = n_cols // tn, mp // tm
    assert nj * tn == n_cols and ni * tm == mp
    m_axis = 0 if m_outer else 1
    assert m_outer or not emit_ssq
    spare = 1 if m_outer else 0

    def spec(shape, fn, **kw):
        return pl.BlockSpec(shape, lambda *g: fn(g[m_axis], g[1 - m_axis]), **kw)

    assert m_outer or not single_buffer_rows
    rows_kw = dict(pipeline_mode=pl.Buffered(1)) if single_buffer_rows else {}
    args, in_specs, meta = [], [], []
    vmem = 0
    for g in groups:
        k = g.a_p.shape[1]
        args += [g.a_p, g.a_s]
        in_specs += [spec((tm, k), lambda i, j: (i, 0), **rows_kw), spec((ms, k), lambda i, j: (0, 0))]
        vmem += ((1 if single_buffer_rows else 2) * tm + 2 * ms) * k * g.a_p.dtype.itemsize
        if g.gain is not None:
            args.append(g.gain.reshape(k, 1))
            in_specs.append(spec((k, 1), lambda i, j: (0, 0)))
            vmem += 2 * k * LANES * 4
        if g.ssq is not None:
            args += list(g.ssq)
            in_specs += [spec((tm, g.ssq[0].shape[1]), lambda i, j: (i, 0)),
                         spec(g.ssq[1].shape, lambda i, j: (0, 0))]
            vmem += 2 * (tm + ms) * g.ssq[0].shape[1] * 4
        for w in g.ws:
            assert w.k_rows == k and not (w.transposed and g.gain is not None)
            args.append(w.arr)
            if w.transposed:
                in_specs.append(spec((None, tn, k), functools.partial(
                    lambda i, j, la, rb, co: (la, co + j, rb), la=w.layer, rb=w.row_blk, co=w.col_off)))
            else:
                in_specs.append(spec((None, k, tn), functools.partial(
                    lambda i, j, la, rb, co: (la, rb, co + j), la=w.layer, rb=w.row_blk, co=w.col_off)))
            vmem += k * tn * (2 * w.arr.dtype.itemsize + 2)
        meta.append((tuple(w.transposed for w in g.ws), g.gain is not None, g.ssq is not None))
    for which in (0, 1):
        for tab_p, tab_s in rowx:
            if which == 0:
                per = tab_p.shape[0] // tm
                args.append(tab_p)
                in_specs.append(spec((tm, tab_p.shape[1]),
                                     functools.partial(lambda i, j, per: (i % per, 0), per=per)))
                vmem += 2 * tm * tab_p.shape[1] * 4
            else:
                args.append(tab_s)
                in_specs.append(spec(tab_s.shape, lambda i, j: (0, 0)))
    for which in (0, 1):
        for t_p, t_s in tilex:
            if which == 0:
                args.append(t_p)
                in_specs.append(spec((tm, tn), lambda i, j: (i, j)))
            else:
                args.append(t_s)
                in_specs.append(spec((ms, tn), lambda i, j: (0, j)))
    kinds = [(dt, tn, nj) for dt in out_dtypes]
    if emit_bf16:
        kinds.append((BF16, tn, nj))
    if emit_ssq:
        kinds.append((F32, LANES, 1))
    out_shape, out_specs = [], []
    for which in (0, 1):
        for dt, width, nblk in kinds:
            col = (lambda i, j: j) if nblk > 1 else (lambda i, j: 0)
            if which == 0:
                out_shape.append(jax.ShapeDtypeStruct((mp, nblk * width), dt))
                out_specs.append(spec((tm, width), functools.partial(lambda i, j, col: (i, col(i, j)), col=col)))
            else:
                out_shape.append(jax.ShapeDtypeStruct((ms, (nblk + spare) * width), dt))
                out_specs.append(spec((ms, width), functools.partial(
                    lambda i, j, col, nblk: (0, jnp.where(i == 0, col(i, j), nblk) if m_outer else col(i, j)),
                    col=col, nblk=nblk)))
    n_dots = sum(len(g.ws) for g in groups)
    vmem += (2 * (len(tilex) + len(kinds)) + n_dots + 2) * tm * tn * 4
    body = functools.partial(
        _mm_body, meta=tuple(meta), n_rx=len(rowx), n_tx=len(tilex), n_main=len(out_dtypes),
        emit_bf16=emit_bf16, emit_ssq=emit_ssq, epilogue=epilogue, m_axis=m_axis, n_j=nj)
    scratch = []
    if not m_outer and any(g.gain is not None for g in groups):
        scratch = [pltpu.VMEM((tn, w.k_rows) if w.transposed else (w.k_rows, tn), BF16)
                   for g in groups for w in g.ws]
    res = pl.pallas_call(
        body, grid=(ni, nj) if m_outer else (nj, ni), in_specs=in_specs, out_specs=out_specs,
        out_shape=out_shape, scratch_shapes=scratch,
        compiler_params=_params(("arbitrary", "arbitrary"), vmem + (6 << 20)),
    )(*args)
    n = len(kinds)
    sample = [r[:, :nblk * width] for r, (_, width, nblk) in zip(res[n:], kinds)]
    return res[:n], sample


def _ep_glu(dots, rx, tx, j):
    return [dots[0] * _sigmoid(dots[1])]


def _ep_swiglu(dots, rx, tx, j):
    return [jax.nn.silu(dots[0]) * dots[1]]


def _ep_residual(dots, rx, tx, j):
    return [tx[0] + sum(dots[1:], dots[0])]


def _ep_ple(dots, rx, tx, j):
    return [tx[0] + _sigmoid(dots[0]) * dots[1]]


def _ep_shortconv(dots, rx, tx, j):
    return [dots[0], dots[1] * dots[2]]


def _ep_rope(dots, rx, tx, j):
    z = dots[0]
    cos, sin_lo, sin_hi = rx
    half = ROPE_DIM // 2
    heads = []
    for h in range(z.shape[1] // HEAD_DIM):
        x = z[:, h * HEAD_DIM:(h + 1) * HEAD_DIM]
        heads.append(x * cos + pltpu.roll(x, half, 1) * sin_hi
                     + pltpu.roll(x, HEAD_DIM - half, 1) * sin_lo)
    roped = jnp.concatenate(heads, axis=1)
    slot = j - ATTN_W // z.shape[1]
    is_v = jnp.logical_and(slot >= 0, slot % 2 == 1)
    lane = lax.broadcasted_iota(jnp.int32, z.shape, 1)
    gates = jnp.where(lane < N_HEADS * N_BRANCH, z, 0.0)
    return [jnp.where(slot == 6, gates, jnp.where(is_v, z, roped))]


def rope_tables(pos):
    half = ROPE_DIM // 2
    inv = ROPE_THETA ** (-2.0 * jnp.arange(half, dtype=F32) / ROPE_DIM)
    ang = pos.astype(F32)[:, None] * inv[None, :]
    cos, sin = jnp.cos(ang), jnp.sin(ang)
    n = pos.shape[0]
    rest = HEAD_DIM - ROPE_DIM
    c = jnp.concatenate([cos, cos, jnp.ones((n, rest), F32)], axis=1)
    s_lo = jnp.concatenate([-sin, jnp.zeros((n, half + rest), F32)], axis=1)
    s_hi = jnp.concatenate([jnp.zeros((n, half), F32), sin, jnp.zeros((n, rest), F32)], axis=1)
    return c, s_lo, s_hi


def _conf_core(xw_ref, cw_ref, cb_ref, g_ref, b_ref, cbuf_ref, o_ref, tt):
    last = CONF_W - 1
    for c in range(CONF_CH // LANES):
        cs = slice(c * LANES, (c + 1) * LANES)
        acc = None
        for r in range(SUBLANES):
            y = None
            for a in range((last - r) // SUBLANES + 1):
                lo = HALO - SUBLANES * (a + 1)
                term = xw_ref[lo:lo + tt + SUBLANES, cs] * cw_ref[last - SUBLANES * a - r:last - SUBLANES * a - r + 1, cs]
                y = term if y is None else y + term
            part = y[SUBLANES - r:SUBLANES - r + tt]
            acc = part if acc is None else acc + part
        cbuf_ref[:, cs] = acc
    c = cbuf_ref[...] + cb_ref[...]
    mu = jnp.mean(c, axis=-1, keepdims=True)
    var = jnp.mean(jnp.square(c - mu), axis=-1, keepdims=True)
    y = (c - mu) * lax.rsqrt(var + LN_EPS) * g_ref[...] + b_ref[...]
    o_ref[0] = jax.nn.silu(y).astype(o_ref.dtype)


def _conf_body(prev_ref, halo_ref, x_ref, cw_ref, cb_ref, g_ref, b_ref, o_ref, xw_ref, cbuf_ref, *, tt):
    first = pl.program_id(1) == 0
    xw_ref[0:HALO] = jnp.where(first, prev_ref[0], halo_ref[0])
    xw_ref[HALO:HALO + tt] = x_ref[0]
    _conf_core(xw_ref, cw_ref, cb_ref, g_ref, b_ref, cbuf_ref, o_ref, tt)


def conf_conv(u, prev, cw, cb, ln_g, ln_b, tt, out_dtype):
    b, t, c = u.shape
    if tt < HALO:
        assert t == tt
        halo_spec = pl.BlockSpec((1, HALO, c), lambda bi, i: (bi, 0, 0))
        halo_arr = prev
    else:
        hb = tt // HALO
        halo_spec = pl.BlockSpec((1, HALO, c), lambda bi, i: (bi, jnp.maximum(i * hb - 1, 0), 0))
        halo_arr = u
    cwp = jnp.pad(cw, ((0, HALO - cw.shape[0]), (0, 0)))
    row = lambda v: v.reshape(1, c)
    const = lambda bi, i: (0, 0)
    return pl.pallas_call(
        functools.partial(_conf_body, tt=tt),
        grid=(b, t // tt),
        in_specs=[pl.BlockSpec((1, HALO, c), lambda bi, i: (bi, 0, 0)), halo_spec,
                  pl.BlockSpec((1, tt, c), lambda bi, i: (bi, i, 0)),
                  pl.BlockSpec((HALO, c), const), pl.BlockSpec((1, c), const),
                  pl.BlockSpec((1, c), const), pl.BlockSpec((1, c), const)],
        out_specs=pl.BlockSpec((1, tt, c), lambda bi, i: (bi, i, 0)),
        out_shape=jax.ShapeDtypeStruct((b, t, c), out_dtype),
        scratch_shapes=[pltpu.VMEM((tt + HALO, c), F32), pltpu.VMEM((tt, c), F32)],
        compiler_params=_params(("arbitrary", "arbitrary"), 10 * (tt + HALO) * c * 4),
    )(prev, halo_arr, u, cwp, row(cb), row(ln_g), row(ln_b))


def _short_body(prev_ref, halo_ref, cv_ref, bg_ref, w_ref, o_ref, xw_ref, *, tt):
    first = pl.program_id(1) == 0
    xw_ref[0:SUBLANES] = jnp.where(first, prev_ref[0], halo_ref[0])
    xw_ref[SUBLANES:SUBLANES + tt] = cv_ref[0]
    base = SUBLANES - (SC_W - 1)
    conv = xw_ref[pl.ds(base, tt), :] * w_ref[0:1, :]
    for k in range(1, SC_W):
        conv = conv + xw_ref[pl.ds(base + k, tt), :] * w_ref[k:k + 1, :]
    o_ref[0] = (bg_ref[0] * conv).astype(o_ref.dtype)


def short_conv(cv, bg, prev, w, tt, out_dtype):
    b, t, c = cv.shape
    hb = tt // SUBLANES
    wp = jnp.pad(w, ((0, SUBLANES - w.shape[0]), (0, 0)))
    return pl.pallas_call(
        functools.partial(_short_body, tt=tt),
        grid=(b, t // tt),
        in_specs=[pl.BlockSpec((1, SUBLANES, c), lambda bi, i: (bi, 0, 0)),
                  pl.BlockSpec((1, SUBLANES, c), lambda bi, i: (bi, jnp.maximum(i * hb - 1, 0), 0)),
                  pl.BlockSpec((1, tt, c), lambda bi, i: (bi, i, 0)),
                  pl.BlockSpec((1, tt, c), lambda bi, i: (bi, i, 0)),
                  pl.BlockSpec((SUBLANES, c), lambda bi, i: (0, 0))],
        out_specs=pl.BlockSpec((1, tt, c), lambda bi, i: (bi, i, 0)),
        out_shape=jax.ShapeDtypeStruct((b, t, c), out_dtype),
        scratch_shapes=[pltpu.VMEM((tt + SUBLANES, c), F32)],
        compiler_params=_params(("arbitrary", "arbitrary"), 10 * (tt + SUBLANES) * c * 4),
    )(prev, cv, cv, bg, wp)


CMP_PAGES = 16
SUB_PITCH = 24


def _cmp_proj_body(pt_ref, *refs, row_major):
    pages = refs[:CMP_PAGES]
    w1_ref, p_ref, kbuf_ref, x_ref = refs[CMP_PAGES:]
    rows = CMP_PAGES * SUB_PER_PAGE
    for kv in range(2):
        for g in range(N_KV):
            kvg = kv * N_KV + g
            for c in range(CMP_PAGES):
                if row_major:
                    tok = pages[c][0, pl.ds(kvg, PAGE_SIZE, stride=2 * N_KV), :]
                else:
                    tok = pages[c][0, :, kvg * HEAD_DIM:(kvg + 1) * HEAD_DIM]
                for m in range(SUB_PER_PAGE):
                    r0 = (c * SUB_PER_PAGE + m) * SUB_PITCH
                    kbuf_ref[g, r0:r0 + CMP_STRIDE, :] = tok[m * CMP_STRIDE:(m + 1) * CMP_STRIDE]
            for s in range(CMP_STRIDE):
                x_ref[g * rows:(g + 1) * rows, s * HEAD_DIM:(s + 1) * HEAD_DIM] = (
                    kbuf_ref[g, pl.ds(s, rows, stride=SUB_PITCH), :])
        p = _dot(x_ref[...].astype(BF16), w1_ref[kv])
        for g in range(N_KV):
            p_ref[0, kv, g] = p[g * rows:(g + 1) * rows]


def cmp_w1_cat(w1):
    w = w1.reshape(2, 2, SUB_FLAT, HEAD_DIM).transpose(0, 2, 1, 3)
    return w.reshape(2, SUB_FLAT, 2 * HEAD_DIM).astype(BF16)


def cmp_project(pages_arr, row_major, col_blk, page_table, w1cat):
    b, n_pages = page_table.shape
    n_sub = n_pages * SUB_PER_PAGE
    steps = n_pages // CMP_PAGES
    assert steps * CMP_PAGES == n_pages
    blk = (1, PAGE_SIZE * 2 * N_KV, HEAD_DIM) if row_major else (1, PAGE_SIZE, PAGE_ROW_W)

    def page_spec(c):
        return pl.BlockSpec(blk, lambda bi, i, pt: (pt[bi, i * CMP_PAGES + c], 0, col_blk))

    rows = CMP_PAGES * SUB_PER_PAGE
    grid_spec = pltpu.PrefetchScalarGridSpec(
        num_scalar_prefetch=1, grid=(b, steps),
        in_specs=[page_spec(c) for c in range(CMP_PAGES)] + [
            pl.BlockSpec((2, SUB_FLAT, 2 * HEAD_DIM), lambda bi, i, pt: (0, 0, 0))],
        out_specs=pl.BlockSpec((1, 2, N_KV, rows, 2 * HEAD_DIM), lambda bi, i, pt: (bi, 0, 0, i, 0)),
        scratch_shapes=[pltpu.VMEM((N_KV, rows * SUB_PITCH, HEAD_DIM), F32),
                        pltpu.VMEM((N_KV * rows, SUB_FLAT), F32)])
    return pl.pallas_call(
        functools.partial(_cmp_proj_body, row_major=row_major), grid_spec=grid_spec,
        out_shape=jax.ShapeDtypeStruct((b, 2, N_KV, n_sub, 2 * HEAD_DIM), F32),
        compiler_params=_params(("arbitrary", "arbitrary"), 32 << 20),
    )(page_table, *([pages_arr] * CMP_PAGES), w1cat)


def _cmp_finish_body(p_ref, pe_ref, w1_ref, b1_ref, w2_ref, o_ref):
    n_sub = p_ref.shape[-2]
    w1 = w1_ref[0]
    pe_term = (_dot(pe_ref[0, 0].astype(BF16), w1)[0:1, 0:HEAD_DIM]
               + _dot(pe_ref[0, 1].astype(BF16), w1)[0:1, HEAD_DIM:])
    p = p_ref[0, 0, 0]
    p1_next = pltpu.roll(p[:, HEAD_DIM:], n_sub - 1, 0)
    pre = p[:, 0:HEAD_DIM] + p1_next + (b1_ref[0] + pe_term)
    o_ref[0, 0, 0] = _dot(jax.nn.gelu(pre).astype(BF16), w2_ref[0].astype(BF16)).astype(o_ref.dtype)


def cmp_finish(p, pe, w1cat, b1, w2):
    b, _, _, n_sub, _ = p.shape
    pe_rows = jnp.broadcast_to(pe.reshape(2, 2, 1, SUB_FLAT), (2, 2, SUBLANES, SUB_FLAT))
    return pl.pallas_call(
        _cmp_finish_body, grid=(b, 2, N_KV),
        in_specs=[pl.BlockSpec((1, 1, 1, n_sub, 2 * HEAD_DIM), lambda bi, kv, g: (bi, kv, g, 0, 0)),
                  pl.BlockSpec((1, 2, SUBLANES, SUB_FLAT), lambda bi, kv, g: (kv, 0, 0, 0)),
                  pl.BlockSpec((1, SUB_FLAT, 2 * HEAD_DIM), lambda bi, kv, g: (kv, 0, 0)),
                  pl.BlockSpec((1, 1, HEAD_DIM), lambda bi, kv, g: (kv, 0, 0)),
                  pl.BlockSpec((1, HEAD_DIM, HEAD_DIM), lambda bi, kv, g: (kv, 0, 0))],
        out_specs=pl.BlockSpec((1, 1, 1, n_sub, HEAD_DIM), lambda bi, kv, g: (bi, kv, g, 0, 0)),
        out_shape=jax.ShapeDtypeStruct((b, 2, N_KV, n_sub, HEAD_DIM), BF16),
        compiler_params=_params(("arbitrary",) * 3, 16 << 20),
    )(p, pe_rows, w1cat, b1.reshape(2, 1, HEAD_DIM), w2)


def _masked_softmax(s, mask):
    s = jnp.where(mask, s, NEG)
    e = jnp.exp(s - jnp.max(s, axis=-1, keepdims=True))
    return jnp.where(mask, e / jnp.sum(e, axis=-1, keepdims=True), 0.0)


def _split3(x):
    x1 = x.astype(BF16)
    r1 = x - x1.astype(F32)
    x2 = r1.astype(BF16)
    x3 = (r1 - x2.astype(F32)).astype(BF16)
    return x1, x2, x3


def _overlap_matrix(n_cmp, n_sel):
    n = lax.broadcasted_iota(jnp.int32, (n_cmp, n_sel), 0) * CMP_STRIDE
    j = lax.broadcasted_iota(jnp.int32, (n_cmp, n_sel), 1) * L_SEL
    return jnp.where(jnp.logical_and(n < j + L_SEL, n + L_CMP > j), 1.0, 0.0).astype(BF16)


def _importance(p_sum, n_sel):
    ov = _overlap_matrix(p_sum.shape[1], n_sel)
    a, b, c = _split3(p_sum)
    return _dot(a, ov) + _dot(b, ov) + _dot(c, ov)


def _importance_t(p_sum_t, n_sel):
    n_cmp = p_sum_t.shape[0]
    j = lax.broadcasted_iota(jnp.int32, (n_sel, n_cmp), 0) * L_SEL
    n = lax.broadcasted_iota(jnp.int32, (n_sel, n_cmp), 1) * CMP_STRIDE
    ov_t = jnp.where(jnp.logical_and(n < j + L_SEL, n + L_CMP > j), 1.0, 0.0).astype(BF16)
    a, b, c = _split3(p_sum_t)
    return _dot(ov_t, a) + _dot(ov_t, b) + _dot(ov_t, c)


def _select_blocks_t(imp_t, t_pos):
    n, r = imp_t.shape
    j = lax.broadcasted_iota(jnp.int32, (n, r), 0)
    cur = t_pos // L_SEL
    forced = (j == 0) | (j == cur) | (j == cur - 1)
    valid = j * L_SEL <= t_pos
    score = jnp.where(forced, FORCE, jnp.where(valid, imp_t, NEG))
    rank = jnp.zeros((n, r), F32)
    for k in range(n):
        row = score[k:k + 1, :]
        ahead = (row > score) | ((row == score) & (j > k))
        rank = rank + jnp.where(ahead, 1.0, 0.0)
    return (rank < N_SEL) & (score > 0.5 * NEG)


def _select_blocks(imp, t_pos):
    r, n = imp.shape
    j = lax.broadcasted_iota(jnp.int32, (r, n), 1)
    cur = t_pos // L_SEL
    forced = (j == 0) | (j == cur) | (j == cur - 1)
    valid = j * L_SEL <= t_pos
    score = jnp.where(forced, FORCE, jnp.where(valid, imp, NEG))
    ok = score > 0.5 * NEG
    picked = jnp.zeros((r, n), jnp.bool_)
    left = score
    jf = j.astype(F32)
    for _ in range(N_SEL):
        top = jnp.max(left, axis=-1, keepdims=True)
        first = jnp.min(jnp.where(left == top, jf, float(n)), axis=-1, keepdims=True)
        hit = jf == first
        picked = picked | hit
        left = jnp.where(hit, -jnp.inf, left)
    return picked & ok


def _flash_update(s, v, m_ref, l_ref, acc_ref):
    m_prev = m_ref[...]
    m_new = jnp.maximum(m_prev, jnp.max(s, axis=-1, keepdims=True))
    alpha = jnp.exp(m_prev - m_new)
    p = jnp.exp(s - m_new)
    l_ref[...] = alpha * l_ref[...] + jnp.sum(p, axis=-1, keepdims=True)
    acc_ref[...] = alpha * acc_ref[...] + _dot(p.astype(BF16), v)
    m_ref[...] = m_new


PQ = 128
PK = 512
P_SEL_PAD = 128
WIN_SPAN = WINDOW + PQ


def _nsa_prompt_body(q_ref, ks_ref, vs_ref, kw_ref, vw_ref, kc_ref, vc_ref, gl_ref, o_ref,
                     kaug_ref, vsb_ref, kwb_ref, vwb_ref, qa_ref, m_ref, l_ref, acc_ref, *, seq):
    i = pl.program_id(2)
    rows = GROUP * PQ
    half = rows // 2
    n_sel = seq // L_SEL
    scale = HEAD_DIM ** -0.5

    @pl.when(i == 0)
    def _():
        kaug_ref[:, 0:HEAD_DIM] = ks_ref[0].astype(BF16)
        key = lax.broadcasted_iota(jnp.int32, (seq, P_SEL_PAD), 0)
        blk = lax.broadcasted_iota(jnp.int32, (seq, P_SEL_PAD), 1)
        kaug_ref[:, HEAD_DIM:] = jnp.where(key // L_SEL == blk, 1.0, 0.0).astype(BF16)
        vsb_ref[...] = vs_ref[0].astype(BF16)
        kwb_ref[...] = kw_ref[0].astype(BF16)
        vwb_ref[...] = vw_ref[0].astype(BF16)

    q = q_ref[0]
    qs = jnp.concatenate([q[:, r * HEAD_DIM:(r + 1) * HEAD_DIM] for r in range(GROUP)], axis=0)
    qb = (qs * scale).astype(BF16)
    qa_ref[:, 0:HEAD_DIM] = qb
    t_q = i * PQ + lax.broadcasted_iota(jnp.int32, (PQ, 1), 0)
    t_rows = jnp.concatenate([t_q] * GROUP, axis=0)

    kc = kc_ref[0, 0, 0]
    n_cmp = kc.shape[0]
    s = _dot_nt(qb, kc)
    cmp_end = lax.broadcasted_iota(jnp.int32, (rows, n_cmp), 1) * CMP_STRIDE + (L_CMP - 1)
    p_cmp = _masked_softmax(s, cmp_end <= t_rows)
    o_cmp = _dot(p_cmp.astype(BF16), vc_ref[0, 0, 0])

    w0 = pl.multiple_of(jnp.clip(i * PQ - WINDOW, 0, seq - WIN_SPAN), PQ)
    s = _dot_nt(qb, kwb_ref[pl.ds(w0, WIN_SPAN), :])
    wpos = w0 + lax.broadcasted_iota(jnp.int32, (rows, WIN_SPAN), 1)
    p_win = _masked_softmax(s, (wpos <= t_rows) & (wpos > t_rows - WINDOW))
    o_win = _dot(p_win.astype(BF16), vwb_ref[pl.ds(w0, WIN_SPAN), :])

    p_sum = p_cmp[0:PQ]
    for r in range(1, GROUP):
        p_sum = p_sum + p_cmp[r * PQ:(r + 1) * PQ]
    t_lane = i * PQ + lax.broadcasted_iota(jnp.int32, (1, PQ), 1)
    sel_t = _select_blocks_t(_importance_t(p_sum.T, n_sel), t_lane)
    pen_t = jnp.concatenate([jnp.where(sel_t, 0.0, NEG), jnp.zeros((P_SEL_PAD - n_sel, PQ), F32)], axis=0)
    pen = pen_t.T.astype(BF16)
    for r in range(GROUP):
        qa_ref[r * PQ:(r + 1) * PQ, HEAD_DIM:] = pen

    m_ref[...] = jnp.full(m_ref.shape, NEG, F32)
    l_ref[...] = jnp.zeros(l_ref.shape, F32)
    acc_ref[...] = jnp.zeros(acc_ref.shape, F32)

    def slc_tile(kt, causal):
        k0 = pl.multiple_of(kt * PK, PK)
        kaug = kaug_ref[pl.ds(k0, PK), :]
        v = vsb_ref[pl.ds(k0, PK), :]
        for h in range(2):
            hs = pl.ds(h * half, half)
            s = _dot_nt(qa_ref[hs, :], kaug)
            if causal:
                kpos = k0 + lax.broadcasted_iota(jnp.int32, (half, PK), 1)
                s = jnp.where(kpos <= t_rows[h * half:(h + 1) * half], s, NEG)
            _flash_update(s, v, m_ref.at[hs], l_ref.at[hs], acc_ref.at[hs])

    n_full = (i * PQ) // PK

    def full_tile(kt, carry):
        slc_tile(kt, False)
        return carry

    lax.fori_loop(0, n_full, full_tile, 0)
    slc_tile(n_full, True)
    o_slc = acc_ref[...] / l_ref[...]

    per_group = GROUP * N_BRANCH
    gate = pltpu.roll(_sigmoid(gl_ref[0]), (LANES - pl.program_id(1) * per_group) % LANES, 1)
    for r in range(GROUP):
        rs = slice(r * PQ, (r + 1) * PQ)
        c0 = r * N_BRANCH
        o = (gate[:, c0:c0 + 1] * o_cmp[rs] + gate[:, c0 + 1:c0 + 2] * o_slc[rs]
             + gate[:, c0 + 2:c0 + 3] * o_win[rs])
        o_ref[0, :, r * HEAD_DIM:(r + 1) * HEAD_DIM] = o.astype(o_ref.dtype)


def nsa_prompt(qkv, cblk):
    b, t, _ = qkv.shape
    qcol = ATTN_W // HEAD_DIM

    def kv_spec(slot):
        return pl.BlockSpec((1, t, HEAD_DIM), lambda bi, g, i: (bi, 0, qcol + slot * N_KV + g))

    n_cmp = cblk.shape[3]
    rows = GROUP * PQ
    return pl.pallas_call(
        functools.partial(_nsa_prompt_body, seq=t),
        grid=(b, N_KV, t // PQ),
        in_specs=[pl.BlockSpec((1, PQ, GROUP * HEAD_DIM), lambda bi, g, i: (bi, i, g)),
                  kv_spec(2), kv_spec(3), kv_spec(4), kv_spec(5),
                  pl.BlockSpec((1, 1, 1, n_cmp, HEAD_DIM), lambda bi, g, i: (bi, 0, g, 0, 0)),
                  pl.BlockSpec((1, 1, 1, n_cmp, HEAD_DIM), lambda bi, g, i: (bi, 1, g, 0, 0)),
                  pl.BlockSpec((1, PQ, LANES), lambda bi, g, i: (bi, i, QKV_W // LANES))],
        out_specs=pl.BlockSpec((1, PQ, GROUP * HEAD_DIM), lambda bi, g, i: (bi, i, g)),
        out_shape=jax.ShapeDtypeStruct((b, t, ATTN_W), BF16),
        scratch_shapes=[pltpu.VMEM((t, 2 * HEAD_DIM), BF16), pltpu.VMEM((t, HEAD_DIM), BF16),
                        pltpu.VMEM((t, HEAD_DIM), BF16), pltpu.VMEM((t, HEAD_DIM), BF16),
                        pltpu.VMEM((rows, 2 * HEAD_DIM), BF16), pltpu.VMEM((rows, 1), F32), pltpu.VMEM((rows, 1), F32),
                        pltpu.VMEM((rows, HEAD_DIM), F32)],
        compiler_params=_params(("arbitrary",) * 3, 40 << 20),
    )(qkv, qkv, qkv, qkv, qkv, cblk, cblk, qkv)


SP = 16
S_KEYS = SP * PAGE_SIZE
S_BLKS = S_KEYS // L_SEL
S_CHUNK = LANES // S_BLKS
S_ROWS = GROUP * S_PAD


def _nsa_sample_body(pt_ref, *refs, t_new, n_steps, n_sel_pad):
    pages = refs[:SP]
    (q_ref, cb_ref, win_ref, gl_ref, o_ref,
     qb_ref, pen_ref, new_ref, ocw_ref, m_ref, l_ref, acc_ref) = refs[SP:]
    i = pl.program_id(1)
    scale = HEAD_DIM ** -0.5
    n_chunks = n_sel_pad // LANES
    n_win = win_ref.shape[1] // (2 * N_KV)
    tok = lax.broadcasted_iota(jnp.int32, (S_PAD, 1), 0)
    t_q = PAST_LEN + tok
    t_rows = jnp.concatenate([t_q] * GROUP, axis=0)
    qcol = ATTN_W

    @pl.when(i == 0)
    def _():
        pad = jnp.zeros((LANES - S_PAD, HEAD_DIM), F32)
        for g in range(N_KV):
            heads = [q_ref[0, :, (g * GROUP + r) * HEAD_DIM:(g * GROUP + r + 1) * HEAD_DIM]
                     for r in range(GROUP)]
            qb_ref[g] = (jnp.concatenate(heads, axis=0) * scale).astype(BF16)
            for slot in range(2, 6):
                c0 = qcol + (slot * N_KV + g) * HEAD_DIM
                new_ref[slot - 2, g] = jnp.concatenate(
                    [q_ref[0, :, c0:c0 + HEAD_DIM], pad], axis=0).astype(BF16)
        m_ref[...] = jnp.full(m_ref.shape, NEG, F32)
        l_ref[...] = jnp.zeros(l_ref.shape, F32)
        acc_ref[...] = jnp.zeros(acc_ref.shape, F32)
        p_sums = []
        for g in range(N_KV):
            qb = qb_ref[g]
            kc = cb_ref[0, 0, g]
            n_cmp = kc.shape[0]
            s = _dot_nt(qb, kc)
            cmp_end = lax.broadcasted_iota(jnp.int32, (S_ROWS, n_cmp), 1) * CMP_STRIDE + (L_CMP - 1)
            p_cmp = _masked_softmax(s, cmp_end <= t_rows)
            ocw_ref[0, g] = _dot(p_cmp.astype(BF16), cb_ref[0, 1, g])
            p_sum = p_cmp[0:S_PAD]
            for r in range(1, GROUP):
                p_sum = p_sum + p_cmp[r * S_PAD:(r + 1) * S_PAD]
            p_sums.append(p_sum)
            kw = win_ref[0, pl.ds(g, n_win, stride=2 * N_KV), :].astype(BF16)
            vw = win_ref[0, pl.ds(N_KV + g, n_win, stride=2 * N_KV), :].astype(BF16)
            s = jnp.concatenate([_dot_nt(qb, kw), _dot_nt(qb, new_ref[2, g])], axis=1)
            lane = lax.broadcasted_iota(jnp.int32, (S_ROWS, n_win + LANES), 1)
            wpos = PAST_LEN - n_win + lane
            mask = (wpos <= t_rows) & (wpos > t_rows - WINDOW) & (lane < n_win + t_new)
            p_win = _masked_softmax(s, mask).astype(BF16)
            ocw_ref[1, g] = _dot(p_win[:, 0:n_win], vw) + _dot(p_win[:, n_win:], new_ref[3, g])
        imp = _importance(jnp.concatenate(p_sums, axis=0), n_sel_pad)
        sel = _select_blocks(imp, jnp.concatenate([t_q] * N_KV, axis=0))
        pen = jnp.where(sel, 0.0, NEG).astype(BF16)
        for g in range(N_KV):
            for ch in range(n_chunks):
                blk = pen[g * S_PAD:(g + 1) * S_PAD, ch * LANES:(ch + 1) * LANES]
                pen_ref[ch, g] = jnp.concatenate([blk] * GROUP, axis=0)

    key = lax.broadcasted_iota(jnp.int32, (S_KEYS, LANES), 0)
    blk = lax.broadcasted_iota(jnp.int32, (S_KEYS, LANES), 1)
    ind = jnp.where((i % S_CHUNK) * S_BLKS + key // L_SEL == blk, 1.0, 0.0).astype(BF16)
    for g in range(N_KV):
        k = jnp.concatenate(
            [p[0, pl.ds(g, PAGE_SIZE, stride=2 * N_KV), :] for p in pages], axis=0)
        v = jnp.concatenate(
            [p[0, pl.ds(N_KV + g, PAGE_SIZE, stride=2 * N_KV), :] for p in pages], axis=0)
        kaug = jnp.concatenate([k.astype(BF16), ind], axis=1)
        qa = jnp.concatenate([qb_ref[g], pen_ref[i // S_CHUNK, g]], axis=1)
        _flash_update(_dot_nt(qa, kaug), v.astype(BF16), m_ref.at[g], l_ref.at[g], acc_ref.at[g])

    @pl.when(i == n_steps - 1)
    def _():
        gate = _sigmoid(gl_ref[0])
        lane = lax.broadcasted_iota(jnp.int32, (S_ROWS, LANES), 1)
        cur_chunk, cur_lane = (PAST_LEN // L_SEL) // LANES, (PAST_LEN // L_SEL) % LANES
        for g in range(N_KV):
            pen_cur = pen_ref[cur_chunk, g][:, cur_lane:cur_lane + 1].astype(F32)
            s = _dot_nt(qb_ref[g], new_ref[0, g]) + pen_cur
            s = jnp.where((PAST_LEN + lane <= t_rows) & (lane < t_new), s, NEG)
            _flash_update(s, new_ref[1, g], m_ref.at[g], l_ref.at[g], acc_ref.at[g])
            o_slc = acc_ref[g] / l_ref[g]
            for r in range(GROUP):
                rs = slice(r * S_PAD, (r + 1) * S_PAD)
                c0 = (g * GROUP + r) * N_BRANCH
                o = (gate[:, c0:c0 + 1] * ocw_ref[0, g][rs] + gate[:, c0 + 1:c0 + 2] * o_slc[rs]
                     + gate[:, c0 + 2:c0 + 3] * ocw_ref[1, g][rs])
                h = g * GROUP + r
                o_ref[0, :, h * HEAD_DIM:(h + 1) * HEAD_DIM] = o.astype(o_ref.dtype)


def nsa_sample(qkv, t_new, cblk, cache_slc, cache_win, page_table, gl):
    b = qkv.shape[0]
    n_pages = page_table.shape[1]
    n_steps = n_pages // SP
    n_cmp = cblk.shape[3]
    n_win = cache_win.shape[1] // (2 * N_KV)
    n_sel = -(-(PAST_LEN + t_new) // L_SEL)
    n_sel_pad = -(-n_sel // LANES) * LANES
    assert n_steps * S_BLKS <= n_sel_pad and n_steps * SP == n_pages

    def page_spec(c):
        return pl.BlockSpec((1, PAGE_SIZE * 2 * N_KV, HEAD_DIM),
                            lambda bi, i, pt: (pt[bi, i * SP + c], 0, 0))

    grid_spec = pltpu.PrefetchScalarGridSpec(
        num_scalar_prefetch=1, grid=(b, n_steps),
        in_specs=[page_spec(c) for c in range(SP)] + [
            pl.BlockSpec((1, S_PAD, QKV_W), lambda bi, i, pt: (bi, 0, 0)),
            pl.BlockSpec((1, 2, N_KV, n_cmp, HEAD_DIM), lambda bi, i, pt: (bi, 0, 0, 0, 0)),
            pl.BlockSpec((1, n_win * 2 * N_KV, HEAD_DIM), lambda bi, i, pt: (bi, 0, 0)),
            pl.BlockSpec((1, S_PAD, LANES), lambda bi, i, pt: (bi, 0, 0))],
        out_specs=pl.BlockSpec((1, S_PAD, ATTN_W), lambda bi, i, pt: (bi, 0, 0)),
        scratch_shapes=[pltpu.VMEM((N_KV, S_ROWS, HEAD_DIM), BF16),
                        pltpu.VMEM((n_sel_pad // LANES, N_KV, S_ROWS, LANES), BF16),
                        pltpu.VMEM((4, N_KV, LANES, HEAD_DIM), BF16),
                        pltpu.VMEM((2, N_KV, S_ROWS, HEAD_DIM), F32),
                        pltpu.VMEM((N_KV, S_ROWS, 1), F32), pltpu.VMEM((N_KV, S_ROWS, 1), F32),
                        pltpu.VMEM((N_KV, S_ROWS, HEAD_DIM), F32)])
    return pl.pallas_call(
        functools.partial(_nsa_sample_body, t_new=t_new, n_steps=n_steps, n_sel_pad=n_sel_pad),
        grid_spec=grid_spec,
        out_shape=jax.ShapeDtypeStruct((b, S_PAD, ATTN_W), F32),
        compiler_params=_params(("arbitrary", "arbitrary"), 40 << 20),
    )(page_table, *([cache_slc] * SP), qkv, cblk, cache_win, gl)


def _kv_rows_body(x_ref, o_ref):
    tt = x_ref.shape[1]
    for kvg in range(2 * N_KV):
        o_ref[0, pl.ds(kvg, tt, stride=2 * N_KV), :] = x_ref[0, :, kvg * HEAD_DIM:(kvg + 1) * HEAD_DIM]


def kv_rows(qkv, slot, t0, tt):
    b, t, _ = qkv.shape
    n = (t - t0) // tt
    assert n * tt == t - t0 and t0 % tt == 0
    col_blk = (ATTN_W + slot * KV_W) // PAGE_ROW_W
    out = pl.pallas_call(
        _kv_rows_body, grid=(b, n),
        in_specs=[pl.BlockSpec((1, tt, PAGE_ROW_W), lambda bi, i: (bi, t0 // tt + i, col_blk))],
        out_specs=pl.BlockSpec((1, tt * 2 * N_KV, HEAD_DIM), lambda bi, i: (bi, i, 0)),
        out_shape=jax.ShapeDtypeStruct((b, (t - t0) * 2 * N_KV, HEAD_DIM), F32),
        compiler_params=_params(("arbitrary", "arbitrary"), 6 * tt * PAGE_ROW_W * 4),
    )(qkv)
    return out.reshape(b, t - t0, 2, N_KV, HEAD_DIM)


def _pad_rows(x, front, total):
    return jnp.pad(x, ((0, 0), (front, total - front - x.shape[1]), (0, 0)))


def kernel(x_prompt, x_sample, cache_cmp_kv, cache_slc_kv, cache_win_kv, state_conf_conv,
           state_short_conv, page_table, p_prompt, p_sample, g_mix0, w_in0, conv_w0, conv_b0,
           ln_g0, ln_b0, cmp_pe, cmp_w1, cmp_b1, cmp_w2, w_out0, g_mix1, w_in1, sconv_w1, w_out1,
           g_ffn, w_ffn_gate, w_ffn_up, w_ffn_down, g_ple, w_ple_gate, w_ple_proj, g_final):
    bp, seq, d = x_prompt.shape
    bs, t_new, _ = x_sample.shape
    mp, ms = bp * seq, bs * t_new
    d_ff = w_ffn_gate.shape[-1]
    n_pool = cache_cmp_kv.shape[1]
    tm = 1024

    xp = x_prompt.reshape(mp, d)
    xs = x_sample.reshape(ms, d)
    rope_p = rope_tables(jnp.arange(seq, dtype=jnp.int32))
    rope_s = tuple(jnp.tile(tb, (bs, 1)) for tb in rope_tables(PAST_LEN + jnp.arange(t_new, dtype=jnp.int32)))
    outs = {}
    w_down_bf16 = w_ffn_down.astype(BF16)

    def residual(groups, x, tm_, tn_, single_buffer_rows=False):
        res_p, res_s = fused_mm(groups, _ep_residual, [F32], d, tm_, tn_, tilex=[x], m_outer=True,
                                emit_bf16=True, emit_ssq=True, single_buffer_rows=single_buffer_rows)
        return tuple(zip(res_p, res_s))

    def normed(stream, gain, ws):
        (_, _), (b_p, b_s), ssq = stream
        return Lhs(b_p, b_s, ws, gain=gain, ssq=ssq)

    def ffn_ple(stream, i, last):
        (gp,), (gs,) = fused_mm([normed(stream, g_ffn[i], [W(w_ffn_gate, i), W(w_ffn_up, i)])],
                                _ep_swiglu, [BF16], d_ff, 2 * tm, 256, m_outer=True, single_buffer_rows=True)
        stream = residual([Lhs(gp, gs, [W(w_down_bf16, i)])], stream[0], tm, 256, single_buffer_rows=True)
        pp = p_prompt[i].reshape(mp, -1).astype(BF16)
        ps = p_sample[i].reshape(ms, -1).astype(BF16)
        res_p, res_s = fused_mm([normed(stream, g_ple[i], [W(w_ple_gate, i)]), Lhs(pp, ps, [W(w_ple_proj, i)])],
                                _ep_ple, [F32], d, tm, 512, tilex=[stream[0]], m_outer=True,
                                emit_bf16=not last, emit_ssq=not last)
        return tuple(zip(res_p, res_s))

    e = 0
    hp, hs = rmsnorm(xp, g_mix0[e], BF16), rmsnorm(xs, g_mix0[e], BF16)
    w_in0_t = jnp.swapaxes(w_in0, 1, 2)
    win = lambda col_off: W(w_in0_t, e, col_off=col_off, transposed=True)
    (up,), (us,) = fused_mm([Lhs(hp, hs, [win(0), win(CONF_CH // 256)])], _ep_glu,
                            [F32], CONF_CH, tm, 256)
    (qkv_p,), (qkv_s,) = fused_mm([Lhs(hp, hs, [win(2 * CONF_CH // KV_W)])], _ep_rope,
                                  [F32], QKVG_W, tm, KV_W, rowx=list(zip(rope_p, rope_s)))

    up3, us3 = up.reshape(bp, seq, CONF_CH), us.reshape(bs, t_new, CONF_CH)
    conf_w = (conv_w0[e], conv_b0[e], ln_g0[e], ln_b0[e])
    a_p = conf_conv(up3, jnp.zeros((bp, HALO, CONF_CH), F32), *conf_w, tt=128, out_dtype=BF16)
    st = state_conf_conv[e]
    a_s = conf_conv(_pad_rows(us3, 0, S_PAD), _pad_rows(st, HALO - st.shape[1], HALO), *conf_w,
                    tt=S_PAD, out_dtype=F32)
    outs["conf_p"] = up3[:, seq - (CONF_W - 1):]
    outs["conf_s"] = jnp.concatenate([st, us3], axis=1)[:, t_new:]

    qkv_p3, qkv_s3 = qkv_p.reshape(bp, seq, QKVG_W), qkv_s.reshape(bs, t_new, QKVG_W)
    kv_shape = lambda x: x.reshape(x.shape[0], x.shape[1], 2, N_KV, HEAD_DIM)
    for name, slot in (("cmp", 0), ("slc", 2), ("win", 4)):
        c0 = ATTN_W + slot * KV_W
        t0 = seq - min(WINDOW, seq) if name == "win" else 0
        outs[name + "_p"] = kv_rows(qkv_p3, slot, t0, 512)
        outs[name + "_s"] = kv_shape(qkv_s3[:, :, c0:c0 + 2 * KV_W])
    outs["win_s"] = jnp.concatenate([cache_win_kv[e], outs["win_s"]], axis=1)[:, t_new:]

    pages_p = seq // PAGE_SIZE
    ident = jnp.arange(bp * pages_p, dtype=jnp.int32).reshape(bp, pages_p)
    w1cat = cmp_w1_cat(cmp_w1[e])
    proj_p = cmp_project(qkv_p.reshape(bp * pages_p, PAGE_SIZE, QKVG_W), False, ATTN_W // PAGE_ROW_W,
                         ident, w1cat)
    row_view = lambda c: c.reshape(-1, c.shape[-4] * 2 * N_KV, HEAD_DIM)
    pool_pages = page_table + e * n_pool
    proj_s = cmp_project(row_view(cache_cmp_kv), True, 0, pool_pages, w1cat)
    cblk_p = cmp_finish(proj_p, cmp_pe[e], w1cat, cmp_b1[e], cmp_w2[e])
    cblk_s = cmp_finish(proj_s, cmp_pe[e], w1cat, cmp_b1[e], cmp_w2[e])

    attn_p = nsa_prompt(qkv_p3, cblk_p)
    gl_s3 = jnp.pad(qkv_s3[:, :, QKV_W:QKV_W + LANES], ((0, 0), (0, S_PAD - t_new), (0, 0)))
    attn_s = nsa_sample(_pad_rows(qkv_s3[:, :, :QKV_W], 0, S_PAD), t_new, cblk_s, row_view(cache_slc_kv),
                        row_view(cache_win_kv[e]), pool_pages, gl_s3)
    a_p2, a_s2 = a_p.reshape(mp, CONF_CH), a_s[:, :t_new].reshape(ms, CONF_CH).astype(BF16)
    at_p2, at_s2 = attn_p.reshape(mp, ATTN_W), attn_s[:, :t_new].reshape(ms, ATTN_W).astype(BF16)
    stream = residual(
        [Lhs(a_p2, a_s2, [W(w_out0, e, k_rows=CONF_CH, row_blk=0)]),
         Lhs(at_p2, at_s2, [W(w_out0, e, k_rows=ATTN_W, row_blk=CONF_CH // ATTN_W)])],
        (xp, xs), tm, 512)
    stream = ffn_ple(stream, 0, last=False)

    o = 0
    (bg_p, cv_p), (bg_s, cv_s) = fused_mm(
        [normed(stream, g_mix1[o], [W(w_in1, o), W(w_in1, o, col_off=SC_CH // 256),
                                    W(w_in1, o, col_off=2 * SC_CH // 256)])],
        _ep_shortconv, [F32, F32], SC_CH, 512, 256)
    cv_p3, cv_s3 = cv_p.reshape(bp, seq, SC_CH), cv_s.reshape(bs, t_new, SC_CH)
    y_p = short_conv(cv_p3, bg_p.reshape(bp, seq, SC_CH), jnp.zeros((bp, SUBLANES, SC_CH), F32),
                     sconv_w1[o], tt=256, out_dtype=BF16)
    st = state_short_conv[o]
    y_s = short_conv(_pad_rows(cv_s3, 0, S_PAD), _pad_rows(bg_s.reshape(bs, t_new, SC_CH), 0, S_PAD),
                     _pad_rows(st, SUBLANES - st.shape[1], SUBLANES), sconv_w1[o], tt=S_PAD,
                     out_dtype=F32)
    outs["sc_p"] = cv_p3[:, seq - (SC_W - 1):]
    outs["sc_s"] = jnp.concatenate([st, cv_s3], axis=1)[:, t_new:]
    stream = residual(
        [Lhs(y_p.reshape(mp, SC_CH), y_s[:, :t_new].reshape(ms, SC_CH).astype(BF16), [W(w_out1, o)])],
        stream[0], tm, 512)
    (xp, xs), = ffn_ple(stream, 1, last=True)

    y_p = rmsnorm(xp, g_final, F32).reshape(bp, seq, d)
    y_s = rmsnorm(xs, g_final, F32).reshape(bs, t_new, d)
    st1 = lambda x: x[None]
    return (y_p, y_s, st1(outs["cmp_p"]), st1(outs["cmp_s"]), st1(outs["slc_p"]), st1(outs["slc_s"]),
            st1(outs["win_p"]), st1(outs["win_s"]), st1(outs["conf_p"]), st1(outs["conf_s"]),
            st1(outs["sc_p"]), st1(outs["sc_s"]))
```
